```python
import math
import jax, jax.numpy as jnp
from jax import lax
import numpy as np

D_MODEL = 1024
BATCH = 8
SEQ = 2048
DEPTH = 4

D_MIX = D_MODEL
N_MIXERS = 4
GROUP_W = D_MIX // N_MIXERS
CONF_KERNEL = 31
CONF_GROUPS = 4
GN_EPS = 1e-5
SCONV_KERNEL = 3
ATT_HEADS = 4
HEAD_DIM = GROUP_W // ATT_HEADS
MOBA_BLOCK = 256
MOBA_TOP_K = 3
Q_BLOCK = 128
LRU_BLOCKS = 4
LRU_BLOCK_W = GROUP_W // LRU_BLOCKS
LRU_CONV = 4
LRU_C = 8.0
D_FF = 4 * D_MODEL
RMS_EPS = 1e-6
CONF_COLS = 2 * GROUP_W
SC_COLS = 3 * GROUP_W
ATT_COLS = 3 * GROUP_W
LRU_COLS = 2 * GROUP_W
IN_COLS = CONF_COLS + SC_COLS + ATT_COLS + LRU_COLS

kernel_name = "hybrid_parallel_groups_moba_rglru_conv"


def rms_norm(x, g):
    xf = x.astype(jnp.float32)
    y = xf * lax.rsqrt(jnp.mean(xf * xf, axis=-1, keepdims=True) + RMS_EPS)
    return (y * g.astype(jnp.float32)).astype(x.dtype)


def causal_depthwise_conv(x, w):
    k_w, c = w.shape
    return lax.conv_general_dilated(
        x, w[:, None, :].astype(x.dtype), window_strides=(1,), padding=[(k_w - 1, 0)],
        dimension_numbers=("NWC", "WIO", "NWC"), feature_group_count=c)


def group_norm_channels(x, g, b):
    bsz, s, c = x.shape
    xf = x.astype(jnp.float32).reshape(bsz, s, CONF_GROUPS, c // CONF_GROUPS)
    mu = jnp.mean(xf, axis=-1, keepdims=True)
    var = jnp.mean(jnp.square(xf - mu), axis=-1, keepdims=True)
    y = ((xf - mu) * lax.rsqrt(var + GN_EPS)).reshape(bsz, s, c)
    return (y * g.astype(jnp.float32) + b.astype(jnp.float32)).astype(x.dtype)


def conformer_conv_mixer(u, dw_w, dw_b, gn_g, gn_b):
    val, gate = jnp.split(u, 2, axis=-1)
    z = val * jax.nn.sigmoid(gate)
    z = causal_depthwise_conv(z, dw_w) + dw_b
    z = group_norm_channels(z, gn_g, gn_b)
    return jax.nn.silu(z)


def short_conv_mixer(u, conv_w):
    b_gate, c_gate, xs = jnp.split(u, 3, axis=-1)
    return b_gate * causal_depthwise_conv(c_gate * xs, conv_w)


def block_diag(x, w, b):
    bsz, s, c = x.shape
    xg = x.reshape(bsz, s, LRU_BLOCKS, LRU_BLOCK_W)
    return jnp.einsum("bsgi,gij->bsgj", xg, w).reshape(bsz, s, c) + b


def rglru_mixer(u, conv_w, conv_b, wa, ba, wx, bx, lam):
    xr, gate = jnp.split(u, 2, axis=-1)
    xc = causal_depthwise_conv(xr, conv_w) + conv_b
    r = jax.nn.sigmoid(block_diag(xc, wa, ba).astype(jnp.float32))
    i = jax.nn.sigmoid(block_diag(xc, wx, bx).astype(jnp.float32))
    log_a = -LRU_C * r * jax.nn.softplus(-lam.astype(jnp.float32))
    a = jnp.exp(log_a)
    mult = jnp.sqrt(-jnp.expm1(2.0 * log_a))
    bterm = mult * (i * xc.astype(jnp.float32))

    def combine(e1, e2):
        a1, b1 = e1
        a2, b2 = e2
        return a1 * a2, a2 * b1 + b2

    _, h = lax.associative_scan(combine, (a, bterm), axis=1)
    return (h * jax.nn.gelu(gate.astype(jnp.float32))).astype(u.dtype)


def moba_attention(u):
    bsz, s, _ = u.shape
    q, k, v = jnp.split(u.astype(jnp.float32), 3, axis=-1)
    q = q.reshape(bsz, s, ATT_HEADS, HEAD_DIM).transpose(0, 2, 1, 3) * (HEAD_DIM ** -0.5)
    k = k.reshape(bsz, s, ATT_HEADS, HEAD_DIM)
    v = v.reshape(bsz, s, ATT_HEADS, HEAD_DIM)
    nb = -(-s // MOBA_BLOCK)
    pad = nb * MOBA_BLOCK - s
    kp = jnp.pad(k, ((0, 0), (0, pad), (0, 0), (0, 0)))
    vp = jnp.pad(v, ((0, 0), (0, pad), (0, 0), (0, 0)))
    kb = kp.reshape(bsz, nb, MOBA_BLOCK, ATT_HEADS, HEAD_DIM).transpose(0, 3, 1, 2, 4)
    vb = vp.reshape(bsz, nb, MOBA_BLOCK, ATT_HEADS, HEAD_DIM).transpose(0, 3, 1, 2, 4)
    slopes = 2.0 ** (-8.0 * jnp.arange(1, ATT_HEADS + 1, dtype=jnp.float32) / ATT_HEADS)

    kmean = jnp.mean(kb, axis=3)
    gate = jnp.einsum("bhsd,bhnd->bhsn", q, kmean)
    q_blk = jnp.arange(s) // MOBA_BLOCK
    past = jnp.arange(nb)[None, :] < q_blk[:, None]
    gate = jnp.where(past, gate, -jnp.inf)
    topk = max(min(MOBA_TOP_K, nb - 1), 1)
    _, idx = lax.top_k(gate, topk)

    nq = s // Q_BLOCK
    q_c = q.reshape(bsz, ATT_HEADS, nq, Q_BLOCK, HEAD_DIM).transpose(0, 2, 1, 3, 4)
    q_c = q_c.reshape(bsz * nq, ATT_HEADS, Q_BLOCK, HEAD_DIM)
    i_c = idx.reshape(bsz, ATT_HEADS, nq, Q_BLOCK, topk).transpose(0, 2, 1, 3, 4)
    i_c = i_c.reshape(bsz * nq, ATT_HEADS, Q_BLOCK, topk)
    b_ids = jnp.repeat(jnp.arange(bsz), nq)
    n_ids = jnp.tile(jnp.arange(nq), bsz)
    hh = jnp.arange(ATT_HEADS)[:, None, None]
    offs = jnp.arange(MOBA_BLOCK)

    def one_block(args):
        qc, ic, b, n = args
        kbb = kb[b]
        vbb = vb[b]
        t = n * Q_BLOCK + jnp.arange(Q_BLOCK)
        own = (n * Q_BLOCK) // MOBA_BLOCK
        k_own = kbb[:, own]
        v_own = vbb[:, own]
        s_pos = own * MOBA_BLOCK + offs
        d_own = (t[:, None] - s_pos[None, :]).astype(jnp.float32)
        s_own = jnp.einsum("hqd,hkd->hqk", qc, k_own) - slopes[:, None, None] * d_own
        s_own = jnp.where(d_own >= 0, s_own, -jnp.inf)
        kg = kbb[hh, ic]
        vg = vbb[hh, ic]
        sel_pos = ic[..., None] * MOBA_BLOCK + offs
        d_sel = (t[None, :, None, None] - sel_pos).astype(jnp.float32)
        s_sel = jnp.einsum("hqd,hqjkd->hqjk", qc, kg) - slopes[:, None, None, None] * d_sel
        s_sel = jnp.where((ic < own)[..., None], s_sel, -jnp.inf)
        scores = jnp.concatenate([s_sel.reshape(ATT_HEADS, Q_BLOCK, topk * MOBA_BLOCK), s_own], axis=-1)
        p = jax.nn.softmax(scores, axis=-1)
        p_sel = p[..., : topk * MOBA_BLOCK].reshape(ATT_HEADS, Q_BLOCK, topk, MOBA_BLOCK)
        p_own = p[..., topk * MOBA_BLOCK:]
        return (jnp.einsum("hqjk,hqjkd->hqd", p_sel, vg)
                + jnp.einsum("hqk,hkd->hqd", p_own, v_own))

    out = lax.map(one_block, (q_c, i_c, b_ids, n_ids))
    out = out.reshape(bsz, nq, ATT_HEADS, Q_BLOCK, HEAD_DIM).transpose(0, 1, 3, 2, 4)
    return out.reshape(bsz, s, GROUP_W).astype(u.dtype)


def setup_inputs(seed: int = 0) -> dict:
    key = jax.random.key(seed)
    ks = jax.random.split(key, 24)
    f32 = jnp.float32

    def nrm(k, shape, scale):
        return jax.random.normal(k, shape, f32) * scale

    def gain(k, shape):
        return 1.0 + 0.05 * jax.random.normal(k, shape, f32)

    u = jax.random.uniform(ks[20], (DEPTH, GROUP_W), f32, 0.9, 0.999)
    base = u ** (1.0 / LRU_C)
    lru_lam = jnp.log(base) - jnp.log1p(-base)
    return {
        "x": jax.random.normal(ks[0], (BATCH, SEQ, D_MODEL), f32),
        "pre_mix_g": gain(ks[1], (DEPTH, D_MODEL)),
        "w_in": nrm(ks[2], (DEPTH, D_MODEL, IN_COLS), D_MODEL ** -0.5),
        "conf_dw_w": nrm(ks[3], (DEPTH, CONF_KERNEL, GROUP_W), CONF_KERNEL ** -0.5),
        "conf_dw_b": nrm(ks[4], (DEPTH, GROUP_W), 0.02),
        "conf_gn_g": gain(ks[5], (DEPTH, GROUP_W)),
        "conf_gn_b": nrm(ks[6], (DEPTH, GROUP_W), 0.02),
        "sconv_w": nrm(ks[7], (DEPTH, SCONV_KERNEL, GROUP_W), SCONV_KERNEL ** -0.5),
        "lru_conv_w": nrm(ks[8], (DEPTH, LRU_CONV, GROUP_W), LRU_CONV ** -0.5),
        "lru_conv_b": nrm(ks[9], (DEPTH, GROUP_W), 0.02),
        "lru_wa": nrm(ks[10], (DEPTH, LRU_BLOCKS, LRU_BLOCK_W, LRU_BLOCK_W), LRU_BLOCK_W ** -0.5),
        "lru_ba": nrm(ks[11], (DEPTH, GROUP_W), 0.02),
        "lru_wx": nrm(ks[12], (DEPTH, LRU_BLOCKS, LRU_BLOCK_W, LRU_BLOCK_W), LRU_BLOCK_W ** -0.5),
        "lru_bx": nrm(ks[13], (DEPTH, GROUP_W), 0.02),
        "lru_lam": lru_lam,
        "w_out": nrm(ks[14], (DEPTH, D_MIX, D_MODEL), D_MIX ** -0.5),
        "post_mix_g": gain(ks[15], (DEPTH, D_MODEL)),
        "pre_mlp_g": gain(ks[16], (DEPTH, D_MODEL)),
        "mlp_w1": nrm(ks[17], (DEPTH, D_MODEL, D_FF), D_MODEL ** -0.5),
        "mlp_w2": nrm(ks[18], (DEPTH, D_FF, D_MODEL), D_FF ** -0.5),
        "post_mlp_g": gain(ks[19], (DEPTH, D_MODEL)),
    }


def reference(x, pre_mix_g, w_in, conf_dw_w, conf_dw_b, conf_gn_g, conf_gn_b, sconv_w,
              lru_conv_w, lru_conv_b, lru_wa, lru_ba, lru_wx, lru_bx, lru_lam, w_out,
              post_mix_g, pre_mlp_g, mlp_w1, mlp_w2, post_mlp_g):
    splits = [CONF_COLS, CONF_COLS + SC_COLS, CONF_COLS + SC_COLS + ATT_COLS]
    for l in range(DEPTH):
        h = rms_norm(x, pre_mix_g[l])
        u = h @ w_in[l]
        u_conf, u_sc, u_att, u_lru = jnp.split(u, splits, axis=-1)
        y_a = conformer_conv_mixer(u_conf, conf_dw_w[l], conf_dw_b[l], conf_gn_g[l], conf_gn_b[l])
        y_b = short_conv_mixer(u_sc, sconv_w[l])
        y_c = moba_attention(u_att)
        y_d = rglru_mixer(u_lru, lru_conv_w[l], lru_conv_b[l], lru_wa[l], lru_ba[l],
                          lru_wx[l], lru_bx[l], lru_lam[l])
        y = jnp.concatenate([y_a, y_b, y_c, y_d], axis=-1) @ w_out[l]
        x = x + rms_norm(y, post_mix_g[l])
        h = rms_norm(x, pre_mlp_g[l])
        m = jnp.square(jax.nn.relu(h @ mlp_w1[l])) @ mlp_w2[l]
        x = x + rms_norm(m, post_mlp_g[l])
    return x
```

```python
import functools
import math

import jax
import jax.numpy as jnp
from jax import lax
from jax.experimental import pallas as pl
from jax.experimental.pallas import tpu as pltpu

F32 = jnp.float32
BF16 = jnp.bfloat16

D_MODEL = 1024
GROUP_W = 256
CONF_KERNEL = 31
CONF_GROUPS = 4
GN_EPS = 1e-5
SCONV_KERNEL = 3
ATT_HEADS = 4
HEAD_DIM = GROUP_W // ATT_HEADS
MOBA_BLOCK = 256
MOBA_TOP_K = 3
LRU_BLOCKS = 4
LRU_CONV = 4
LRU_C = 8.0
D_FF = 4 * D_MODEL
RMS_EPS = 1e-6
IN_COLS = 10 * GROUP_W

TOKEN_TILE = 512
SEQ_TILE = 512
CONV_ROWS = 64
HALO_A = 32
HALO_S = 8
FF_CHUNK = 1024
VMEM_LIMIT = 56 * 1024 * 1024

NEG_INF = float("-inf")


def _rms(x, g):
    return x * lax.rsqrt(jnp.mean(x * x, axis=-1, keepdims=True) + RMS_EPS) * g


def _dot(a, b):
    return jnp.dot(a, b, preferred_element_type=F32)


def _in_proj_kernel(x_ref, g_ref, w_ref, u_ref):
    h = _rms(x_ref[...], g_ref[...]).astype(BF16)
    step = 2 * GROUP_W
    for c in range(0, IN_COLS, step):
        u_ref[:, c:c + step] = _dot(h, w_ref[:, c:c + step])


def _in_proj(x2, g, w, layer):
    tok = x2.shape[0]
    return pl.pallas_call(
        _in_proj_kernel,
        grid=(tok // TOKEN_TILE,),
        in_specs=[
            pl.BlockSpec((TOKEN_TILE, D_MODEL), lambda t: (t, 0)),
            pl.BlockSpec((None, 1, D_MODEL), lambda t: (layer, 0, 0)),
            pl.BlockSpec((None, D_MODEL, IN_COLS), lambda t: (layer, 0, 0)),
        ],
        out_specs=pl.BlockSpec((TOKEN_TILE, IN_COLS), lambda t: (t, 0)),
        out_shape=jax.ShapeDtypeStruct((tok, IN_COLS), F32),
        compiler_params=pltpu.CompilerParams(
            dimension_semantics=("parallel",), vmem_limit_bytes=VMEM_LIMIT),
        name="in_proj",
    )(x2, g, w)


def _group_mean(v, gm):
    hi = v.astype(BF16)
    lo = (v - hi.astype(F32)).astype(BF16)
    return _dot(hi, gm) + _dot(lo, gm)


def _causal_conv(buf, w_ref, halo, width, rows):
    acc = None
    for k in range(width):
        off = halo - (width - 1) + k
        term = w_ref[k:k + 1, :] * buf[off:off + rows, :]
        acc = term if acc is None else acc + term
    return acc


def _mixers_kernel(cv_ref, cg_ref, sb_ref, sc_ref, sx_ref, lr_ref, lg_ref,
                   dww_ref, dwb_ref, gng_ref, gnb_ref, scw_ref, lcw_ref, lcb_ref,
                   wax_ref, bax_ref, lam_ref, gm_ref,
                   ya_ref, yb_ref, yd_ref,
                   zbuf, cbuf, pbuf, xbuf, hcar):
    ts = SEQ_TILE

    @pl.when(pl.program_id(1) == 0)
    def _reset():
        zbuf[0:HALO_A, :] = jnp.zeros((HALO_A, GROUP_W), F32)
        pbuf[0:HALO_S, :] = jnp.zeros((HALO_S, GROUP_W), F32)
        xbuf[0:HALO_S, :] = jnp.zeros((HALO_S, GROUP_W), F32)
        hcar[...] = jnp.zeros_like(hcar)

    zbuf[HALO_A:HALO_A + ts, :] = cv_ref[...] * jax.nn.sigmoid(cg_ref[...])
    for r0 in range(0, ts, CONV_ROWS):
        acc = jnp.broadcast_to(dwb_ref[...], (CONV_ROWS, GROUP_W))
        for k in range(CONF_KERNEL):
            off = r0 + HALO_A - (CONF_KERNEL - 1) + k
            acc = acc + dww_ref[k:k + 1, :] * zbuf[off:off + CONV_ROWS, :]
        cbuf[r0:r0 + CONV_ROWS, :] = acc
    zbuf[0:HALO_A, :] = zbuf[ts:ts + HALO_A, :]
    c = cbuf[...]
    gm = gm_ref[...]
    d = c - _group_mean(c, gm)
    var = _group_mean(d * d, gm)
    yn = d * lax.rsqrt(var + GN_EPS) * gng_ref[...] + gnb_ref[...]
    ya_ref[...] = (yn * jax.nn.sigmoid(yn)).astype(ya_ref.dtype)

    pbuf[HALO_S:HALO_S + ts, :] = sc_ref[...] * sx_ref[...]
    conv_b = _causal_conv(pbuf, scw_ref, HALO_S, SCONV_KERNEL, ts)
    pbuf[0:HALO_S, :] = pbuf[ts:ts + HALO_S, :]
    yb_ref[...] = (sb_ref[...] * conv_b).astype(yb_ref.dtype)

    xbuf[HALO_S:HALO_S + ts, :] = lr_ref[...]
    xc = _causal_conv(xbuf, lcw_ref, HALO_S, LRU_CONV, ts) + lcb_ref[...]
    xbuf[0:HALO_S, :] = xbuf[ts:ts + HALO_S, :]
    gates = _dot(xc.astype(BF16), wax_ref[...]) + bax_ref[...]
    r = jax.nn.sigmoid(gates[:, :GROUP_W])
    i = jax.nn.sigmoid(gates[:, GROUP_W:])
    lam = lam_ref[...]
    softplus_neg_lam = jnp.maximum(-lam, 0.0) + jnp.log1p(jnp.exp(-jnp.abs(lam)))
    log_a = (-LRU_C * softplus_neg_lam) * r
    a = jnp.exp(log_a)
    th = jnp.tanh(log_a)
    b = jnp.sqrt(-2.0 * th / (1.0 - th)) * (i * xc)
    row = lax.broadcasted_iota(jnp.int32, (ts, GROUP_W), 0)
    dist = 1
    while dist < ts:
        keep = row >= dist
        a_prev = pltpu.roll(a, dist, axis=0)
        b_prev = pltpu.roll(b, dist, axis=0)
        b = jnp.where(keep, a * b_prev + b, b)
        a = jnp.where(keep, a * a_prev, a)
        dist *= 2
    h = a * hcar[0:1, :] + b
    hcar[0:1, :] = h[ts - 1:ts, :]
    g = lg_ref[...]
    gelu = 0.5 * g * (1.0 + jnp.tanh(math.sqrt(2.0 / math.pi) * (g + 0.044715 * (g * g * g))))
    yd_ref[...] = (h * gelu).astype(yd_ref.dtype)


def _mixers(u3, p, layer):
    bsz, seq, _ = u3.shape
    ts = SEQ_TILE

    def ucol(j):
        return pl.BlockSpec((None, ts, GROUP_W), lambda b, s: (b, s, j))

    def par(shape):
        return pl.BlockSpec((None,) + shape, lambda b, s: (layer,) + (0,) * len(shape))

    out_spec = pl.BlockSpec((None, ts, GROUP_W), lambda b, s: (b, s, 0))
    out_sds = jax.ShapeDtypeStruct((bsz, seq, GROUP_W), BF16)
    return pl.pallas_call(
        _mixers_kernel,
        grid=(bsz, seq // ts),
        in_specs=[ucol(0), ucol(1), ucol(2), ucol(3), ucol(4), ucol(8), ucol(9),
                  par((CONF_KERNEL, GROUP_W)), par((1, GROUP_W)), par((1, GROUP_W)), par((1, GROUP_W)),
                  par((SCONV_KERNEL, GROUP_W)), par((LRU_CONV, GROUP_W)), par((1, GROUP_W)),
                  par((GROUP_W, 2 * GROUP_W)), par((1, 2 * GROUP_W)), par((1, GROUP_W)),
                  pl.BlockSpec((GROUP_W, GROUP_W), lambda b, s: (0, 0))],
        out_specs=[out_spec, out_spec, out_spec],
        out_shape=[out_sds, out_sds, out_sds],
        scratch_shapes=[
            pltpu.VMEM((HALO_A + ts, GROUP_W), F32),
            pltpu.VMEM((ts, GROUP_W), F32),
            pltpu.VMEM((HALO_S + ts, GROUP_W), F32),
            pltpu.VMEM((HALO_S + ts, GROUP_W), F32),
            pltpu.VMEM((8, GROUP_W), F32),
        ],
        compiler_params=pltpu.CompilerParams(
            dimension_semantics=("parallel", "arbitrary"), vmem_limit_bytes=VMEM_LIMIT),
        name="mixers",
    )(u3, u3, u3, u3, u3, u3, u3,
      p["dww"], p["dwb"], p["gng"], p["gnb"], p["scw"], p["lcw"], p["lcb"],
      p["wax"], p["bax"], p["lam"], p["gm"])


_NT = (((1,), (1,)), ((), ()))


def _attn_kernel(q_ref, k_ref, v_ref, o_ref, kb, vt, kmm, gate_s, nsd, cbias, rbias, qm, m_s, l_s, acc):
    nb = kb.shape[0]
    blk = MOBA_BLOCK
    i = pl.program_id(1)
    slopes = [2.0 ** (-8.0 * (h + 1) / ATT_HEADS) for h in range(ATT_HEADS)]
    lane_head = lax.broadcasted_iota(jnp.int32, (1, GROUP_W), 1) // HEAD_DIM

    @pl.when(i == 0)
    def _per_batch():
        for n in range(nb):
            kblk = k_ref[n * blk:(n + 1) * blk, :]
            kb[n] = kblk.astype(BF16)
            vt[n] = v_ref[n * blk:(n + 1) * blk, :].T.astype(BF16)
            kmean = jnp.mean(kblk, axis=0, keepdims=True)
            for h in range(ATT_HEADS):
                kmm[h * nb + n:h * nb + n + 1, :] = jnp.where(lane_head == h, kmean, 0.0)
        t_l = lax.broadcasted_iota(jnp.int32, (blk, blk), 1)
        s_l = lax.broadcasted_iota(jnp.int32, (blk, blk), 0)
        d0 = (t_l - s_l).astype(F32)
        for h in range(ATT_HEADS):
            nsd[h] = -slopes[h] * d0
            cbias[h] = jnp.where(d0 >= 0.0, -slopes[h] * d0, NEG_INF)

    q = q_ref[...] * (HEAD_DIM ** -0.5)
    gate_s[...] = lax.dot_general(kmm[...], q, _NT, precision=lax.Precision.HIGHEST,
                                  preferred_element_type=F32)
    blk_id = lax.broadcasted_iota(jnp.int32, (nb, blk), 0)
    k_own = kb[i]
    for h in range(ATT_HEADS):
        g = gate_s[h * nb:(h + 1) * nb, :]
        rank = jnp.zeros((nb, blk), jnp.int32)
        for m in range(nb):
            gm = gate_s[h * nb + m:h * nb + m + 1, :]
            beats = (gm > g) | ((gm == g) & (m < blk_id))
            rank = rank + jnp.where(beats, 1, 0) * jnp.where(m < i, 1, 0)
        sel = (blk_id < i) & (rank < MOBA_TOP_K)
        rbias[h] = jnp.where(sel, (-slopes[h] * blk) * (i - blk_id).astype(F32), NEG_INF)
        qh = jnp.where(lane_head == h, q, 0.0).astype(BF16)
        qm[h] = qh
        s_t = lax.dot_general(k_own, qh, _NT, preferred_element_type=F32) + cbias[h]
        m0 = jnp.max(s_t, axis=0, keepdims=True)
        p = jnp.exp(s_t - m0)
        m_s[h] = m0
        l_s[h] = jnp.sum(p, axis=0, keepdims=True)
        acc[h] = _dot(vt[i, h * HEAD_DIM:(h + 1) * HEAD_DIM, :], p.astype(BF16))

    def past_block(n, carry):
        k_n = kb[n]
        for h in range(ATT_HEADS):
            s_t = (lax.dot_general(k_n, qm[h], _NT, preferred_element_type=F32)
                   + nsd[h] + rbias[h, pl.ds(n, 1), :])
            m_old = m_s[h]
            m_new = jnp.maximum(m_old, jnp.max(s_t, axis=0, keepdims=True))
            alpha = jnp.exp(m_old - m_new)
            p = jnp.exp(s_t - m_new)
            l_s[h] = alpha * l_s[h] + jnp.sum(p, axis=0, keepdims=True)
            acc[h] = alpha * acc[h] + _dot(vt[n, h * HEAD_DIM:(h + 1) * HEAD_DIM, :], p.astype(BF16))
            m_s[h] = m_new
        return carry

    lax.fori_loop(0, i, past_block, 0)
    out_t = jnp.concatenate([acc[h] / l_s[h] for h in range(ATT_HEADS)], axis=0)
    o_ref[...] = out_t.T.astype(o_ref.dtype)


def _attention(u3):
    bsz, seq, _ = u3.shape
    nb = seq // MOBA_BLOCK
    blk = MOBA_BLOCK
    return pl.pallas_call(
        _attn_kernel,
        grid=(bsz, nb),
        in_specs=[
            pl.BlockSpec((None, blk, GROUP_W), lambda b, i: (b, i, 5)),
            pl.BlockSpec((None, seq, GROUP_W), lambda b, i: (b, 0, 6)),
            pl.BlockSpec((None, seq, GROUP_W), lambda b, i: (b, 0, 7)),
        ],
        out_specs=pl.BlockSpec((None, blk, GROUP_W), lambda b, i: (b, i, 0)),
        out_shape=jax.ShapeDtypeStruct((bsz, seq, GROUP_W), BF16),
        scratch_shapes=[
            pltpu.VMEM((nb, blk, GROUP_W), BF16),
            pltpu.VMEM((nb, GROUP_W, blk), BF16),
            pltpu.VMEM((ATT_HEADS * nb, GROUP_W), F32),
            pltpu.VMEM((ATT_HEADS * nb, blk), F32),
            pltpu.VMEM((ATT_HEADS, blk, blk), F32),
            pltpu.VMEM((ATT_HEADS, blk, blk), F32),
            pltpu.VMEM((ATT_HEADS, nb, blk), F32),
            pltpu.VMEM((ATT_HEADS, blk, GROUP_W), BF16),
            pltpu.VMEM((ATT_HEADS, 1, blk), F32),
            pltpu.VMEM((ATT_HEADS, 1, blk), F32),
            pltpu.VMEM((ATT_HEADS, HEAD_DIM, blk), F32),
        ],
        compiler_params=pltpu.CompilerParams(
            dimension_semantics=("parallel", "arbitrary"), vmem_limit_bytes=VMEM_LIMIT),
        name="moba_attention",
    )(u3, u3, u3)


def _out_proj_kernel(ya_ref, yb_ref, yc_ref, yd_ref, x_ref, w_ref, gpost_ref, gpre_ref, xo_ref, h_ref):
    y = None
    for j, y_ref in enumerate((ya_ref, yb_ref, yc_ref, yd_ref)):
        term = _dot(y_ref[...], w_ref[j * GROUP_W:(j + 1) * GROUP_W, :])
        y = term if y is None else y + term
    x = x_ref[...] + _rms(y, gpost_ref[...])
    xo_ref[...] = x
    h_ref[...] = _rms(x, gpre_ref[...]).astype(h_ref.dtype)


def _out_proj(ya, yb, yc, yd, x2, w, gpost, gpre, layer):
    tok = x2.shape[0]
    tm = TOKEN_TILE
    yspec = pl.BlockSpec((tm, GROUP_W), lambda t: (t, 0))
    xspec = pl.BlockSpec((tm, D_MODEL), lambda t: (t, 0))
    gspec = pl.BlockSpec((None, 1, D_MODEL), lambda t: (layer, 0, 0))
    return pl.pallas_call(
        _out_proj_kernel,
        grid=(tok // tm,),
        in_specs=[yspec, yspec, yspec, yspec, xspec,
                  pl.BlockSpec((None, D_MODEL, D_MODEL), lambda t: (layer, 0, 0)), gspec, gspec],
        out_specs=[xspec, xspec],
        out_shape=[jax.ShapeDtypeStruct((tok, D_MODEL), F32), jax.ShapeDtypeStruct((tok, D_MODEL), BF16)],
        compiler_params=pltpu.CompilerParams(
            dimension_semantics=("parallel",), vmem_limit_bytes=VMEM_LIMIT),
        name="out_proj",
    )(ya, yb, yc, yd, x2, w, gpost, gpre)


def _mlp_kernel(h_ref, x_ref, w1_ref, w2_ref, g_ref, xo_ref):
    h = h_ref[...]
    m = None
    for c in range(0, D_FF, FF_CHUNK):
        t = jnp.maximum(_dot(h, w1_ref[:, c:c + FF_CHUNK]), 0.0)
        term = _dot((t * t).astype(BF16), w2_ref[c:c + FF_CHUNK, :])
        m = term if m is None else m + term
    xo_ref[...] = x_ref[...] + _rms(m, g_ref[...])


def _mlp(h2, x2, w1, w2, g, layer):
    tok = x2.shape[0]
    tm = TOKEN_TILE
    xspec = pl.BlockSpec((tm, D_MODEL), lambda t: (t, 0))
    return pl.pallas_call(
        _mlp_kernel,
        grid=(tok // tm,),
        in_specs=[xspec, xspec,
                  pl.BlockSpec((None, D_MODEL, D_FF), lambda t: (layer, 0, 0), pipeline_mode=pl.Buffered(1)),
                  pl.BlockSpec((None, D_FF, D_MODEL), lambda t: (layer, 0, 0), pipeline_mode=pl.Buffered(1)),
                  pl.BlockSpec((None, 1, D_MODEL), lambda t: (layer, 0, 0))],
        out_specs=xspec,
        out_shape=jax.ShapeDtypeStruct((tok, D_MODEL), F32),
        compiler_params=pltpu.CompilerParams(
            dimension_semantics=("parallel",), vmem_limit_bytes=VMEM_LIMIT),
        name="mlp",
    )(h2, x2, w1, w2, g)


def _block_diag_dense(w):
    depth, nblk, n, _ = w.shape
    eye = jnp.eye(nblk, dtype=w.dtype)
    return jnp.einsum("dgij,gh->dgihj", w, eye).reshape(depth, nblk * n, nblk * n)


def kernel(x, pre_mix_g, w_in, conf_dw_w, conf_dw_b, conf_gn_g, conf_gn_b, sconv_w, lru_conv_w, lru_conv_b,
           lru_wa, lru_ba, lru_wx, lru_bx, lru_lam, w_out, post_mix_g, pre_mlp_g, mlp_w1, mlp_w2, post_mlp_g):
    bsz, seq, dm = x.shape
    depth = w_in.shape[0]
    assert dm == D_MODEL and seq % SEQ_TILE == 0 and seq % MOBA_BLOCK == 0 and (bsz * seq) % TOKEN_TILE == 0
    assert seq // MOBA_BLOCK > MOBA_TOP_K

    row = lambda a: a.reshape(depth, 1, a.shape[-1])
    group_of = jnp.arange(GROUP_W) // (GROUP_W // CONF_GROUPS)
    params = {
        "dww": conf_dw_w, "dwb": row(conf_dw_b), "gng": row(conf_gn_g), "gnb": row(conf_gn_b),
        "scw": sconv_w, "lcw": lru_conv_w, "lcb": row(lru_conv_b),
        "wax": jnp.concatenate([_block_diag_dense(lru_wa), _block_diag_dense(lru_wx)], axis=-1).astype(BF16),
        "bax": row(jnp.concatenate([lru_ba, lru_bx], axis=-1)),
        "lam": row(lru_lam),
        "gm": ((group_of[:, None] == group_of[None, :]).astype(F32) / (GROUP_W // CONF_GROUPS)).astype(BF16),
    }
    w_in_b, w_out_b = w_in.astype(BF16), w_out.astype(BF16)
    w1_b, w2_b = mlp_w1.astype(BF16), mlp_w2.astype(BF16)
    g_pre_mix, g_post_mix = row(pre_mix_g), row(post_mix_g)
    g_pre_mlp, g_post_mlp = row(pre_mlp_g), row(post_mlp_g)

    x2 = x.reshape(bsz * seq, dm)
    for layer in range(depth):
        u = _in_proj(x2, g_pre_mix, w_in_b, layer)
        u3 = u.reshape(bsz, seq, IN_COLS)
        ya, yb, yd = _mixers(u3, params, layer)
        yc = _attention(u3)
        flat = lambda a: a.reshape(bsz * seq, GROUP_W)
        x2, h2 = _out_proj(flat(ya), flat(yb), flat(yc), flat(yd), x2, w_out_b, g_post_mix, g_pre_mlp, layer)
        x2 = _mlp(h2, x2, w1_b, w2_b, g_post_mlp, layer)
    return x2.reshape(bsz, seq, dm)
```

```python
import functools
import math

import jax
import jax.numpy as jnp
from jax import lax
from jax.experimental import pallas as pl
from jax.experimental.pallas import tpu as pltpu

F32 = jnp.float32
BF16 = jnp.bfloat16

D_MODEL = 1024
GROUP_W = 256
CONF_KERNEL = 31
CONF_GROUPS = 4
GN_EPS = 1e-5
SCONV_KERNEL = 3
ATT_HEADS = 4
HEAD_DIM = GROUP_W // ATT_HEADS
MOBA_BLOCK = 256
MOBA_TOP_K = 3
LRU_BLOCKS = 4
LRU_CONV = 4
LRU_C = 8.0
D_FF = 4 * D_MODEL
RMS_EPS = 1e-6
IN_COLS = 10 * GROUP_W

TOKEN_TILE = 512
SEQ_TILE = 512
CONV_ROWS = 64
HALO_A = 32
HALO_S = 8
FF_CHUNK = 1024
VMEM_LIMIT = 56 * 1024 * 1024

NEG_INF = float("-inf")


def _rms(x, g):
    return x * lax.rsqrt(jnp.mean(x * x, axis=-1, keepdims=True) + RMS_EPS) * g


def _dot(a, b):
    return jnp.dot(a, b, preferred_element_type=F32)


def _in_proj_kernel(x_ref, g_ref, w_ref, u_ref):
    h = _rms(x_ref[...], g_ref[...]).astype(BF16)
    step = 2 * GROUP_W
    for c in range(0, IN_COLS, step):
        u_ref[:, c:c + step] = _dot(h, w_ref[:, c:c + step])


def _in_proj(x2, g, w, layer):
    tok = x2.shape[0]
    return pl.pallas_call(
        _in_proj_kernel,
        grid=(tok // TOKEN_TILE,),
        in_specs=[
            pl.BlockSpec((TOKEN_TILE, D_MODEL), lambda t: (t, 0)),
            pl.BlockSpec((None, 1, D_MODEL), lambda t: (layer, 0, 0)),
            pl.BlockSpec((None, D_MODEL, IN_COLS), lambda t: (layer, 0, 0)),
        ],
        out_specs=pl.BlockSpec((TOKEN_TILE, IN_COLS), lambda t: (t, 0)),
        out_shape=jax.ShapeDtypeStruct((tok, IN_COLS), F32),
        compiler_params=pltpu.CompilerParams(
            dimension_semantics=("parallel",), vmem_limit_bytes=VMEM_LIMIT),
        name="in_proj",
    )(x2, g, w)


def _group_mean(v, gm):
    hi = v.astype(BF16)
    lo = (v - hi.astype(F32)).astype(BF16)
    return _dot(hi, gm) + _dot(lo, gm)


def _causal_conv(buf, w_ref, halo, width, rows):
    acc = None
    for k in range(width):
        off = halo - (width - 1) + k
        term = w_ref[k:k + 1, :] * buf[off:off + rows, :]
        acc = term if acc is None else acc + term
    return acc


def _mixers_kernel(cv_ref, cg_ref, sb_ref, sc_ref, sx_ref, lr_ref, lg_ref,
                   dww_ref, dwb_ref, gng_ref, gnb_ref, scw_ref, lcw_ref, lcb_ref,
                   wax_ref, bax_ref, lam_ref, gm_ref,
                   ya_ref, yb_ref, yd_ref,
                   zbuf, cbuf, pbuf, xbuf, hcar):
    ts = SEQ_TILE

    @pl.when(pl.program_id(1) == 0)
    def _reset():
        zbuf[0:HALO_A, :] = jnp.zeros((HALO_A, GROUP_W), F32)
        pbuf[0:HALO_S, :] = jnp.zeros((HALO_S, GROUP_W), F32)
        xbuf[0:HALO_S, :] = jnp.zeros((HALO_S, GROUP_W), F32)
        hcar[...] = jnp.zeros_like(hcar)

    zbuf[HALO_A:HALO_A + ts, :] = cv_ref[...] * jax.nn.sigmoid(cg_ref[...])
    for r0 in range(0, ts, CONV_ROWS):
        acc = jnp.broadcast_to(dwb_ref[...], (CONV_ROWS, GROUP_W))
        for k in range(CONF_KERNEL):
            off = r0 + HALO_A - (CONF_KERNEL - 1) + k
            acc = acc + dww_ref[k:k + 1, :] * zbuf[off:off + CONV_ROWS, :]
        cbuf[r0:r0 + CONV_ROWS, :] = acc
    zbuf[0:HALO_A, :] = zbuf[ts:ts + HALO_A, :]
    c = cbuf[...]
    gm = gm_ref[...]
    d = c - _group_mean(c, gm)
    var = _group_mean(d * d, gm)
    yn = d * lax.rsqrt(var + GN_EPS) * gng_ref[...] + gnb_ref[...]
    ya_ref[...] = (yn * jax.nn.sigmoid(yn)).astype(ya_ref.dtype)

    pbuf[HALO_S:HALO_S + ts, :] = sc_ref[...] * sx_ref[...]
    conv_b = _causal_conv(pbuf, scw_ref, HALO_S, SCONV_KERNEL, ts)
    pbuf[0:HALO_S, :] = pbuf[ts:ts + HALO_S, :]
    yb_ref[...] = (sb_ref[...] * conv_b).astype(yb_ref.dtype)

    xbuf[HALO_S:HALO_S + ts, :] = lr_ref[...]
    xc = _causal_conv(xbuf, lcw_ref, HALO_S, LRU_CONV, ts) + lcb_ref[...]
    xbuf[0:HALO_S, :] = xbuf[ts:ts + HALO_S, :]
    gates = _dot(xc.astype(BF16), wax_ref[...]) + bax_ref[...]
    r = jax.nn.sigmoid(gates[:, :GROUP_W])
    i = jax.nn.sigmoid(gates[:, GROUP_W:])
    lam = lam_ref[...]
    softplus_neg_lam = jnp.maximum(-lam, 0.0) + jnp.log1p(jnp.exp(-jnp.abs(lam)))
    log_a = (-LRU_C * softplus_neg_lam) * r
    a = jnp.exp(log_a)
    th = jnp.tanh(log_a)
    b = jnp.sqrt(-2.0 * th / (1.0 - th)) * (i * xc)
    row = lax.broadcasted_iota(jnp.int32, (ts, GROUP_W), 0)
    dist = 1
    while dist < ts:
        keep = row >= dist
        a_prev = pltpu.roll(a, dist, axis=0)
        b_prev = pltpu.roll(b, dist, axis=0)
        b = jnp.where(keep, a * b_prev + b, b)
        a = jnp.where(keep, a * a_prev, a)
        dist *= 2
    h = a * hcar[0:1, :] + b
    hcar[0:1, :] = h[ts - 1:ts, :]
    g = lg_ref[...]
    gelu = 0.5 * g * (1.0 + jnp.tanh(math.sqrt(2.0 / math.pi) * (g + 0.044715 * (g * g * g))))
    yd_ref[...] = (h * gelu).astype(yd_ref.dtype)


def _mixers(u3, p, layer):
    bsz, seq, _ = u3.shape
    ts = SEQ_TILE

    def ucol(j):
        return pl.BlockSpec((None, ts, GROUP_W), lambda b, s: (b, s, j))

    def par(shape):
        return pl.BlockSpec((None,) + shape, lambda b, s: (layer,) + (0,) * len(shape))

    out_spec = pl.BlockSpec((None, ts, GROUP_W), lambda b, s: (b, s, 0))
    out_sds = jax.ShapeDtypeStruct((bsz, seq, GROUP_W), BF16)
    return pl.pallas_call(
        _mixers_kernel,
        grid=(bsz, seq // ts),
        in_specs=[ucol(0), ucol(1), ucol(2), ucol(3), ucol(4), ucol(8), ucol(9),
                  par((CONF_KERNEL, GROUP_W)), par((1, GROUP_W)), par((1, GROUP_W)), par((1, GROUP_W)),
                  par((SCONV_KERNEL, GROUP_W)), par((LRU_CONV, GROUP_W)), par((1, GROUP_W)),
                  par((GROUP_W, 2 * GROUP_W)), par((1, 2 * GROUP_W)), par((1, GROUP_W)),
                  pl.BlockSpec((GROUP_W, GROUP_W), lambda b, s: (0, 0))],
        out_specs=[out_spec, out_spec, out_spec],
        out_shape=[out_sds, out_sds, out_sds],
        scratch_shapes=[
            pltpu.VMEM((HALO_A + ts, GROUP_W), F32),
            pltpu.VMEM((ts, GROUP_W), F32),
            pltpu.VMEM((HALO_S + ts, GROUP_W), F32),
            pltpu.VMEM((HALO_S + ts, GROUP_W), F32),
            pltpu.VMEM((8, GROUP_W), F32),
        ],
        compiler_params=pltpu.CompilerParams(
            dimension_semantics=("parallel", "arbitrary"), vmem_limit_bytes=VMEM_LIMIT),
        name="mixers",
    )(u3, u3, u3, u3, u3, u3, u3,
      p["dww"], p["dwb"], p["gng"], p["gnb"], p["scw"], p["lcw"], p["lcb"],
      p["wax"], p["bax"], p["lam"], p["gm"])


_NT = (((1,), (1,)), ((), ()))


def _attn_kernel(q_ref, k_ref, v_ref, o_ref, kb, vt, kmm, gate_s, nsd, cbias, rbias, qm, s_buf, m_run, m_acc, l_s, acc):
    nb = kb.shape[0]
    blk = MOBA_BLOCK
    i = pl.program_id(1)
    slopes = [2.0 ** (-8.0 * (h + 1) / ATT_HEADS) for h in range(ATT_HEADS)]
    lane_head = lax.broadcasted_iota(jnp.int32, (1, GROUP_W), 1) // HEAD_DIM

    @pl.when(i == 0)
    def _per_batch():
        for n in range(nb):
            kblk = k_ref[n * blk:(n + 1) * blk, :]
            kb[n] = kblk.astype(BF16)
            vt[n] = v_ref[n * blk:(n + 1) * blk, :].T.astype(BF16)
            kmean = jnp.mean(kblk, axis=0, keepdims=True)
            for h in range(ATT_HEADS):
                kmm[h * nb + n:h * nb + n + 1, :] = jnp.where(lane_head == h, kmean, 0.0)
        t_l = lax.broadcasted_iota(jnp.int32, (blk, blk), 1)
        s_l = lax.broadcasted_iota(jnp.int32, (blk, blk), 0)
        d0 = (t_l - s_l).astype(F32)
        for h in range(ATT_HEADS):
            nsd[h] = -slopes[h] * d0
            cbias[h] = jnp.where(d0 >= 0.0, -slopes[h] * d0, NEG_INF)

    q = q_ref[...] * (HEAD_DIM ** -0.5)
    gate_s[...] = lax.dot_general(kmm[...], q, _NT, precision=lax.Precision.HIGHEST,
                                  preferred_element_type=F32)
    blk_id = lax.broadcasted_iota(jnp.int32, (nb, blk), 0)
    k_own = kb[i]
    for h in range(ATT_HEADS):
        g = gate_s[h * nb:(h + 1) * nb, :]
        rank = jnp.zeros((nb, blk), jnp.int32)
        for m in range(nb):
            gm = gate_s[h * nb + m:h * nb + m + 1, :]
            beats = (gm > g) | ((gm == g) & (m < blk_id))
            rank = rank + jnp.where(beats, 1, 0) * jnp.where(m < i, 1, 0)
        sel = (blk_id < i) & (rank < MOBA_TOP_K)
        past_bias = jnp.where(sel, (-slopes[h] * blk) * (i - blk_id).astype(F32), NEG_INF)
        rbias[h] = jnp.where(blk_id == i, 0.0, past_bias)
        qh = jnp.where(lane_head == h, q, 0.0).astype(BF16)
        qm[h] = qh
        s_t = lax.dot_general(k_own, qh, _NT, preferred_element_type=F32) + cbias[h]
        s_buf[h, 0] = s_t
        m0 = jnp.max(s_t, axis=0, keepdims=True)
        m_run[h] = m0
        m_acc[h] = m0
        l_s[h] = jnp.zeros((1, blk), F32)
        acc[h] = jnp.zeros((HEAD_DIM, blk), F32)

    def accumulate(block, slot):
        for h in range(ATT_HEADS):
            m_now = m_run[h]
            alpha = jnp.exp(m_acc[h] - m_now)
            p = jnp.exp(s_buf[h, slot] - (m_now - rbias[h, pl.ds(block, 1), :]))
            l_s[h] = alpha * l_s[h] + jnp.sum(p, axis=0, keepdims=True)
            acc[h] = alpha * acc[h] + _dot(vt[block, h * HEAD_DIM:(h + 1) * HEAD_DIM, :], p.astype(BF16))
            m_acc[h] = m_now

    def step(j, carry):
        accumulate(jnp.where(j == 1, i, j - 2), (j - 1) % 2)
        k_n = kb[j - 1]
        for h in range(ATT_HEADS):
            s_t = lax.dot_general(k_n, qm[h], _NT, preferred_element_type=F32) + nsd[h]
            s_buf[h, j % 2] = s_t
            m_run[h] = jnp.maximum(m_run[h], jnp.max(s_t, axis=0, keepdims=True) + rbias[h, pl.ds(j - 1, 1), :])
        return carry

    lax.fori_loop(1, i + 1, step, 0)
    accumulate(jnp.where(i == 0, 0, i - 1), i % 2)
    out_t = jnp.concatenate([acc[h] / l_s[h] for h in range(ATT_HEADS)], axis=0)
    o_ref[...] = out_t.T.astype(o_ref.dtype)


def _attention(u3):
    bsz, seq, _ = u3.shape
    nb = seq // MOBA_BLOCK
    blk = MOBA_BLOCK
    return pl.pallas_call(
        _attn_kernel,
        grid=(bsz, nb),
        in_specs=[
            pl.BlockSpec((None, blk, GROUP_W), lambda b, i: (b, i, 5)),
            pl.BlockSpec((None, seq, GROUP_W), lambda b, i: (b, 0, 6)),
            pl.BlockSpec((None, seq, GROUP_W), lambda b, i: (b, 0, 7)),
        ],
        out_specs=pl.BlockSpec((None, blk, GROUP_W), lambda b, i: (b, i, 0)),
        out_shape=jax.ShapeDtypeStruct((bsz, seq, GROUP_W), BF16),
        scratch_shapes=[
            pltpu.VMEM((nb, blk, GROUP_W), BF16),
            pltpu.VMEM((nb, GROUP_W, blk), BF16),
            pltpu.VMEM((ATT_HEADS * nb, GROUP_W), F32),
            pltpu.VMEM((ATT_HEADS * nb, blk), F32),
            pltpu.VMEM((ATT_HEADS, blk, blk), F32),
            pltpu.VMEM((ATT_HEADS, blk, blk), F32),
            pltpu.VMEM((ATT_HEADS, nb, blk), F32),
            pltpu.VMEM((ATT_HEADS, blk, GROUP_W), BF16),
            pltpu.VMEM((ATT_HEADS, 2, blk, blk), F32),
            pltpu.VMEM((ATT_HEADS, 1, blk), F32),
            pltpu.VMEM((ATT_HEADS, 1, blk), F32),
            pltpu.VMEM((ATT_HEADS, 1, blk), F32),
            pltpu.VMEM((ATT_HEADS, HEAD_DIM, blk), F32),
        ],
        compiler_params=pltpu.CompilerParams(
            dimension_semantics=("parallel", "arbitrary"), vmem_limit_bytes=VMEM_LIMIT),
        name="moba_attention",
    )(u3, u3, u3)


def _out_proj_kernel(ya_ref, yb_ref, yc_ref, yd_ref, x_ref, w_ref, gpost_ref, gpre_ref, xo_ref, h_ref):
    y = None
    for j, y_ref in enumerate((ya_ref, yb_ref, yc_ref, yd_ref)):
        term = _dot(y_ref[...], w_ref[j * GROUP_W:(j + 1) * GROUP_W, :])
        y = term if y is None else y + term
    x = x_ref[...] + _rms(y, gpost_ref[...])
    xo_ref[...] = x
    h_ref[...] = _rms(x, gpre_ref[...]).astype(h_ref.dtype)


def _out_proj(ya, yb, yc, yd, x2, w, gpost, gpre, layer):
    tok = x2.shape[0]
    tm = TOKEN_TILE
    yspec = pl.BlockSpec((tm, GROUP_W), lambda t: (t, 0))
    xspec = pl.BlockSpec((tm, D_MODEL), lambda t: (t, 0))
    gspec = pl.BlockSpec((None, 1, D_MODEL), lambda t: (layer, 0, 0))
    return pl.pallas_call(
        _out_proj_kernel,
        grid=(tok // tm,),
        in_specs=[yspec, yspec, yspec, yspec, xspec,
                  pl.BlockSpec((None, D_MODEL, D_MODEL), lambda t: (layer, 0, 0)), gspec, gspec],
        out_specs=[xspec, xspec],
        out_shape=[jax.ShapeDtypeStruct((tok, D_MODEL), F32), jax.ShapeDtypeStruct((tok, D_MODEL), BF16)],
        compiler_params=pltpu.CompilerParams(
            dimension_semantics=("parallel",), vmem_limit_bytes=VMEM_LIMIT),
        name="out_proj",
    )(ya, yb, yc, yd, x2, w, gpost, gpre)


def _mlp_kernel(h_ref, x_ref, w1_ref, w2_ref, g_ref, xo_ref):
    h = h_ref[...]
    m = None
    for c in range(0, D_FF, FF_CHUNK):
        t = jnp.maximum(_dot(h, w1_ref[:, c:c + FF_CHUNK]), 0.0)
        term = _dot((t * t).astype(BF16), w2_ref[c:c + FF_CHUNK, :])
        m = term if m is None else m + term
    xo_ref[...] = x_ref[...] + _rms(m, g_ref[...])


def _mlp(h2, x2, w1, w2, g, layer):
    tok = x2.shape[0]
    tm = TOKEN_TILE
    xspec = pl.BlockSpec((tm, D_MODEL), lambda t: (t, 0))
    return pl.pallas_call(
        _mlp_kernel,
        grid=(tok // tm,),
        in_specs=[xspec, xspec,
                  pl.BlockSpec((None, D_MODEL, D_FF), lambda t: (layer, 0, 0), pipeline_mode=pl.Buffered(1)),
                  pl.BlockSpec((None, D_FF, D_MODEL), lambda t: (layer, 0, 0), pipeline_mode=pl.Buffered(1)),
                  pl.BlockSpec((None, 1, D_MODEL), lambda t: (layer, 0, 0))],
        out_specs=xspec,
        out_shape=jax.ShapeDtypeStruct((tok, D_MODEL), F32),
        compiler_params=pltpu.CompilerParams(
            dimension_semantics=("parallel",), vmem_limit_bytes=VMEM_LIMIT),
        name="mlp",
    )(h2, x2, w1, w2, g)


def _block_diag_dense(w):
    depth, nblk, n, _ = w.shape
    eye = jnp.eye(nblk, dtype=w.dtype)
    return jnp.einsum("dgij,gh->dgihj", w, eye).reshape(depth, nblk * n, nblk * n)


def kernel(x, pre_mix_g, w_in, conf_dw_w, conf_dw_b, conf_gn_g, conf_gn_b, sconv_w, lru_conv_w, lru_conv_b,
           lru_wa, lru_ba, lru_wx, lru_bx, lru_lam, w_out, post_mix_g, pre_mlp_g, mlp_w1, mlp_w2, post_mlp_g):
    bsz, seq, dm = x.shape
    depth = w_in.shape[0]
    assert dm == D_MODEL and seq % SEQ_TILE == 0 and seq % MOBA_BLOCK == 0 and (bsz * seq) % TOKEN_TILE == 0
    assert seq // MOBA_BLOCK > MOBA_TOP_K

    row = lambda a: a.reshape(depth, 1, a.shape[-1])
    group_of = jnp.arange(GROUP_W) // (GROUP_W // CONF_GROUPS)
    params = {
        "dww": conf_dw_w, "dwb": row(conf_dw_b), "gng": row(conf_gn_g), "gnb": row(conf_gn_b),
        "scw": sconv_w, "lcw": lru_conv_w, "lcb": row(lru_conv_b),
        "wax": jnp.concatenate([_block_diag_dense(lru_wa), _block_diag_dense(lru_wx)], axis=-1).astype(BF16),
        "bax": row(jnp.concatenate([lru_ba, lru_bx], axis=-1)),
        "lam": row(lru_lam),
        "gm": ((group_of[:, None] == group_of[None, :]).astype(F32) / (GROUP_W // CONF_GROUPS)).astype(BF16),
    }
    w_in_b, w_out_b = w_in.astype(BF16), w_out.astype(BF16)
    w1_b, w2_b = mlp_w1.astype(BF16), mlp_w2.astype(BF16)
    g_pre_mix, g_post_mix = row(pre_mix_g), row(post_mix_g)
    g_pre_mlp, g_post_mlp = row(pre_mlp_g), row(post_mlp_g)

    x2 = x.reshape(bsz * seq, dm)
    for layer in range(depth):
        u = _in_proj(x2, g_pre_mix, w_in_b, layer)
        u3 = u.reshape(bsz, seq, IN_COLS)
        ya, yb, yd = _mixers(u3, params, layer)
        yc = _attention(u3)
        flat = lambda a: a.reshape(bsz * seq, GROUP_W)
        x2, h2 = _out_proj(flat(ya), flat(yb), flat(yc), flat(yd), x2, w_out_b, g_post_mix, g_pre_mlp, layer)
        x2 = _mlp(h2, x2, w1_b, w2_b, g_post_mlp, layer)
    return x2.reshape(bsz, seq, dm)
```

```python
import functools
import math

import jax
import jax.numpy as jnp
from jax import lax
from jax.experimental import pallas as pl
from jax.experimental.pallas import tpu as pltpu

F32 = jnp.float32
BF16 = jnp.bfloat16

D_MODEL = 1024
GROUP_W = 256
CONF_KERNEL = 31
CONF_GROUPS = 4
GN_EPS = 1e-5
SCONV_KERNEL = 3
ATT_HEADS = 4
HEAD_DIM = GROUP_W // ATT_HEADS
MOBA_BLOCK = 256
MOBA_TOP_K = 3
LRU_BLOCKS = 4
LRU_CONV = 4
LRU_C = 8.0
D_FF = 4 * D_MODEL
RMS_EPS = 1e-6
IN_COLS = 10 * GROUP_W

TOKEN_TILE = 512
SEQ_TILE = 512
UNIT = MOBA_BLOCK
SEG = UNIT // 8
FF_CHUNK = 1024
VMEM_LIMIT = 56 * 1024 * 1024

NEG_INF = float("-inf")


def _rms(x, g):
    return x * lax.rsqrt(jnp.mean(x * x, axis=-1, keepdims=True) + RMS_EPS) * g


def _dot(a, b):
    return jnp.dot(a, b, preferred_element_type=F32)


def _in_proj_kernel(x_ref, g_ref, w_ref, u_ref):
    h = _rms(x_ref[...], g_ref[...]).astype(BF16)
    step = 2 * GROUP_W
    for c in range(0, IN_COLS, step):
        u_ref[:, c:c + step] = _dot(h, w_ref[:, c:c + step])


def _in_proj(x2, g, w, layer):
    tok = x2.shape[0]
    return pl.pallas_call(
        _in_proj_kernel,
        grid=(tok // TOKEN_TILE,),
        in_specs=[
            pl.BlockSpec((TOKEN_TILE, D_MODEL), lambda t: (t, 0)),
            pl.BlockSpec((None, 1, D_MODEL), lambda t: (layer, 0, 0)),
            pl.BlockSpec((None, D_MODEL, IN_COLS), lambda t: (layer, 0, 0)),
        ],
        out_specs=pl.BlockSpec((TOKEN_TILE, IN_COLS), lambda t: (t, 0)),
        out_shape=jax.ShapeDtypeStruct((tok, IN_COLS), F32),
        compiler_params=pltpu.CompilerParams(
            dimension_semantics=("parallel",), vmem_limit_bytes=VMEM_LIMIT),
        name="in_proj",
    )(x2, g, w)


def _group_mean(v, gm):
    hi = v.astype(BF16)
    lo = (v - hi.astype(F32)).astype(BF16)
    return _dot(hi, gm) + _dot(lo, gm)


def _seg_conv(z, e_ref, w_ref, bias, width, out_ref):
    sub = lax.broadcasted_iota(jnp.int32, (8, GROUP_W), 0)
    for a in range(SEG - (width - 1), SEG):
        cur = z[8 * a:8 * a + 8, :]
        prev = e_ref[UNIT + 8 * a:UNIT + 8 * a + 8, :]
        e_ref[8 * a:8 * a + 8, :] = pltpu.roll(jnp.where(sub == 7, prev, cur), 1, axis=0)
    e_ref[UNIT:2 * UNIT, :] = z
    for a0 in range(0, SEG, 8):
        accs = [bias] * 8
        for j in range(width):
            w_j = w_ref[width - 1 - j]
            for a in range(a0, a0 + 8):
                src = UNIT + 8 * (a - j) if a >= j else 8 * (a - j + SEG)
                term = w_j * e_ref[src:src + 8, :]
                accs[a - a0] = term if accs[a - a0] is None else accs[a - a0] + term
        for a in range(a0, a0 + 8):
            out_ref[8 * a:8 * a + 8, :] = accs[a - a0]


def _mixers_kernel(cv_ref, cg_ref, sb_ref, sc_ref, sx_ref, lr_ref, lg_ref,
                   dww_ref, dwb_ref, gng_ref, gnb_ref, scw_ref, lcw_ref, lcb_ref,
                   wax_ref, bax_ref, lam_ref, gm_ref,
                   ya_ref, yb_ref, yd_ref,
                   ea, eb, ed, cbuf, abuf, bbuf, hloc, acum, endb, hcar):
    @pl.when(pl.program_id(1) == 0)
    def _reset():
        for e in (ea, eb, ed):
            e[UNIT:2 * UNIT, :] = jnp.zeros((UNIT, GROUP_W), F32)
        hcar[...] = jnp.zeros_like(hcar)

    sub = lax.broadcasted_iota(jnp.int32, (8, GROUP_W), 0)
    bias_a = jnp.broadcast_to(dwb_ref[...], (8, GROUP_W))
    bias_d = jnp.broadcast_to(lcb_ref[...], (8, GROUP_W))
    lam = lam_ref[...]
    neg_c_softplus = -LRU_C * (jnp.maximum(-lam, 0.0) + jnp.log1p(jnp.exp(-jnp.abs(lam))))
    gm = gm_ref[...]

    def unit(u, carry):
        rows = pl.ds(pl.multiple_of(u * UNIT, UNIT), UNIT)

        _seg_conv(cv_ref[rows, :] * jax.nn.sigmoid(cg_ref[rows, :]), ea, dww_ref, bias_a, CONF_KERNEL, cbuf)
        c = cbuf[...]
        d = c - _group_mean(c, gm)
        var = _group_mean(d * d, gm)
        yn = d * lax.rsqrt(var + GN_EPS) * gng_ref[...] + gnb_ref[...]
        ya_ref[rows, :] = (yn * jax.nn.sigmoid(yn)).astype(ya_ref.dtype)

        _seg_conv(sc_ref[rows, :] * sx_ref[rows, :], eb, scw_ref, None, SCONV_KERNEL, cbuf)
        yb_ref[rows, :] = (sb_ref[rows, :] * cbuf[...]).astype(yb_ref.dtype)

        _seg_conv(lr_ref[rows, :], ed, lcw_ref, bias_d, LRU_CONV, cbuf)
        xc = cbuf[...]
        gates = _dot(xc.astype(BF16), wax_ref[...]) + bax_ref[...]
        r = jax.nn.sigmoid(gates[:, :GROUP_W])
        i = jax.nn.sigmoid(gates[:, GROUP_W:])
        log_a = neg_c_softplus * r
        abuf[...] = jnp.exp(log_a)
        th = jnp.tanh(log_a)
        bbuf[...] = jnp.sqrt(-2.0 * th / (1.0 - th)) * (i * xc)
        h_run = bbuf[0:8, :]
        a_run = abuf[0:8, :]
        hloc[0:8, :] = h_run
        acum[0:8, :] = a_run
        for s in range(1, SEG):
            a_s = abuf[8 * s:8 * s + 8, :]
            h_run = a_s * h_run + bbuf[8 * s:8 * s + 8, :]
            a_run = a_s * a_run
            hloc[8 * s:8 * s + 8, :] = h_run
            acum[8 * s:8 * s + 8, :] = a_run
        dist = 1
        while dist < 8:
            keep = sub >= dist
            h_run = jnp.where(keep, a_run * pltpu.roll(h_run, dist, axis=0) + h_run, h_run)
            a_run = jnp.where(keep, a_run * pltpu.roll(a_run, dist, axis=0), a_run)
            dist *= 2
        h_in = hcar[...]
        seg_end = a_run * h_in + h_run
        endb[...] = seg_end
        seg_in = jnp.where(sub == 0, h_in, pltpu.roll(seg_end, 1, axis=0))
        hcar[...] = jnp.broadcast_to(endb[7:8, :], (8, GROUP_W))
        for s in range(SEG):
            hloc[8 * s:8 * s + 8, :] = hloc[8 * s:8 * s + 8, :] + acum[8 * s:8 * s + 8, :] * seg_in
        g = lg_ref[rows, :]
        gelu = 0.5 * g * (1.0 + jnp.tanh(math.sqrt(2.0 / math.pi) * (g + 0.044715 * (g * g * g))))
        yd_ref[rows, :] = (hloc[...] * gelu).astype(yd_ref.dtype)
        return carry

    lax.fori_loop(0, SEQ_TILE // UNIT, unit, 0)


def _mixers(u3, p, layer):
    bsz, seq, _ = u3.shape
    ts = SEQ_TILE

    def ucol(j):
        return pl.BlockSpec((None, ts, GROUP_W), lambda b, s: (b, s, j))

    def par(shape):
        return pl.BlockSpec((None,) + shape, lambda b, s: (layer,) + (0,) * len(shape))

    out_spec = pl.BlockSpec((None, ts, GROUP_W), lambda b, s: (b, s, 0))
    out_sds = jax.ShapeDtypeStruct((bsz, seq, GROUP_W), BF16)
    return pl.pallas_call(
        _mixers_kernel,
        grid=(bsz, seq // ts),
        in_specs=[ucol(0), ucol(1), ucol(2), ucol(3), ucol(4), ucol(8), ucol(9),
                  par((CONF_KERNEL, 8, GROUP_W)), par((1, GROUP_W)), par((1, GROUP_W)), par((1, GROUP_W)),
                  par((SCONV_KERNEL, 8, GROUP_W)), par((LRU_CONV, 8, GROUP_W)), par((1, GROUP_W)),
                  par((GROUP_W, 2 * GROUP_W)), par((1, 2 * GROUP_W)), par((1, GROUP_W)),
                  pl.BlockSpec((GROUP_W, GROUP_W), lambda b, s: (0, 0))],
        out_specs=[out_spec, out_spec, out_spec],
        out_shape=[out_sds, out_sds, out_sds],
        scratch_shapes=(
            [pltpu.VMEM((2 * UNIT, GROUP_W), F32)] * 3
            + [pltpu.VMEM((UNIT, GROUP_W), F32)] * 5
            + [pltpu.VMEM((8, GROUP_W), F32)] * 2
        ),
        compiler_params=pltpu.CompilerParams(
            dimension_semantics=("parallel", "arbitrary"), vmem_limit_bytes=VMEM_LIMIT),
        name="mixers",
    )(u3, u3, u3, u3, u3, u3, u3,
      p["dww"], p["dwb"], p["gng"], p["gnb"], p["scw"], p["lcw"], p["lcb"],
      p["wax"], p["bax"], p["lam"], p["gm"])


_NT = (((1,), (1,)), ((), ()))


def _attn_kernel(q_ref, k_ref, v_ref, o_ref, kb, vt, kmm, gate_s, nsd, cbias, rbias, qm, s_buf, m_run, m_acc, l_s, acc):
    nb = kb.shape[0]
    blk = MOBA_BLOCK
    i = pl.program_id(1)
    slopes = [2.0 ** (-8.0 * (h + 1) / ATT_HEADS) for h in range(ATT_HEADS)]
    lane_head = lax.broadcasted_iota(jnp.int32, (1, GROUP_W), 1) // HEAD_DIM

    @pl.when(i == 0)
    def _per_batch():
        for n in range(nb):
            kblk = k_ref[n * blk:(n + 1) * blk, :]
            kb[n] = kblk.astype(BF16)
            vt[n] = v_ref[n * blk:(n + 1) * blk, :].T.astype(BF16)
            kmean = jnp.mean(kblk, axis=0, keepdims=True)
            for h in range(ATT_HEADS):
                kmm[h * nb + n:h * nb + n + 1, :] = jnp.where(lane_head == h, kmean, 0.0)
        pos = lambda rho: SEG * (rho & 7) + (rho >> 3)
        t_l = pos(lax.broadcasted_iota(jnp.int32, (blk, blk), 1))
        s_l = pos(lax.broadcasted_iota(jnp.int32, (blk, blk), 0))
        d0 = (t_l - s_l).astype(F32)
        for h in range(ATT_HEADS):
            nsd[h] = -slopes[h] * d0
            cbias[h] = jnp.where(d0 >= 0.0, -slopes[h] * d0, NEG_INF)

    q = q_ref[...] * (HEAD_DIM ** -0.5)
    gate_s[...] = lax.dot_general(kmm[...], q, _NT, precision=lax.Precision.HIGHEST,
                                  preferred_element_type=F32)
    blk_id = lax.broadcasted_iota(jnp.int32, (nb, blk), 0)
    k_own = kb[i]
    for h in range(ATT_HEADS):
        g = gate_s[h * nb:(h + 1) * nb, :]
        rank = jnp.zeros((nb, blk), jnp.int32)
        for m in range(nb):
            gm = gate_s[h * nb + m:h * nb + m + 1, :]
            beats = (gm > g) | ((gm == g) & (m < blk_id))
            rank = rank + jnp.where(beats, 1, 0) * jnp.where(m < i, 1, 0)
        sel = (blk_id < i) & (rank < MOBA_TOP_K)
        past_bias = jnp.where(sel, (-slopes[h] * blk) * (i - blk_id).astype(F32), NEG_INF)
        rbias[h] = jnp.where(blk_id == i, 0.0, past_bias)
        qh = jnp.where(lane_head == h, q, 0.0).astype(BF16)
        qm[h] = qh
        s_t = lax.dot_general(k_own, qh, _NT, preferred_element_type=F32) + cbias[h]
        s_buf[h, 0] = s_t
        m0 = jnp.max(s_t, axis=0, keepdims=True)
        m_run[h] = m0
        m_acc[h] = m0
        l_s[h] = jnp.zeros((1, blk), F32)
        acc[h] = jnp.zeros((HEAD_DIM, blk), F32)

    def accumulate(block, slot):
        for h in range(ATT_HEADS):
            m_now = m_run[h]
            alpha = jnp.exp(m_acc[h] - m_now)
            p = jnp.exp(s_buf[h, slot] - (m_now - rbias[h, pl.ds(block, 1), :]))
            l_s[h] = alpha * l_s[h] + jnp.sum(p, axis=0, keepdims=True)
            acc[h] = alpha * acc[h] + _dot(vt[block, h * HEAD_DIM:(h + 1) * HEAD_DIM, :], p.astype(BF16))
            m_acc[h] = m_now

    def step(j, carry):
        accumulate(jnp.where(j == 1, i, j - 2), (j - 1) % 2)
        k_n = kb[j - 1]
        for h in range(ATT_HEADS):
            s_t = lax.dot_general(k_n, qm[h], _NT, preferred_element_type=F32) + nsd[h]
            s_buf[h, j % 2] = s_t
            m_run[h] = jnp.maximum(m_run[h], jnp.max(s_t, axis=0, keepdims=True) + rbias[h, pl.ds(j - 1, 1), :])
        return carry

    lax.fori_loop(1, i + 1, step, 0)
    accumulate(jnp.where(i == 0, 0, i - 1), i % 2)
    out_t = jnp.concatenate([acc[h] / l_s[h] for h in range(ATT_HEADS)], axis=0)
    o_ref[...] = out_t.T.astype(o_ref.dtype)


def _attention(u3):
    bsz, seq, _ = u3.shape
    nb = seq // MOBA_BLOCK
    blk = MOBA_BLOCK
    return pl.pallas_call(
        _attn_kernel,
        grid=(bsz, nb),
        in_specs=[
            pl.BlockSpec((None, blk, GROUP_W), lambda b, i: (b, i, 5)),
            pl.BlockSpec((None, seq, GROUP_W), lambda b, i: (b, 0, 6)),
            pl.BlockSpec((None, seq, GROUP_W), lambda b, i: (b, 0, 7)),
        ],
        out_specs=pl.BlockSpec((None, blk, GROUP_W), lambda b, i: (b, i, 0)),
        out_shape=jax.ShapeDtypeStruct((bsz, seq, GROUP_W), BF16),
        scratch_shapes=[
            pltpu.VMEM((nb, blk, GROUP_W), BF16),
            pltpu.VMEM((nb, GROUP_W, blk), BF16),
            pltpu.VMEM((ATT_HEADS * nb, GROUP_W), F32),
            pltpu.VMEM((ATT_HEADS * nb, blk), F32),
            pltpu.VMEM((ATT_HEADS, blk, blk), F32),
            pltpu.VMEM((ATT_HEADS, blk, blk), F32),
            pltpu.VMEM((ATT_HEADS, nb, blk), F32),
            pltpu.VMEM((ATT_HEADS, blk, GROUP_W), BF16),
            pltpu.VMEM((ATT_HEADS, 2, blk, blk), F32),
            pltpu.VMEM((ATT_HEADS, 1, blk), F32),
            pltpu.VMEM((ATT_HEADS, 1, blk), F32),
            pltpu.VMEM((ATT_HEADS, 1, blk), F32),
            pltpu.VMEM((ATT_HEADS, HEAD_DIM, blk), F32),
        ],
        compiler_params=pltpu.CompilerParams(
            dimension_semantics=("parallel", "arbitrary"), vmem_limit_bytes=VMEM_LIMIT),
        name="moba_attention",
    )(u3, u3, u3)


def _out_proj_kernel(ya_ref, yb_ref, yc_ref, yd_ref, x_ref, w_ref, gpost_ref, gpre_ref, xo_ref, h_ref):
    y = None
    for j, y_ref in enumerate((ya_ref, yb_ref, yc_ref, yd_ref)):
        term = _dot(y_ref[...], w_ref[j * GROUP_W:(j + 1) * GROUP_W, :])
        y = term if y is None else y + term
    x = x_ref[...] + _rms(y, gpost_ref[...])
    xo_ref[...] = x
    h_ref[...] = _rms(x, gpre_ref[...]).astype(h_ref.dtype)


def _out_proj(ya, yb, yc, yd, x2, w, gpost, gpre, layer):
    tok = x2.shape[0]
    tm = TOKEN_TILE
    yspec = pl.BlockSpec((tm, GROUP_W), lambda t: (t, 0))
    xspec = pl.BlockSpec((tm, D_MODEL), lambda t: (t, 0))
    gspec = pl.BlockSpec((None, 1, D_MODEL), lambda t: (layer, 0, 0))
    return pl.pallas_call(
        _out_proj_kernel,
        grid=(tok // tm,),
        in_specs=[yspec, yspec, yspec, yspec, xspec,
                  pl.BlockSpec((None, D_MODEL, D_MODEL), lambda t: (layer, 0, 0)), gspec, gspec],
        out_specs=[xspec, xspec],
        out_shape=[jax.ShapeDtypeStruct((tok, D_MODEL), F32), jax.ShapeDtypeStruct((tok, D_MODEL), BF16)],
        compiler_params=pltpu.CompilerParams(
            dimension_semantics=("parallel",), vmem_limit_bytes=VMEM_LIMIT),
        name="out_proj",
    )(ya, yb, yc, yd, x2, w, gpost, gpre)


def _mlp_kernel(h_ref, x_ref, w1_ref, w2_ref, g_ref, xo_ref):
    h = h_ref[...]
    m = None
    for c in range(0, D_FF, FF_CHUNK):
        t = jnp.maximum(_dot(h, w1_ref[:, c:c + FF_CHUNK]), 0.0)
        term = _dot((t * t).astype(BF16), w2_ref[c:c + FF_CHUNK, :])
        m = term if m is None else m + term
    xo_ref[...] = x_ref[...] + _rms(m, g_ref[...])


def _mlp(h2, x2, w1, w2, g, layer):
    tok = x2.shape[0]
    tm = TOKEN_TILE
    xspec = pl.BlockSpec((tm, D_MODEL), lambda t: (t, 0))
    return pl.pallas_call(
        _mlp_kernel,
        grid=(tok // tm,),
        in_specs=[xspec, xspec,
                  pl.BlockSpec((None, D_MODEL, D_FF), lambda t: (layer, 0, 0), pipeline_mode=pl.Buffered(1)),
                  pl.BlockSpec((None, D_FF, D_MODEL), lambda t: (layer, 0, 0), pipeline_mode=pl.Buffered(1)),
                  pl.BlockSpec((None, 1, D_MODEL), lambda t: (layer, 0, 0))],
        out_specs=xspec,
        out_shape=jax.ShapeDtypeStruct((tok, D_MODEL), F32),
        compiler_params=pltpu.CompilerParams(
            dimension_semantics=("parallel",), vmem_limit_bytes=VMEM_LIMIT),
        name="mlp",
    )(h2, x2, w1, w2, g)


def _block_diag_dense(w):
    depth, nblk, n, _ = w.shape
    eye = jnp.eye(nblk, dtype=w.dtype)
    return jnp.einsum("dgij,gh->dgihj", w, eye).reshape(depth, nblk * n, nblk * n)


def _to_segment_order(x):
    bsz, seq, dm = x.shape
    return x.reshape(bsz, seq // UNIT, 8, SEG, dm).swapaxes(2, 3).reshape(bsz, seq, dm)


def _from_segment_order(x):
    bsz, seq, dm = x.shape
    return x.reshape(bsz, seq // UNIT, SEG, 8, dm).swapaxes(2, 3).reshape(bsz, seq, dm)


def kernel(x, pre_mix_g, w_in, conf_dw_w, conf_dw_b, conf_gn_g, conf_gn_b, sconv_w, lru_conv_w, lru_conv_b,
           lru_wa, lru_ba, lru_wx, lru_bx, lru_lam, w_out, post_mix_g, pre_mlp_g, mlp_w1, mlp_w2, post_mlp_g):
    bsz, seq, dm = x.shape
    depth = w_in.shape[0]
    assert dm == D_MODEL and seq % SEQ_TILE == 0 and seq % MOBA_BLOCK == 0 and (bsz * seq) % TOKEN_TILE == 0
    assert seq // MOBA_BLOCK > MOBA_TOP_K

    row = lambda a: a.reshape(depth, 1, a.shape[-1])
    taps = lambda w: jnp.broadcast_to(w[:, :, None, :], w.shape[:2] + (8, w.shape[-1]))
    group_of = jnp.arange(GROUP_W) // (GROUP_W // CONF_GROUPS)
    params = {
        "dww": taps(conf_dw_w), "dwb": row(conf_dw_b), "gng": row(conf_gn_g), "gnb": row(conf_gn_b),
        "scw": taps(sconv_w), "lcw": taps(lru_conv_w), "lcb": row(lru_conv_b),
        "wax": jnp.concatenate([_block_diag_dense(lru_wa), _block_diag_dense(lru_wx)], axis=-1).astype(BF16),
        "bax": row(jnp.concatenate([lru_ba, lru_bx], axis=-1)),
        "lam": row(lru_lam),
        "gm": ((group_of[:, None] == group_of[None, :]).astype(F32) / (GROUP_W // CONF_GROUPS)).astype(BF16),
    }
    w_in_b, w_out_b = w_in.astype(BF16), w_out.astype(BF16)
    w1_b, w2_b = mlp_w1.astype(BF16), mlp_w2.astype(BF16)
    g_pre_mix, g_post_mix = row(pre_mix_g), row(post_mix_g)
    g_pre_mlp, g_post_mlp = row(pre_mlp_g), row(post_mlp_g)

    x2 = _to_segment_order(x).reshape(bsz * seq, dm)
    for layer in range(depth):
        u = _in_proj(x2, g_pre_mix, w_in_b, layer)
        u3 = u.reshape(bsz, seq, IN_COLS)
        ya, yb, yd = _mixers(u3, params, layer)
        yc = _attention(u3)
        flat = lambda a: a.reshape(bsz * seq, GROUP_W)
        x2, h2 = _out_proj(flat(ya), flat(yb), flat(yc), flat(yd), x2, w_out_b, g_post_mix, g_pre_mlp, layer)
        x2 = _mlp(h2, x2, w1_b, w2_b, g_post_mlp, layer)
    return _from_segment_order(x2.reshape(bsz, seq, dm))
```

```python
import math

import jax
import jax.numpy as jnp
from jax import lax
from jax.experimental import pallas as pl
from jax.experimental.pallas import tpu as pltpu

F32 = jnp.float32
BF16 = jnp.bfloat16

D_MODEL = 1024
GROUP_W = 256
CONF_KERNEL = 31
CONF_GROUPS = 4
GN_EPS = 1e-5
SCONV_KERNEL = 3
ATT_HEADS = 4
HEAD_DIM = GROUP_W // ATT_HEADS
MOBA_BLOCK = 256
MOBA_TOP_K = 3
LRU_BLOCKS = 4
LRU_CONV = 4
LRU_C = 8.0
D_FF = 4 * D_MODEL
RMS_EPS = 1e-6
IN_COLS = 10 * GROUP_W

TOKEN_TILE = 512
SEQ_TILE = 512
UNIT = MOBA_BLOCK
SEG = UNIT // 8
FF_CHUNK = 1024
V_ROWS = HEAD_DIM + 16
VMEM_LIMIT = 56 * 1024 * 1024

NEG_INF = float("-inf")


def _rms(x, g):
    return x * lax.rsqrt(jnp.mean(x * x, axis=-1, keepdims=True) + RMS_EPS) * g


def _dot(a, b):
    return jnp.dot(a, b, preferred_element_type=F32)


def _in_proj_kernel(x_ref, g_ref, w_ref, u_ref):
    h = _rms(x_ref[...], g_ref[...]).astype(BF16)
    step = 2 * GROUP_W
    for c in range(0, IN_COLS, step):
        u_ref[:, c:c + step] = _dot(h, w_ref[:, c:c + step])


def _in_proj(x2, g, w, layer):
    tok = x2.shape[0]
    return pl.pallas_call(
        _in_proj_kernel,
        grid=(tok // TOKEN_TILE,),
        in_specs=[
            pl.BlockSpec((TOKEN_TILE, D_MODEL), lambda t: (t, 0)),
            pl.BlockSpec((None, 1, D_MODEL), lambda t: (layer, 0, 0)),
            pl.BlockSpec((None, D_MODEL, IN_COLS), lambda t: (layer, 0, 0)),
        ],
        out_specs=pl.BlockSpec((TOKEN_TILE, IN_COLS), lambda t: (t, 0)),
        out_shape=jax.ShapeDtypeStruct((tok, IN_COLS), F32),
        compiler_params=pltpu.CompilerParams(
            dimension_semantics=("parallel",), vmem_limit_bytes=VMEM_LIMIT),
        name="in_proj",
    )(x2, g, w)


def _group_mean(v, gm):
    hi = v.astype(BF16)
    lo = (v - hi.astype(F32)).astype(BF16)
    return _dot(hi, gm) + _dot(lo, gm)


def _seg_conv(z, e_ref, w_ref, bias, width, out_ref):
    sub = lax.broadcasted_iota(jnp.int32, (8, GROUP_W), 0)
    for a in range(SEG - (width - 1), SEG):
        cur = z[8 * a:8 * a + 8, :]
        prev = e_ref[UNIT + 8 * a:UNIT + 8 * a + 8, :]
        e_ref[8 * a:8 * a + 8, :] = pltpu.roll(jnp.where(sub == 7, prev, cur), 1, axis=0)
    e_ref[UNIT:2 * UNIT, :] = z
    for a0 in range(0, SEG, 8):
        accs = [bias] * 8
        for j in range(width):
            w_j = w_ref[width - 1 - j]
            for a in range(a0, a0 + 8):
                src = UNIT + 8 * (a - j) if a >= j else 8 * (a - j + SEG)
                term = w_j * e_ref[src:src + 8, :]
                accs[a - a0] = term if accs[a - a0] is None else accs[a - a0] + term
        for a in range(a0, a0 + 8):
            out_ref[8 * a:8 * a + 8, :] = accs[a - a0]


def _mixers_kernel(cv_ref, cg_ref, sb_ref, sc_ref, sx_ref, lr_ref, lg_ref,
                   dww_ref, dwb_ref, gng_ref, gnb_ref, scw_ref, lcw_ref, lcb_ref,
                   wax_ref, bax_ref, lam_ref, gm_ref,
                   ya_ref, yb_ref, yd_ref,
                   ea, eb, ed, cbuf, abuf, bbuf, hloc, acum, endb, hcar):
    @pl.when(pl.program_id(1) == 0)
    def _reset():
        for e in (ea, eb, ed):
            e[UNIT:2 * UNIT, :] = jnp.zeros((UNIT, GROUP_W), F32)
        hcar[...] = jnp.zeros_like(hcar)

    sub = lax.broadcasted_iota(jnp.int32, (8, GROUP_W), 0)
    bias_a = jnp.broadcast_to(dwb_ref[...], (8, GROUP_W))
    bias_d = jnp.broadcast_to(lcb_ref[...], (8, GROUP_W))
    lam = lam_ref[...]
    neg_c_softplus = -LRU_C * (jnp.maximum(-lam, 0.0) + jnp.log1p(jnp.exp(-jnp.abs(lam))))
    gm = gm_ref[...]

    def unit(u, carry):
        rows = pl.ds(pl.multiple_of(u * UNIT, UNIT), UNIT)

        _seg_conv(cv_ref[rows, :] * jax.nn.sigmoid(cg_ref[rows, :]), ea, dww_ref, bias_a, CONF_KERNEL, cbuf)
        c = cbuf[...]
        d = c - _group_mean(c, gm)
        var = _group_mean(d * d, gm)
        yn = d * lax.rsqrt(var + GN_EPS) * gng_ref[...] + gnb_ref[...]
        ya_ref[rows, :] = (yn * jax.nn.sigmoid(yn)).astype(ya_ref.dtype)

        _seg_conv(sc_ref[rows, :] * sx_ref[rows, :], eb, scw_ref, None, SCONV_KERNEL, cbuf)
        yb_ref[rows, :] = (sb_ref[rows, :] * cbuf[...]).astype(yb_ref.dtype)

        _seg_conv(lr_ref[rows, :], ed, lcw_ref, bias_d, LRU_CONV, cbuf)
        xc = cbuf[...]
        gates = _dot(xc.astype(BF16), wax_ref[...]) + bax_ref[...]
        r = jax.nn.sigmoid(gates[:, :GROUP_W])
        i = jax.nn.sigmoid(gates[:, GROUP_W:])
        log_a = neg_c_softplus * r
        abuf[...] = jnp.exp(log_a)
        th = jnp.tanh(log_a)
        bbuf[...] = jnp.sqrt(-2.0 * th / (1.0 - th)) * (i * xc)
        h_run = bbuf[0:8, :]
        a_run = abuf[0:8, :]
        hloc[0:8, :] = h_run
        acum[0:8, :] = a_run
        for s in range(1, SEG):
            a_s = abuf[8 * s:8 * s + 8, :]
            h_run = a_s * h_run + bbuf[8 * s:8 * s + 8, :]
            a_run = a_s * a_run
            hloc[8 * s:8 * s + 8, :] = h_run
            acum[8 * s:8 * s + 8, :] = a_run
        dist = 1
        while dist < 8:
            keep = sub >= dist
            h_run = jnp.where(keep, a_run * pltpu.roll(h_run, dist, axis=0) + h_run, h_run)
            a_run = jnp.where(keep, a_run * pltpu.roll(a_run, dist, axis=0), a_run)
            dist *= 2
        h_in = hcar[...]
        seg_end = a_run * h_in + h_run
        endb[...] = seg_end
        seg_in = jnp.where(sub == 0, h_in, pltpu.roll(seg_end, 1, axis=0))
        hcar[...] = jnp.broadcast_to(endb[7:8, :], (8, GROUP_W))
        for s in range(SEG):
            hloc[8 * s:8 * s + 8, :] = hloc[8 * s:8 * s + 8, :] + acum[8 * s:8 * s + 8, :] * seg_in
        g = lg_ref[rows, :]
        gelu = 0.5 * g * (1.0 + jnp.tanh(math.sqrt(2.0 / math.pi) * (g + 0.044715 * (g * g * g))))
        yd_ref[rows, :] = (hloc[...] * gelu).astype(yd_ref.dtype)
        return carry

    lax.fori_loop(0, SEQ_TILE // UNIT, unit, 0)


def _mixers(u3, p, layer):
    bsz, seq, _ = u3.shape
    ts = SEQ_TILE

    def ucol(j):
        return pl.BlockSpec((None, ts, GROUP_W), lambda b, s: (b, s, j))

    def par(shape):
        return pl.BlockSpec((None,) + shape, lambda b, s: (layer,) + (0,) * len(shape))

    out_spec = pl.BlockSpec((None, ts, GROUP_W), lambda b, s: (b, s, 0))
    out_sds = jax.ShapeDtypeStruct((bsz, seq, GROUP_W), BF16)
    return pl.pallas_call(
        _mixers_kernel,
        grid=(bsz, seq // ts),
        in_specs=[ucol(0), ucol(1), ucol(2), ucol(3), ucol(4), ucol(8), ucol(9),
                  par((CONF_KERNEL, 8, GROUP_W)), par((1, GROUP_W)), par((1, GROUP_W)), par((1, GROUP_W)),
                  par((SCONV_KERNEL, 8, GROUP_W)), par((LRU_CONV, 8, GROUP_W)), par((1, GROUP_W)),
                  par((GROUP_W, 2 * GROUP_W)), par((1, 2 * GROUP_W)), par((1, GROUP_W)),
                  pl.BlockSpec((GROUP_W, GROUP_W), lambda b, s: (0, 0))],
        out_specs=[out_spec, out_spec, out_spec],
        out_shape=[out_sds, out_sds, out_sds],
        scratch_shapes=(
            [pltpu.VMEM((2 * UNIT, GROUP_W), F32)] * 3
            + [pltpu.VMEM((UNIT, GROUP_W), F32)] * 5
            + [pltpu.VMEM((8, GROUP_W), F32)] * 2
        ),
        compiler_params=pltpu.CompilerParams(
            dimension_semantics=("parallel", "arbitrary"), vmem_limit_bytes=VMEM_LIMIT),
        name="mixers",
    )(u3, u3, u3, u3, u3, u3, u3,
      p["dww"], p["dwb"], p["gng"], p["gnb"], p["scw"], p["lcw"], p["lcb"],
      p["wax"], p["bax"], p["lam"], p["gm"])


_NT = (((1,), (1,)), ((), ()))


def _attn_kernel(q_ref, k_ref, v_ref, o_ref,
                 kb, vt, kmm, gate_s, cmask, pos_q, rbias, qm, s_even, s_odd, m_run, m_acc, acc):
    nb = kb.shape[0]
    blk = MOBA_BLOCK
    i = pl.program_id(1)
    log2e = math.log2(math.e)
    slopes2 = [log2e * 2.0 ** (-8.0 * (h + 1) / ATT_HEADS) for h in range(ATT_HEADS)]
    lane_head = lax.broadcasted_iota(jnp.int32, (1, GROUP_W), 1) // HEAD_DIM
    pos = lambda rho: SEG * (rho & 7) + (rho >> 3)

    @pl.when(i == 0)
    def _per_batch():
        pos_s = pos(lax.broadcasted_iota(jnp.int32, (1, blk), 1))
        rel = (pos_s - (blk - 1)).astype(F32)
        first_row = lax.broadcasted_iota(jnp.int32, (V_ROWS - HEAD_DIM, blk), 0) == 0
        for n in range(nb):
            kblk = k_ref[n * blk:(n + 1) * blk, :]
            kb[n] = kblk.astype(BF16)
            v_t = v_ref[n * blk:(n + 1) * blk, :].T
            kmean = jnp.mean(kblk, axis=0, keepdims=True)
            for h in range(ATT_HEADS):
                kmm[h * nb + n:h * nb + n + 1, :] = jnp.where(lane_head == h, kmean, 0.0)
                f = jnp.exp2(slopes2[h] * rel)
                vt[n, h * V_ROWS:h * V_ROWS + HEAD_DIM, :] = (v_t[h * HEAD_DIM:(h + 1) * HEAD_DIM, :] * f).astype(BF16)
                vt[n, h * V_ROWS + HEAD_DIM:(h + 1) * V_ROWS, :] = jnp.where(first_row, f, 0.0).astype(BF16)
        t_l = pos(lax.broadcasted_iota(jnp.int32, (blk, blk), 1))
        s_l = pos(lax.broadcasted_iota(jnp.int32, (blk, blk), 0))
        cmask[...] = jnp.where(t_l >= s_l, 0.0, NEG_INF)
        pos_q[...] = pos(lax.broadcasted_iota(jnp.int32, (nb, blk), 1)).astype(F32)

    q = q_ref[...] * (HEAD_DIM ** -0.5 * log2e)
    gate_s[...] = lax.dot_general(kmm[...], q, _NT, precision=lax.Precision.HIGHEST,
                                  preferred_element_type=F32)
    blk_id = lax.broadcasted_iota(jnp.int32, (nb, blk), 0)
    dist = (i - blk_id).astype(F32) * blk + (pos_q[...] - (blk - 1))
    k_own = kb[i]
    for h in range(ATT_HEADS):
        g = gate_s[h * nb:(h + 1) * nb, :]
        rank = jnp.zeros((nb, blk), jnp.int32)
        for m in range(nb):
            gm = gate_s[h * nb + m:h * nb + m + 1, :]
            beats = (gm > g) | ((gm == g) & (m < blk_id))
            rank = rank + jnp.where(beats, 1, 0) * jnp.where(m < i, 1, 0)
        visible = (blk_id == i) | ((blk_id < i) & (rank < MOBA_TOP_K))
        rbias[h] = jnp.where(visible, -slopes2[h] * dist, NEG_INF)
        qh = jnp.where(lane_head == h, q, 0.0).astype(BF16)
        qm[h] = qh
        s_t = lax.dot_general(k_own, qh, _NT, preferred_element_type=F32) + cmask[...]
        s_even[h] = s_t
        m0 = jnp.max(s_t, axis=0, keepdims=True) + rbias[h, pl.ds(i, 1), :]
        m_run[h] = m0
        m_acc[h] = m0
        acc[h] = jnp.zeros((V_ROWS, blk), F32)

    def accumulate(block, s_ref, m_now):
        for h in range(ATT_HEADS):
            alpha = jnp.exp2(m_acc[h] - m_now[h])
            p = jnp.exp2(s_ref[h] - (m_now[h] - rbias[h, pl.ds(block, 1), :]))
            acc[h] = alpha * acc[h] + _dot(vt[block, h * V_ROWS:(h + 1) * V_ROWS, :], p.astype(BF16))
            m_acc[h] = m_now[h]

    def step(j, s_new, s_old):
        m_before = [m_run[h] for h in range(ATT_HEADS)]
        k_n = kb[j - 1]
        for h in range(ATT_HEADS):
            s_t = lax.dot_general(k_n, qm[h], _NT, preferred_element_type=F32)
            s_new[h] = s_t
            m_run[h] = jnp.maximum(m_before[h], jnp.max(s_t, axis=0, keepdims=True) + rbias[h, pl.ds(j - 1, 1), :])
        accumulate(jnp.where(j == 1, i, j - 2), s_old, m_before)

    def finish(s_ref):
        accumulate(jnp.maximum(i - 1, 0), s_ref, [m_run[h] for h in range(ATT_HEADS)])

    def two_steps(jj, carry):
        step(2 * jj + 1, s_odd, s_even)
        step(2 * jj + 2, s_even, s_odd)
        return carry

    lax.fori_loop(0, i // 2, two_steps, 0)

    @pl.when(i % 2 == 1)
    def _odd_tail():
        step(i, s_odd, s_even)
        finish(s_odd)

    @pl.when(i % 2 == 0)
    def _even_tail():
        finish(s_even)

    out_t = jnp.concatenate([acc[h, 0:HEAD_DIM, :] / acc[h, HEAD_DIM:HEAD_DIM + 1, :] for h in range(ATT_HEADS)],
                            axis=0)
    o_ref[...] = out_t.T.astype(o_ref.dtype)


def _attention(u3):
    bsz, seq, _ = u3.shape
    nb = seq // MOBA_BLOCK
    blk = MOBA_BLOCK
    return pl.pallas_call(
        _attn_kernel,
        grid=(bsz, nb),
        in_specs=[
            pl.BlockSpec((None, blk, GROUP_W), lambda b, i: (b, i, 5)),
            pl.BlockSpec((None, seq, GROUP_W), lambda b, i: (b, 0, 6)),
            pl.BlockSpec((None, seq, GROUP_W), lambda b, i: (b, 0, 7)),
        ],
        out_specs=pl.BlockSpec((None, blk, GROUP_W), lambda b, i: (b, i, 0)),
        out_shape=jax.ShapeDtypeStruct((bsz, seq, GROUP_W), BF16),
        scratch_shapes=[
            pltpu.VMEM((nb, blk, GROUP_W), BF16),
            pltpu.VMEM((nb, ATT_HEADS * V_ROWS, blk), BF16),
            pltpu.VMEM((ATT_HEADS * nb, GROUP_W), F32),
            pltpu.VMEM((ATT_HEADS * nb, blk), F32),
            pltpu.VMEM((blk, blk), F32),
            pltpu.VMEM((nb, blk), F32),
            pltpu.VMEM((ATT_HEADS, nb, blk), F32),
            pltpu.VMEM((ATT_HEADS, blk, GROUP_W), BF16),
            pltpu.VMEM((ATT_HEADS, blk, blk), F32),
            pltpu.VMEM((ATT_HEADS, blk, blk), F32),
            pltpu.VMEM((ATT_HEADS, 1, blk), F32),
            pltpu.VMEM((ATT_HEADS, 1, blk), F32),
            pltpu.VMEM((ATT_HEADS, V_ROWS, blk), F32),
        ],
        compiler_params=pltpu.CompilerParams(
            dimension_semantics=("parallel", "arbitrary"), vmem_limit_bytes=VMEM_LIMIT),
        name="moba_attention",
    )(u3, u3, u3)


def _out_mlp_kernel(ya_ref, yb_ref, yc_ref, yd_ref, x_ref, wo_ref, gpost_ref, gpre_ref, w1_ref, w2_ref, gmlp_ref,
                    xo_ref):
    y = None
    for j, y_ref in enumerate((ya_ref, yb_ref, yc_ref, yd_ref)):
        term = _dot(y_ref[...], wo_ref[j * GROUP_W:(j + 1) * GROUP_W, :])
        y = term if y is None else y + term
    x = x_ref[...] + _rms(y, gpost_ref[...])
    h = _rms(x, gpre_ref[...]).astype(BF16)
    m = None
    for c in range(0, D_FF, FF_CHUNK):
        t = jnp.maximum(_dot(h, w1_ref[:, c:c + FF_CHUNK]), 0.0)
        term = _dot((t * t).astype(BF16), w2_ref[c:c + FF_CHUNK, :])
        m = term if m is None else m + term
    xo_ref[...] = x + _rms(m, gmlp_ref[...])


def _out_mlp(ya, yb, yc, yd, x2, wo, gpost, gpre, w1, w2, gmlp, layer):
    tok = x2.shape[0]
    tm = TOKEN_TILE
    yspec = pl.BlockSpec((tm, GROUP_W), lambda t: (t, 0))
    xspec = pl.BlockSpec((tm, D_MODEL), lambda t: (t, 0))
    gspec = pl.BlockSpec((None, 1, D_MODEL), lambda t: (layer, 0, 0))

    def resident(shape):
        return pl.BlockSpec((None,) + shape, lambda t: (layer, 0, 0), pipeline_mode=pl.Buffered(1))

    return pl.pallas_call(
        _out_mlp_kernel,
        grid=(tok // tm,),
        in_specs=[yspec, yspec, yspec, yspec, xspec, resident((D_MODEL, D_MODEL)), gspec, gspec,
                  resident((D_MODEL, D_FF)), resident((D_FF, D_MODEL)), gspec],
        out_specs=xspec,
        out_shape=jax.ShapeDtypeStruct((tok, D_MODEL), F32),
        compiler_params=pltpu.CompilerParams(
            dimension_semantics=("parallel",), vmem_limit_bytes=VMEM_LIMIT),
        name="out_mlp",
    )(ya, yb, yc, yd, x2, wo, gpost, gpre, w1, w2, gmlp)


def _block_diag_dense(w):
    depth, nblk, n, _ = w.shape
    eye = jnp.eye(nblk, dtype=w.dtype)
    return jnp.einsum("dgij,gh->dgihj", w, eye).reshape(depth, nblk * n, nblk * n)


def _to_segment_order(x):
    bsz, seq, dm = x.shape
    return x.reshape(bsz, seq // UNIT, 8, SEG, dm).swapaxes(2, 3).reshape(bsz, seq, dm)


def _from_segment_order(x):
    bsz, seq, dm = x.shape
    return x.reshape(bsz, seq // UNIT, SEG, 8, dm).swapaxes(2, 3).reshape(bsz, seq, dm)


def kernel(x, pre_mix_g, w_in, conf_dw_w, conf_dw_b, conf_gn_g, conf_gn_b, sconv_w, lru_conv_w, lru_conv_b,
           lru_wa, lru_ba, lru_wx, lru_bx, lru_lam, w_out, post_mix_g, pre_mlp_g, mlp_w1, mlp_w2, post_mlp_g):
    bsz, seq, dm = x.shape
    depth = w_in.shape[0]
    assert dm == D_MODEL and seq % SEQ_TILE == 0 and seq % MOBA_BLOCK == 0 and (bsz * seq) % TOKEN_TILE == 0
    assert seq // MOBA_BLOCK > MOBA_TOP_K

    row = lambda a: a.reshape(depth, 1, a.shape[-1])
    taps = lambda w: jnp.broadcast_to(w[:, :, None, :], w.shape[:2] + (8, w.shape[-1]))
    group_of = jnp.arange(GROUP_W) // (GROUP_W // CONF_GROUPS)
    params = {
        "dww": taps(conf_dw_w), "dwb": row(conf_dw_b), "gng": row(conf_gn_g), "gnb": row(conf_gn_b),
        "scw": taps(sconv_w), "lcw": taps(lru_conv_w), "lcb": row(lru_conv_b),
        "wax": jnp.concatenate([_block_diag_dense(lru_wa), _block_diag_dense(lru_wx)], axis=-1).astype(BF16),
        "bax": row(jnp.concatenate([lru_ba, lru_bx], axis=-1)),
        "lam": row(lru_lam),
        "gm": ((group_of[:, None] == group_of[None, :]).astype(F32) / (GROUP_W // CONF_GROUPS)).astype(BF16),
    }
    w_in_b, w_out_b = w_in.astype(BF16), w_out.astype(BF16)
    w1_b, w2_b = mlp_w1.astype(BF16), mlp_w2.astype(BF16)
    g_pre_mix, g_post_mix = row(pre_mix_g), row(post_mix_g)
    g_pre_mlp, g_post_mlp = row(pre_mlp_g), row(post_mlp_g)

    x2 = _to_segment_order(x).reshape(bsz * seq, dm)
    for layer in range(depth):
        u = _in_proj(x2, g_pre_mix, w_in_b, layer)
        u3 = u.reshape(bsz, seq, IN_COLS)
        ya, yb, yd = _mixers(u3, params, layer)
        yc = _attention(u3)
        flat = lambda a: a.reshape(bsz * seq, GROUP_W)
        x2 = _out_mlp(flat(ya), flat(yb), flat(yc), flat(yd), x2, w_out_b, g_post_mix, g_pre_mlp,
                      w1_b, w2_b, g_post_mlp, layer)
    return _from_segment_order(x2.reshape(bsz, seq, dm))
```

```python
import math

import jax
import jax.numpy as jnp
from jax import lax
from jax.experimental import pallas as pl
from jax.experimental.pallas import tpu as pltpu

F32 = jnp.float32
BF16 = jnp.bfloat16

D_MODEL = 1024
GROUP_W = 256
CONF_KERNEL = 31
CONF_GROUPS = 4
GN_EPS = 1e-5
SCONV_KERNEL = 3
ATT_HEADS = 4
HEAD_DIM = GROUP_W // ATT_HEADS
MOBA_BLOCK = 256
MOBA_TOP_K = 3
LRU_BLOCKS = 4
LRU_CONV = 4
LRU_C = 8.0
D_FF = 4 * D_MODEL
RMS_EPS = 1e-6
IN_COLS = 10 * GROUP_W

TOKEN_TILE = 512
SEQ_TILE = 512
UNIT = MOBA_BLOCK
SEG = UNIT // 8
FF_CHUNK = 1024
V_ROWS = HEAD_DIM + 16
VMEM_LIMIT = 56 * 1024 * 1024

NEG_INF = float("-inf")


def _rms(x, g):
    return x * lax.rsqrt(jnp.mean(x * x, axis=-1, keepdims=True) + RMS_EPS) * g


def _dot(a, b):
    return jnp.dot(a, b, preferred_element_type=F32)


def _in_proj_kernel(x_ref, g_ref, w_ref, u_ref):
    h = _rms(x_ref[...], g_ref[...]).astype(BF16)
    step = 2 * GROUP_W
    for c in range(0, IN_COLS, step):
        u_ref[:, c:c + step] = _dot(h, w_ref[:, c:c + step])


def _in_proj(x2, g, w, layer):
    tok = x2.shape[0]
    return pl.pallas_call(
        _in_proj_kernel,
        grid=(tok // TOKEN_TILE,),
        in_specs=[
            pl.BlockSpec((TOKEN_TILE, D_MODEL), lambda t: (t, 0)),
            pl.BlockSpec((None, 1, D_MODEL), lambda t: (layer, 0, 0)),
            pl.BlockSpec((None, D_MODEL, IN_COLS), lambda t: (layer, 0, 0)),
        ],
        out_specs=pl.BlockSpec((TOKEN_TILE, IN_COLS), lambda t: (t, 0)),
        out_shape=jax.ShapeDtypeStruct((tok, IN_COLS), F32),
        compiler_params=pltpu.CompilerParams(
            dimension_semantics=("parallel",), vmem_limit_bytes=VMEM_LIMIT),
        name="in_proj",
    )(x2, g, w)


def _group_mean(v, gm):
    hi = v.astype(BF16)
    lo = (v - hi.astype(F32)).astype(BF16)
    return _dot(hi, gm) + _dot(lo, gm)


def _seg_conv(z, e_ref, w_ref, bias, width, out_ref):
    sub = lax.broadcasted_iota(jnp.int32, (8, GROUP_W), 0)
    for a in range(SEG - (width - 1), SEG):
        cur = z[8 * a:8 * a + 8, :]
        prev = e_ref[UNIT + 8 * a:UNIT + 8 * a + 8, :]
        e_ref[8 * a:8 * a + 8, :] = pltpu.roll(jnp.where(sub == 7, prev, cur), 1, axis=0)
    e_ref[UNIT:2 * UNIT, :] = z
    for a0 in range(0, SEG, 8):
        accs = [bias] * 8
        for j in range(width):
            w_j = w_ref[width - 1 - j]
            for a in range(a0, a0 + 8):
                src = UNIT + 8 * (a - j) if a >= j else 8 * (a - j + SEG)
                term = w_j * e_ref[src:src + 8, :]
                accs[a - a0] = term if accs[a - a0] is None else accs[a - a0] + term
        for a in range(a0, a0 + 8):
            out_ref[8 * a:8 * a + 8, :] = accs[a - a0]


def _mixers_kernel(cv_ref, cg_ref, sb_ref, sc_ref, sx_ref, lr_ref, lg_ref,
                   dww_ref, dwb_ref, gng_ref, gnb_ref, scw_ref, lcw_ref, lcb_ref,
                   wax_ref, bax_ref, lam_ref, gm_ref,
                   ya_ref, yb_ref, yd_ref,
                   ea, eb, ed, cbuf, abuf, bbuf, hloc, acum, endb, hcar):
    @pl.when(pl.program_id(1) == 0)
    def _reset():
        for e in (ea, eb, ed):
            e[UNIT:2 * UNIT, :] = jnp.zeros((UNIT, GROUP_W), F32)
        hcar[...] = jnp.zeros_like(hcar)

    sub = lax.broadcasted_iota(jnp.int32, (8, GROUP_W), 0)
    bias_a = jnp.broadcast_to(dwb_ref[...], (8, GROUP_W))
    bias_d = jnp.broadcast_to(lcb_ref[...], (8, GROUP_W))
    lam = lam_ref[...]
    neg_c_softplus = -LRU_C * (jnp.maximum(-lam, 0.0) + jnp.log1p(jnp.exp(-jnp.abs(lam))))
    gm = gm_ref[...]

    def unit(u, carry):
        rows = pl.ds(pl.multiple_of(u * UNIT, UNIT), UNIT)

        _seg_conv(cv_ref[rows, :] * jax.nn.sigmoid(cg_ref[rows, :]), ea, dww_ref, bias_a, CONF_KERNEL, cbuf)
        c = cbuf[...]
        d = c - _group_mean(c, gm)
        var = _group_mean(d * d, gm)
        yn = d * lax.rsqrt(var + GN_EPS) * gng_ref[...] + gnb_ref[...]
        ya_ref[rows, :] = (yn * jax.nn.sigmoid(yn)).astype(ya_ref.dtype)

        _seg_conv(sc_ref[rows, :] * sx_ref[rows, :], eb, scw_ref, None, SCONV_KERNEL, cbuf)
        yb_ref[rows, :] = (sb_ref[rows, :] * cbuf[...]).astype(yb_ref.dtype)

        _seg_conv(lr_ref[rows, :], ed, lcw_ref, bias_d, LRU_CONV, cbuf)
        xc = cbuf[...]
        gates = _dot(xc.astype(BF16), wax_ref[...]) + bax_ref[...]
        r = jax.nn.sigmoid(gates[:, :GROUP_W])
        i = jax.nn.sigmoid(gates[:, GROUP_W:])
        log_a = neg_c_softplus * r
        abuf[...] = jnp.exp(log_a)
        th = jnp.tanh(log_a)
        bbuf[...] = jnp.sqrt(-2.0 * th / (1.0 - th)) * (i * xc)
        h_run = bbuf[0:8, :]
        a_run = abuf[0:8, :]
        hloc[0:8, :] = h_run
        acum[0:8, :] = a_run
        for s in range(1, SEG):
            a_s = abuf[8 * s:8 * s + 8, :]
            h_run = a_s * h_run + bbuf[8 * s:8 * s + 8, :]
            a_run = a_s * a_run
            hloc[8 * s:8 * s + 8, :] = h_run
            acum[8 * s:8 * s + 8, :] = a_run
        dist = 1
        while dist < 8:
            keep = sub >= dist
            h_run = jnp.where(keep, a_run * pltpu.roll(h_run, dist, axis=0) + h_run, h_run)
            a_run = jnp.where(keep, a_run * pltpu.roll(a_run, dist, axis=0), a_run)
            dist *= 2
        h_in = hcar[...]
        seg_end = a_run * h_in + h_run
        endb[...] = seg_end
        seg_in = jnp.where(sub == 0, h_in, pltpu.roll(seg_end, 1, axis=0))
        hcar[...] = jnp.broadcast_to(endb[7:8, :], (8, GROUP_W))
        for s in range(SEG):
            hloc[8 * s:8 * s + 8, :] = hloc[8 * s:8 * s + 8, :] + acum[8 * s:8 * s + 8, :] * seg_in
        g = lg_ref[rows, :]
        gelu = 0.5 * g * (1.0 + jnp.tanh(math.sqrt(2.0 / math.pi) * (g + 0.044715 * (g * g * g))))
        yd_ref[rows, :] = (hloc[...] * gelu).astype(yd_ref.dtype)
        return carry

    lax.fori_loop(0, SEQ_TILE // UNIT, unit, 0)


def _mixers(u3, p, layer):
    bsz, seq, _ = u3.shape
    ts = SEQ_TILE

    def ucol(j):
        return pl.BlockSpec((None, ts, GROUP_W), lambda b, s: (b, s, j))

    def par(shape):
        return pl.BlockSpec((None,) + shape, lambda b, s: (layer,) + (0,) * len(shape))

    out_spec = pl.BlockSpec((None, ts, GROUP_W), lambda b, s: (b, s, 0))
    out_sds = jax.ShapeDtypeStruct((bsz, seq, GROUP_W), BF16)
    return pl.pallas_call(
        _mixers_kernel,
        grid=(bsz, seq // ts),
        in_specs=[ucol(0), ucol(1), ucol(2), ucol(3), ucol(4), ucol(8), ucol(9),
                  par((CONF_KERNEL, 8, GROUP_W)), par((1, GROUP_W)), par((1, GROUP_W)), par((1, GROUP_W)),
                  par((SCONV_KERNEL, 8, GROUP_W)), par((LRU_CONV, 8, GROUP_W)), par((1, GROUP_W)),
                  par((GROUP_W, 2 * GROUP_W)), par((1, 2 * GROUP_W)), par((1, GROUP_W)),
                  pl.BlockSpec((GROUP_W, GROUP_W), lambda b, s: (0, 0))],
        out_specs=[out_spec, out_spec, out_spec],
        out_shape=[out_sds, out_sds, out_sds],
        scratch_shapes=(
            [pltpu.VMEM((2 * UNIT, GROUP_W), F32)] * 3
            + [pltpu.VMEM((UNIT, GROUP_W), F32)] * 5
            + [pltpu.VMEM((8, GROUP_W), F32)] * 2
        ),
        compiler_params=pltpu.CompilerParams(
            dimension_semantics=("parallel", "arbitrary"), vmem_limit_bytes=VMEM_LIMIT),
        name="mixers",
    )(u3, u3, u3, u3, u3, u3, u3,
      p["dww"], p["dwb"], p["gng"], p["gnb"], p["scw"], p["lcw"], p["lcb"],
      p["wax"], p["bax"], p["lam"], p["gm"])


_NT = (((1,), (1,)), ((), ()))


def _attn_kernel(q_ref, k_ref, v_ref, o_ref, kb, vt, kmm, gate_s, cmask, pos_q, rbias, qm, s_buf, acc):
    nb = kb.shape[0]
    blk = MOBA_BLOCK
    heads = range(ATT_HEADS)
    log2e = math.log2(math.e)
    slopes2 = [log2e * 2.0 ** (-8.0 * (h + 1) / ATT_HEADS) for h in heads]
    lane_head = lax.broadcasted_iota(jnp.int32, (1, GROUP_W), 1) // HEAD_DIM
    pos = lambda rho: SEG * (rho & 7) + (rho >> 3)

    pos_s = pos(lax.broadcasted_iota(jnp.int32, (1, blk), 1))
    rel = (pos_s - (blk - 1)).astype(F32)
    first_row = lax.broadcasted_iota(jnp.int32, (V_ROWS - HEAD_DIM, blk), 0) == 0
    for n in range(nb):
        kblk = k_ref[n * blk:(n + 1) * blk, :]
        kb[n] = kblk.astype(BF16)
        v_t = v_ref[n * blk:(n + 1) * blk, :].T
        kmean = jnp.mean(kblk, axis=0, keepdims=True)
        for h in heads:
            kmm[h * nb + n:h * nb + n + 1, :] = jnp.where(lane_head == h, kmean, 0.0)
            f = jnp.exp2(slopes2[h] * rel)
            vt[n, h * V_ROWS:h * V_ROWS + HEAD_DIM, :] = (v_t[h * HEAD_DIM:(h + 1) * HEAD_DIM, :] * f).astype(BF16)
            vt[n, h * V_ROWS + HEAD_DIM:(h + 1) * V_ROWS, :] = jnp.where(first_row, f, 0.0).astype(BF16)
    t_l = pos(lax.broadcasted_iota(jnp.int32, (blk, blk), 1))
    s_l = pos(lax.broadcasted_iota(jnp.int32, (blk, blk), 0))
    cmask[...] = jnp.where(t_l >= s_l, 0.0, NEG_INF)
    pos_q[...] = pos(lax.broadcasted_iota(jnp.int32, (nb, blk), 1)).astype(F32)
    blk_id = lax.broadcasted_iota(jnp.int32, (nb, blk), 0)

    def prepare(i):
        par = i % 2
        q = q_ref[i * blk:(i + 1) * blk, :] * (HEAD_DIM ** -0.5 * log2e)
        for h in heads:
            qm[par, h] = jnp.where(lane_head == h, q, 0.0).astype(BF16)
        gate_s[...] = lax.dot_general(kmm[...], q, _NT, precision=lax.Precision.HIGHEST,
                                      preferred_element_type=F32)
        dist = (i - blk_id).astype(F32) * blk + (pos_q[...] - (blk - 1))
        for h in heads:
            g = gate_s[h * nb:(h + 1) * nb, :]
            rank = jnp.zeros((nb, blk), jnp.int32)
            for m in range(i):
                gm = gate_s[h * nb + m:h * nb + m + 1, :]
                rank = rank + jnp.where((gm > g) | ((gm == g) & (m < blk_id)), 1, 0)
            visible = (blk_id == i) | ((blk_id < i) & (rank < MOBA_TOP_K))
            rbias[par, h] = jnp.where(visible, -slopes2[h] * dist, NEG_INF)

    def score(k, i, n, m_run):
        out = []
        for h in heads:
            s_t = lax.dot_general(kb[n], qm[i % 2, h], _NT, preferred_element_type=F32)
            if n == i:
                s_t = s_t + cmask[...]
            s_buf[k % 2, h] = s_t
            m_blk = jnp.max(s_t, axis=0, keepdims=True) + rbias[i % 2, h, n:n + 1, :]
            out.append(m_blk if n == i else jnp.maximum(m_run[h], m_blk))
        return out

    def fold(k, i, n, m_now, m_acc):
        for h in heads:
            p = jnp.exp2(s_buf[k % 2, h] - (m_now[h] - rbias[i % 2, h, n:n + 1, :]))
            pv = _dot(vt[n, h * V_ROWS:(h + 1) * V_ROWS, :], p.astype(BF16))
            acc[i % 2, h] = pv if n == i else jnp.exp2(m_acc[h] - m_now[h]) * acc[i % 2, h] + pv

    def finalize(i):
        out_t = jnp.concatenate(
            [acc[i % 2, h, 0:HEAD_DIM, :] / acc[i % 2, h, HEAD_DIM:HEAD_DIM + 1, :] for h in heads], axis=0)
        o_ref[i * blk:(i + 1) * blk, :] = out_t.T.astype(o_ref.dtype)

    units = [(i, n) for i in range(nb) for n in [i] + list(range(i))]
    prepare(0)
    m_run = score(0, 0, 0, None)
    prepare(1)
    m_fold = None
    for k in range(1, len(units) + 1):
        pi, pn = units[k - 1]
        m_prev = m_run
        if k < len(units):
            i, n = units[k]
            m_run = score(k, i, n, m_run)
        fold(k - 1, pi, pn, m_prev, m_fold)
        m_fold = m_prev
        if pn == pi - 1 or pi == 0:
            finalize(pi)
        if k < len(units) and n == i and i + 1 < nb:
            prepare(i + 1)


def _attention(u3):
    bsz, seq, _ = u3.shape
    nb = seq // MOBA_BLOCK
    blk = MOBA_BLOCK

    def ucol(j):
        return pl.BlockSpec((None, seq, GROUP_W), lambda b: (b, 0, j))

    return pl.pallas_call(
        _attn_kernel,
        grid=(bsz,),
        in_specs=[ucol(5), ucol(6), ucol(7)],
        out_specs=pl.BlockSpec((None, seq, GROUP_W), lambda b: (b, 0, 0)),
        out_shape=jax.ShapeDtypeStruct((bsz, seq, GROUP_W), BF16),
        scratch_shapes=[
            pltpu.VMEM((nb, blk, GROUP_W), BF16),
            pltpu.VMEM((nb, ATT_HEADS * V_ROWS, blk), BF16),
            pltpu.VMEM((ATT_HEADS * nb, GROUP_W), F32),
            pltpu.VMEM((ATT_HEADS * nb, blk), F32),
            pltpu.VMEM((blk, blk), F32),
            pltpu.VMEM((nb, blk), F32),
            pltpu.VMEM((2, ATT_HEADS, nb, blk), F32),
            pltpu.VMEM((2, ATT_HEADS, blk, GROUP_W), BF16),
            pltpu.VMEM((2, ATT_HEADS, blk, blk), F32),
            pltpu.VMEM((2, ATT_HEADS, V_ROWS, blk), F32),
        ],
        compiler_params=pltpu.CompilerParams(
            dimension_semantics=("parallel",), vmem_limit_bytes=VMEM_LIMIT),
        name="moba_attention",
    )(u3, u3, u3)


def _out_mlp_kernel(ya_ref, yb_ref, yc_ref, yd_ref, x_ref, wo_ref, gpost_ref, gpre_ref, w1_ref, w2_ref, gmlp_ref,
                    xo_ref):
    y = None
    for j, y_ref in enumerate((ya_ref, yb_ref, yc_ref, yd_ref)):
        term = _dot(y_ref[...], wo_ref[j * GROUP_W:(j + 1) * GROUP_W, :])
        y = term if y is None else y + term
    x = x_ref[...] + _rms(y, gpost_ref[...])
    h = _rms(x, gpre_ref[...]).astype(BF16)
    m = None
    for c in range(0, D_FF, FF_CHUNK):
        t = jnp.maximum(_dot(h, w1_ref[:, c:c + FF_CHUNK]), 0.0)
        term = _dot((t * t).astype(BF16), w2_ref[c:c + FF_CHUNK, :])
        m = term if m is None else m + term
    xo_ref[...] = x + _rms(m, gmlp_ref[...])


def _out_mlp(ya, yb, yc, yd, x2, wo, gpost, gpre, w1, w2, gmlp, layer):
    tok = x2.shape[0]
    tm = TOKEN_TILE
    yspec = pl.BlockSpec((tm, GROUP_W), lambda t: (t, 0))
    xspec = pl.BlockSpec((tm, D_MODEL), lambda t: (t, 0))
    gspec = pl.BlockSpec((None, 1, D_MODEL), lambda t: (layer, 0, 0))

    def resident(shape):
        return pl.BlockSpec((None,) + shape, lambda t: (layer, 0, 0), pipeline_mode=pl.Buffered(1))

    return pl.pallas_call(
        _out_mlp_kernel,
        grid=(tok // tm,),
        in_specs=[yspec, yspec, yspec, yspec, xspec, resident((D_MODEL, D_MODEL)), gspec, gspec,
                  resident((D_MODEL, D_FF)), resident((D_FF, D_MODEL)), gspec],
        out_specs=xspec,
        out_shape=jax.ShapeDtypeStruct((tok, D_MODEL), F32),
        compiler_params=pltpu.CompilerParams(
            dimension_semantics=("parallel",), vmem_limit_bytes=VMEM_LIMIT),
        name="out_mlp",
    )(ya, yb, yc, yd, x2, wo, gpost, gpre, w1, w2, gmlp)


def _block_diag_dense(w):
    depth, nblk, n, _ = w.shape
    eye = jnp.eye(nblk, dtype=w.dtype)
    return jnp.einsum("dgij,gh->dgihj", w, eye).reshape(depth, nblk * n, nblk * n)


def _to_segment_order(x):
    bsz, seq, dm = x.shape
    return x.reshape(bsz, seq // UNIT, 8, SEG, dm).swapaxes(2, 3).reshape(bsz, seq, dm)


def _from_segment_order(x):
    bsz, seq, dm = x.shape
    return x.reshape(bsz, seq // UNIT, SEG, 8, dm).swapaxes(2, 3).reshape(bsz, seq, dm)


def kernel(x, pre_mix_g, w_in, conf_dw_w, conf_dw_b, conf_gn_g, conf_gn_b, sconv_w, lru_conv_w, lru_conv_b,
           lru_wa, lru_ba, lru_wx, lru_bx, lru_lam, w_out, post_mix_g, pre_mlp_g, mlp_w1, mlp_w2, post_mlp_g):
    bsz, seq, dm = x.shape
    depth = w_in.shape[0]
    assert dm == D_MODEL and seq % SEQ_TILE == 0 and seq % MOBA_BLOCK == 0 and (bsz * seq) % TOKEN_TILE == 0
    assert seq // MOBA_BLOCK > MOBA_TOP_K

    row = lambda a: a.reshape(depth, 1, a.shape[-1])
    taps = lambda w: jnp.broadcast_to(w[:, :, None, :], w.shape[:2] + (8, w.shape[-1]))
    group_of = jnp.arange(GROUP_W) // (GROUP_W // CONF_GROUPS)
    params = {
        "dww": taps(conf_dw_w), "dwb": row(conf_dw_b), "gng": row(conf_gn_g), "gnb": row(conf_gn_b),
        "scw": taps(sconv_w), "lcw": taps(lru_conv_w), "lcb": row(lru_conv_b),
        "wax": jnp.concatenate([_block_diag_dense(lru_wa), _block_diag_dense(lru_wx)], axis=-1).astype(BF16),
        "bax": row(jnp.concatenate([lru_ba, lru_bx], axis=-1)),
        "lam": row(lru_lam),
        "gm": ((group_of[:, None] == group_of[None, :]).astype(F32) / (GROUP_W // CONF_GROUPS)).astype(BF16),
    }
    w_in_b, w_out_b = w_in.astype(BF16), w_out.astype(BF16)
    w1_b, w2_b = mlp_w1.astype(BF16), mlp_w2.astype(BF16)
    g_pre_mix, g_post_mix = row(pre_mix_g), row(post_mix_g)
    g_pre_mlp, g_post_mlp = row(pre_mlp_g), row(post_mlp_g)

    x2 = _to_segment_order(x).reshape(bsz * seq, dm)
    for layer in range(depth):
        u = _in_proj(x2, g_pre_mix, w_in_b, layer)
        u3 = u.reshape(bsz, seq, IN_COLS)
        ya, yb, yd = _mixers(u3, params, layer)
        yc = _attention(u3)
        flat = lambda a: a.reshape(bsz * seq, GROUP_W)
        x2 = _out_mlp(flat(ya), flat(yb), flat(yc), flat(yd), x2, w_out_b, g_post_mix, g_pre_mlp,
                      w1_b, w2_b, g_post_mlp, layer)
    return _from_segment_order(x2.reshape(bsz, seq, dm))
```

```python
import math

import jax
import jax.numpy as jnp
from jax import lax
from jax.experimental import pallas as pl
from jax.experimental.pallas import tpu as pltpu

F32 = jnp.float32
BF16 = jnp.bfloat16

D_MODEL = 1024
GROUP_W = 256
CONF_KERNEL = 31
CONF_GROUPS = 4
GN_EPS = 1e-5
SCONV_KERNEL = 3
ATT_HEADS = 4
HEAD_DIM = GROUP_W // ATT_HEADS
MOBA_BLOCK = 256
MOBA_TOP_K = 3
LRU_BLOCKS = 4
LRU_CONV = 4
LRU_C = 8.0
D_FF = 4 * D_MODEL
RMS_EPS = 1e-6
IN_COLS = 10 * GROUP_W

TOKEN_TILE = 1024
SEQ_TILE = 1024
UNIT = MOBA_BLOCK
SEG = UNIT // 8
FF_CHUNK = 1024
V_ROWS = HEAD_DIM + 16
VMEM_LIMIT = 56 * 1024 * 1024

NEG_INF = float("-inf")


def _rms(x, g):
    return x * lax.rsqrt(jnp.mean(x * x, axis=-1, keepdims=True) + RMS_EPS) * g


def _dot(a, b):
    return jnp.dot(a, b, preferred_element_type=F32)


def _in_proj_kernel(x_ref, g_ref, w_ref, u_ref):
    h = _rms(x_ref[...], g_ref[...]).astype(BF16)
    step = 2 * GROUP_W
    for c in range(0, IN_COLS, step):
        u_ref[:, c:c + step] = _dot(h, w_ref[:, c:c + step])


def _in_proj(x2, g, w, layer):
    tok = x2.shape[0]
    return pl.pallas_call(
        _in_proj_kernel,
        grid=(tok // TOKEN_TILE,),
        in_specs=[
            pl.BlockSpec((TOKEN_TILE, D_MODEL), lambda t: (t, 0)),
            pl.BlockSpec((None, 1, D_MODEL), lambda t: (layer, 0, 0)),
            pl.BlockSpec((None, D_MODEL, IN_COLS), lambda t: (layer, 0, 0)),
        ],
        out_specs=pl.BlockSpec((TOKEN_TILE, IN_COLS), lambda t: (t, 0)),
        out_shape=jax.ShapeDtypeStruct((tok, IN_COLS), F32),
        compiler_params=pltpu.CompilerParams(
            dimension_semantics=("parallel",), vmem_limit_bytes=VMEM_LIMIT),
        name="in_proj",
    )(x2, g, w)


def _group_mean(v, gm):
    hi = v.astype(BF16)
    lo = (v - hi.astype(F32)).astype(BF16)
    return _dot(hi, gm) + _dot(lo, gm)


def _seg_conv(z, e_ref, w_ref, bias, width, out_ref):
    sub = lax.broadcasted_iota(jnp.int32, (8, GROUP_W), 0)
    for a in range(SEG - (width - 1), SEG):
        cur = z[8 * a:8 * a + 8, :]
        prev = e_ref[UNIT + 8 * a:UNIT + 8 * a + 8, :]
        e_ref[8 * a:8 * a + 8, :] = pltpu.roll(jnp.where(sub == 7, prev, cur), 1, axis=0)
    e_ref[UNIT:2 * UNIT, :] = z
    for a0 in range(0, SEG, 8):
        accs = [bias] * 8
        for j in range(width):
            w_j = w_ref[width - 1 - j]
            for a in range(a0, a0 + 8):
                src = UNIT + 8 * (a - j) if a >= j else 8 * (a - j + SEG)
                term = w_j * e_ref[src:src + 8, :]
                accs[a - a0] = term if accs[a - a0] is None else accs[a - a0] + term
        for a in range(a0, a0 + 8):
            out_ref[8 * a:8 * a + 8, :] = accs[a - a0]


def _mixers_kernel(cv_ref, cg_ref, sb_ref, sc_ref, sx_ref, lr_ref, lg_ref,
                   dww_ref, dwb_ref, gng_ref, gnb_ref, scw_ref, lcw_ref, lcb_ref,
                   wax_ref, bax_ref, lam_ref, gm_ref,
                   ya_ref, yb_ref, yd_ref,
                   ea, eb, ed, cbuf, abuf, bbuf, hloc, acum, endb, hcar):
    @pl.when(pl.program_id(1) == 0)
    def _reset():
        for e in (ea, eb, ed):
            e[UNIT:2 * UNIT, :] = jnp.zeros((UNIT, GROUP_W), F32)
        hcar[...] = jnp.zeros_like(hcar)

    sub = lax.broadcasted_iota(jnp.int32, (8, GROUP_W), 0)
    bias_a = jnp.broadcast_to(dwb_ref[...], (8, GROUP_W))
    bias_d = jnp.broadcast_to(lcb_ref[...], (8, GROUP_W))
    lam = lam_ref[...]
    neg_c_softplus = -LRU_C * (jnp.maximum(-lam, 0.0) + jnp.log1p(jnp.exp(-jnp.abs(lam))))
    gm = gm_ref[...]

    def unit(u, carry):
        rows = pl.ds(pl.multiple_of(u * UNIT, UNIT), UNIT)

        _seg_conv(cv_ref[rows, :] * jax.nn.sigmoid(cg_ref[rows, :]), ea, dww_ref, bias_a, CONF_KERNEL, cbuf)
        c = cbuf[...]
        d = c - _group_mean(c, gm)
        var = _group_mean(d * d, gm)
        yn = d * lax.rsqrt(var + GN_EPS) * gng_ref[...] + gnb_ref[...]
        ya_ref[rows, :] = (yn * jax.nn.sigmoid(yn)).astype(ya_ref.dtype)

        _seg_conv(sc_ref[rows, :] * sx_ref[rows, :], eb, scw_ref, None, SCONV_KERNEL, cbuf)
        yb_ref[rows, :] = (sb_ref[rows, :] * cbuf[...]).astype(yb_ref.dtype)

        _seg_conv(lr_ref[rows, :], ed, lcw_ref, bias_d, LRU_CONV, cbuf)
        xc = cbuf[...]
        gates = _dot(xc.astype(BF16), wax_ref[...]) + bax_ref[...]
        r = jax.nn.sigmoid(gates[:, :GROUP_W])
        i = jax.nn.sigmoid(gates[:, GROUP_W:])
        log_a = neg_c_softplus * r
        abuf[...] = jnp.exp(log_a)
        th = jnp.tanh(log_a)
        bbuf[...] = jnp.sqrt(-2.0 * th / (1.0 - th)) * (i * xc)
        h_run = bbuf[0:8, :]
        a_run = abuf[0:8, :]
        hloc[0:8, :] = h_run
        acum[0:8, :] = a_run
        for s in range(1, SEG):
            a_s = abuf[8 * s:8 * s + 8, :]
            h_run = a_s * h_run + bbuf[8 * s:8 * s + 8, :]
            a_run = a_s * a_run
            hloc[8 * s:8 * s + 8, :] = h_run
            acum[8 * s:8 * s + 8, :] = a_run
        dist = 1
        while dist < 8:
            keep = sub >= dist
            h_run = jnp.where(keep, a_run * pltpu.roll(h_run, dist, axis=0) + h_run, h_run)
            a_run = jnp.where(keep, a_run * pltpu.roll(a_run, dist, axis=0), a_run)
            dist *= 2
        h_in = hcar[...]
        seg_end = a_run * h_in + h_run
        endb[...] = seg_end
        seg_in = jnp.where(sub == 0, h_in, pltpu.roll(seg_end, 1, axis=0))
        hcar[...] = jnp.broadcast_to(endb[7:8, :], (8, GROUP_W))
        for s in range(SEG):
            hloc[8 * s:8 * s + 8, :] = hloc[8 * s:8 * s + 8, :] + acum[8 * s:8 * s + 8, :] * seg_in
        g = lg_ref[rows, :]
        gelu = 0.5 * g * (1.0 + jnp.tanh(math.sqrt(2.0 / math.pi) * (g + 0.044715 * (g * g * g))))
        yd_ref[rows, :] = (hloc[...] * gelu).astype(yd_ref.dtype)
        return carry

    lax.fori_loop(0, SEQ_TILE // UNIT, unit, 0)


def _mixers(u3, p, layer):
    bsz, seq, _ = u3.shape
    ts = SEQ_TILE

    def ucol(j):
        return pl.BlockSpec((None, ts, GROUP_W), lambda b, s: (b, s, j))

    def par(shape):
        return pl.BlockSpec((None,) + shape, lambda b, s: (layer,) + (0,) * len(shape))

    out_spec = pl.BlockSpec((None, ts, GROUP_W), lambda b, s: (b, s, 0))
    out_sds = jax.ShapeDtypeStruct((bsz, seq, GROUP_W), BF16)
    return pl.pallas_call(
        _mixers_kernel,
        grid=(bsz, seq // ts),
        in_specs=[ucol(0), ucol(1), ucol(2), ucol(3), ucol(4), ucol(8), ucol(9),
                  par((CONF_KERNEL, 8, GROUP_W)), par((1, GROUP_W)), par((1, GROUP_W)), par((1, GROUP_W)),
                  par((SCONV_KERNEL, 8, GROUP_W)), par((LRU_CONV, 8, GROUP_W)), par((1, GROUP_W)),
                  par((GROUP_W, 2 * GROUP_W)), par((1, 2 * GROUP_W)), par((1, GROUP_W)),
                  pl.BlockSpec((GROUP_W, GROUP_W), lambda b, s: (0, 0))],
        out_specs=[out_spec, out_spec, out_spec],
        out_shape=[out_sds, out_sds, out_sds],
        scratch_shapes=(
            [pltpu.VMEM((2 * UNIT, GROUP_W), F32)] * 3
            + [pltpu.VMEM((UNIT, GROUP_W), F32)] * 5
            + [pltpu.VMEM((8, GROUP_W), F32)] * 2
        ),
        compiler_params=pltpu.CompilerParams(
            dimension_semantics=("parallel", "arbitrary"), vmem_limit_bytes=VMEM_LIMIT),
        name="mixers",
    )(u3, u3, u3, u3, u3, u3, u3,
      p["dww"], p["dwb"], p["gng"], p["gnb"], p["scw"], p["lcw"], p["lcb"],
      p["wax"], p["bax"], p["lam"], p["gm"])


_NT = (((1,), (1,)), ((), ()))


def _attn_kernel(q_ref, k_ref, v_ref, wo_ref, w1_ref, w2_ref, o_ref, wo_b_ref, w1_b_ref, w2_b_ref,
                 kb, vt, kmm, gate_s, cmask, pos_q, rbias, qm, s_buf, acc):
    wo_b_ref[...] = wo_ref[...].astype(BF16)
    w1_b_ref[...] = w1_ref[...].astype(BF16)
    w2_b_ref[...] = w2_ref[...].astype(BF16)
    nb = kb.shape[0]
    blk = MOBA_BLOCK
    heads = range(ATT_HEADS)
    log2e = math.log2(math.e)
    slopes2 = [log2e * 2.0 ** (-8.0 * (h + 1) / ATT_HEADS) for h in heads]
    lane_head = lax.broadcasted_iota(jnp.int32, (1, GROUP_W), 1) // HEAD_DIM
    pos = lambda rho: SEG * (rho & 7) + (rho >> 3)

    pos_s = pos(lax.broadcasted_iota(jnp.int32, (1, blk), 1))
    rel = (pos_s - (blk - 1)).astype(F32)
    first_row = lax.broadcasted_iota(jnp.int32, (V_ROWS - HEAD_DIM, blk), 0) == 0
    for n in range(nb):
        kblk = k_ref[n * blk:(n + 1) * blk, :]
        kb[n] = kblk.astype(BF16)
        v_t = v_ref[n * blk:(n + 1) * blk, :].T
        kmean = jnp.mean(kblk, axis=0, keepdims=True)
        for h in heads:
            kmm[h * nb + n:h * nb + n + 1, :] = jnp.where(lane_head == h, kmean, 0.0)
            f = jnp.exp2(slopes2[h] * rel)
            vt[n, h * V_ROWS:h * V_ROWS + HEAD_DIM, :] = (v_t[h * HEAD_DIM:(h + 1) * HEAD_DIM, :] * f).astype(BF16)
            vt[n, h * V_ROWS + HEAD_DIM:(h + 1) * V_ROWS, :] = jnp.where(first_row, f, 0.0).astype(BF16)
    t_l = pos(lax.broadcasted_iota(jnp.int32, (blk, blk), 1))
    s_l = pos(lax.broadcasted_iota(jnp.int32, (blk, blk), 0))
    cmask[...] = jnp.where(t_l >= s_l, 0.0, NEG_INF)
    pos_q[...] = pos(lax.broadcasted_iota(jnp.int32, (nb, blk), 1)).astype(F32)
    blk_id = lax.broadcasted_iota(jnp.int32, (nb, blk), 0)

    def prepare(i):
        par = i % 2
        q = q_ref[i * blk:(i + 1) * blk, :] * (HEAD_DIM ** -0.5 * log2e)
        for h in heads:
            qm[par, h] = jnp.where(lane_head == h, q, 0.0).astype(BF16)
        gate_s[...] = lax.dot_general(kmm[...], q, _NT, precision=lax.Precision.HIGHEST,
                                      preferred_element_type=F32)
        dist = (i - blk_id).astype(F32) * blk + (pos_q[...] - (blk - 1))
        for h in heads:
            g = gate_s[h * nb:(h + 1) * nb, :]
            rank = jnp.zeros((nb, blk), jnp.int32)
            for m in range(i):
                gm = gate_s[h * nb + m:h * nb + m + 1, :]
                rank = rank + jnp.where((gm > g) | ((gm == g) & (m < blk_id)), 1, 0)
            visible = (blk_id == i) | ((blk_id < i) & (rank < MOBA_TOP_K))
            rbias[par, h] = jnp.where(visible, -slopes2[h] * dist, NEG_INF)

    def score(k, i, n, m_run):
        out = []
        for h in heads:
            s_t = lax.dot_general(kb[n], qm[i % 2, h], _NT, preferred_element_type=F32)
            if n == i:
                s_t = s_t + cmask[...]
            s_buf[k % 2, h] = s_t
            m_blk = jnp.max(s_t, axis=0, keepdims=True) + rbias[i % 2, h, n:n + 1, :]
            out.append(m_blk if n == i else jnp.maximum(m_run[h], m_blk))
        return out

    def fold(k, i, n, m_now, m_acc):
        for h in heads:
            p = jnp.exp2(s_buf[k % 2, h] - (m_now[h] - rbias[i % 2, h, n:n + 1, :]))
            pv = _dot(vt[n, h * V_ROWS:(h + 1) * V_ROWS, :], p.astype(BF16))
            acc[i % 2, h] = pv if n == i else jnp.exp2(m_acc[h] - m_now[h]) * acc[i % 2, h] + pv

    def finalize(i):
        out_t = jnp.concatenate(
            [acc[i % 2, h, 0:HEAD_DIM, :] / acc[i % 2, h, HEAD_DIM:HEAD_DIM + 1, :] for h in heads], axis=0)
        o_ref[i * blk:(i + 1) * blk, :] = out_t.T.astype(o_ref.dtype)

    units = [(i, n) for i in range(nb) for n in [i] + list(range(i))]
    prepare(0)
    m_run = score(0, 0, 0, None)
    prepare(1)
    m_fold = None
    for k in range(1, len(units) + 1):
        pi, pn = units[k - 1]
        m_prev = m_run
        if k < len(units):
            i, n = units[k]
            m_run = score(k, i, n, m_run)
        fold(k - 1, pi, pn, m_prev, m_fold)
        m_fold = m_prev
        if pn == pi - 1 or pi == 0:
            finalize(pi)
        if k < len(units) and n == i and i + 1 < nb:
            prepare(i + 1)


def _attention(u3, w_out, w1, w2, layer):
    bsz, seq, _ = u3.shape
    nb = seq // MOBA_BLOCK
    blk = MOBA_BLOCK

    def ucol(j):
        return pl.BlockSpec((None, seq, GROUP_W), lambda b: (b, 0, j))

    def slab_in(w):
        return pl.BlockSpec((None, w.shape[1] // bsz, w.shape[2]), lambda b: (layer, b, 0))

    def slab_out(w):
        return pl.BlockSpec((w.shape[1] // bsz, w.shape[2]), lambda b: (b, 0))

    weights = (w_out, w1, w2)
    return pl.pallas_call(
        _attn_kernel,
        grid=(bsz,),
        in_specs=[ucol(5), ucol(6), ucol(7)] + [slab_in(w) for w in weights],
        out_specs=[pl.BlockSpec((None, seq, GROUP_W), lambda b: (b, 0, 0))] + [slab_out(w) for w in weights],
        out_shape=[jax.ShapeDtypeStruct((bsz, seq, GROUP_W), BF16)]
        + [jax.ShapeDtypeStruct(w.shape[1:], BF16) for w in weights],
        scratch_shapes=[
            pltpu.VMEM((nb, blk, GROUP_W), BF16),
            pltpu.VMEM((nb, ATT_HEADS * V_ROWS, blk), BF16),
            pltpu.VMEM((ATT_HEADS * nb, GROUP_W), F32),
            pltpu.VMEM((ATT_HEADS * nb, blk), F32),
            pltpu.VMEM((blk, blk), F32),
            pltpu.VMEM((nb, blk), F32),
            pltpu.VMEM((2, ATT_HEADS, nb, blk), F32),
            pltpu.VMEM((2, ATT_HEADS, blk, GROUP_W), BF16),
            pltpu.VMEM((2, ATT_HEADS, blk, blk), F32),
            pltpu.VMEM((2, ATT_HEADS, V_ROWS, blk), F32),
        ],
        compiler_params=pltpu.CompilerParams(
            dimension_semantics=("parallel",), vmem_limit_bytes=VMEM_LIMIT),
        name="moba_attention",
    )(u3, u3, u3, w_out, w1, w2)


def _out_mlp_kernel(ya_ref, yb_ref, yc_ref, yd_ref, x_ref, wo_ref, gpost_ref, gpre_ref, w1_ref, w2_ref, gmlp_ref,
                    xo_ref):
    y = None
    for j, y_ref in enumerate((ya_ref, yb_ref, yc_ref, yd_ref)):
        term = _dot(y_ref[...], wo_ref[j * GROUP_W:(j + 1) * GROUP_W, :])
        y = term if y is None else y + term
    x = x_ref[...] + _rms(y, gpost_ref[...])
    h = _rms(x, gpre_ref[...]).astype(BF16)
    m = None
    for c in range(0, D_FF, FF_CHUNK):
        t = jnp.maximum(_dot(h, w1_ref[:, c:c + FF_CHUNK]), 0.0)
        term = _dot((t * t).astype(BF16), w2_ref[c:c + FF_CHUNK, :])
        m = term if m is None else m + term
    xo_ref[...] = x + _rms(m, gmlp_ref[...])


def _out_mlp(ya, yb, yc, yd, x2, wo, gpost, gpre, w1, w2, gmlp, layer):
    tok = x2.shape[0]
    tm = TOKEN_TILE
    yspec = pl.BlockSpec((tm, GROUP_W), lambda t: (t, 0))
    xspec = pl.BlockSpec((tm, D_MODEL), lambda t: (t, 0))
    gspec = pl.BlockSpec((None, 1, D_MODEL), lambda t: (layer, 0, 0))

    def resident(shape):
        return pl.BlockSpec(shape, lambda t: (0, 0), pipeline_mode=pl.Buffered(1))

    return pl.pallas_call(
        _out_mlp_kernel,
        grid=(tok // tm,),
        in_specs=[yspec, yspec, yspec, yspec, xspec, resident((D_MODEL, D_MODEL)), gspec, gspec,
                  resident((D_MODEL, D_FF)), resident((D_FF, D_MODEL)), gspec],
        out_specs=xspec,
        out_shape=jax.ShapeDtypeStruct((tok, D_MODEL), F32),
        compiler_params=pltpu.CompilerParams(
            dimension_semantics=("parallel",), vmem_limit_bytes=VMEM_LIMIT),
        name="out_mlp",
    )(ya, yb, yc, yd, x2, wo, gpost, gpre, w1, w2, gmlp)


def _block_diag_dense(w):
    depth, nblk, n, _ = w.shape
    eye = jnp.eye(nblk, dtype=w.dtype)
    return jnp.einsum("dgij,gh->dgihj", w, eye).reshape(depth, nblk * n, nblk * n)


def _to_segment_order(x):
    bsz, seq, dm = x.shape
    return x.reshape(bsz, seq // UNIT, 8, SEG, dm).swapaxes(2, 3).reshape(bsz, seq, dm)


def _from_segment_order(x):
    bsz, seq, dm = x.shape
    return x.reshape(bsz, seq // UNIT, SEG, 8, dm).swapaxes(2, 3).reshape(bsz, seq, dm)


def kernel(x, pre_mix_g, w_in, conf_dw_w, conf_dw_b, conf_gn_g, conf_gn_b, sconv_w, lru_conv_w, lru_conv_b,
           lru_wa, lru_ba, lru_wx, lru_bx, lru_lam, w_out, post_mix_g, pre_mlp_g, mlp_w1, mlp_w2, post_mlp_g):
    bsz, seq, dm = x.shape
    depth = w_in.shape[0]
    assert dm == D_MODEL and seq % SEQ_TILE == 0 and seq % MOBA_BLOCK == 0 and (bsz * seq) % TOKEN_TILE == 0
    assert seq // MOBA_BLOCK > MOBA_TOP_K
    assert D_MODEL % (16 * bsz) == 0

    row = lambda a: a.reshape(depth, 1, a.shape[-1])
    taps = lambda w: jnp.broadcast_to(w[:, :, None, :], w.shape[:2] + (8, w.shape[-1]))
    group_of = jnp.arange(GROUP_W) // (GROUP_W // CONF_GROUPS)
    params = {
        "dww": taps(conf_dw_w), "dwb": row(conf_dw_b), "gng": row(conf_gn_g), "gnb": row(conf_gn_b),
        "scw": taps(sconv_w), "lcw": taps(lru_conv_w), "lcb": row(lru_conv_b),
        "wax": jnp.concatenate([_block_diag_dense(lru_wa), _block_diag_dense(lru_wx)], axis=-1).astype(BF16),
        "bax": row(jnp.concatenate([lru_ba, lru_bx], axis=-1)),
        "lam": row(lru_lam),
        "gm": ((group_of[:, None] == group_of[None, :]).astype(F32) / (GROUP_W // CONF_GROUPS)).astype(BF16),
    }
    w_in_b = w_in.astype(BF16)
    g_pre_mix, g_post_mix = row(pre_mix_g), row(post_mix_g)
    g_pre_mlp, g_post_mlp = row(pre_mlp_g), row(post_mlp_g)

    x2 = _to_segment_order(x).reshape(bsz * seq, dm)
    for layer in range(depth):
        u = _in_proj(x2, g_pre_mix, w_in_b, layer)
        u3 = u.reshape(bsz, seq, IN_COLS)
        ya, yb, yd = _mixers(u3, params, layer)
        yc, w_out_b, w1_b, w2_b = _attention(u3, w_out, mlp_w1, mlp_w2, layer)
        flat = lambda a: a.reshape(bsz * seq, GROUP_W)
        x2 = _out_mlp(flat(ya), flat(yb), flat(yc), flat(yd), x2, w_out_b, g_post_mix, g_pre_mlp,
                      w1_b, w2_b, g_post_mlp, layer)
    return _from_segment_order(x2.reshape(bsz, seq, dm))
```

```python
import math

import jax
import jax.numpy as jnp
from jax import lax
from jax.experimental import pallas as pl
from jax.experimental.pallas import tpu as pltpu

F32 = jnp.float32
BF16 = jnp.bfloat16

D_MODEL = 1024
GROUP_W = 256
CONF_KERNEL = 31
CONF_GROUPS = 4
GN_EPS = 1e-5
SCONV_KERNEL = 3
ATT_HEADS = 4
HEAD_DIM = GROUP_W // ATT_HEADS
MOBA_BLOCK = 256
MOBA_TOP_K = 3
LRU_BLOCKS = 4
LRU_CONV = 4
LRU_C = 8.0
D_FF = 4 * D_MODEL
RMS_EPS = 1e-6
IN_COLS = 10 * GROUP_W
ATT_COL0, ATT_COL1 = 5 * GROUP_W, 8 * GROUP_W
MIX_COLS = IN_COLS - (ATT_COL1 - ATT_COL0)

TOKEN_TILE = 1024
UNIT = MOBA_BLOCK
SEG = UNIT // 8
FF_CHUNK = 1024
V_ROWS = HEAD_DIM + 16
VMEM_LIMIT = 56 * 1024 * 1024

NEG_INF = float("-inf")


def _rms(x, g):
    return x * lax.rsqrt(jnp.mean(x * x, axis=-1, keepdims=True) + RMS_EPS) * g


def _dot(a, b):
    return jnp.dot(a, b, preferred_element_type=F32)


def _in_proj_kernel(x_ref, g_ref, w_ref, u_ref):
    u_ref[...] = _dot(_rms(x_ref[...], g_ref[...]).astype(BF16), w_ref[...])


def _in_proj(x2, g, w, layer):
    tok = x2.shape[0]
    cols = w.shape[-1]
    return pl.pallas_call(
        _in_proj_kernel,
        grid=(tok // TOKEN_TILE,),
        in_specs=[
            pl.BlockSpec((TOKEN_TILE, D_MODEL), lambda t: (t, 0)),
            pl.BlockSpec((None, 1, D_MODEL), lambda t: (layer, 0, 0)),
            pl.BlockSpec((None, D_MODEL, cols), lambda t: (layer, 0, 0)),
        ],
        out_specs=pl.BlockSpec((TOKEN_TILE, cols), lambda t: (t, 0)),
        out_shape=jax.ShapeDtypeStruct((tok, cols), F32),
        compiler_params=pltpu.CompilerParams(
            dimension_semantics=("parallel",), vmem_limit_bytes=VMEM_LIMIT),
        name="in_proj",
    )(x2, g, w)


def _group_mean(v, gm):
    hi = v.astype(BF16)
    lo = (v - hi.astype(F32)).astype(BF16)
    return _dot(hi, gm) + _dot(lo, gm)


def _seg_conv(z, e_ref, w_ref, bias, width, out_ref):
    sub = lax.broadcasted_iota(jnp.int32, (8, GROUP_W), 0)
    for a in range(SEG - (width - 1), SEG):
        cur = z[8 * a:8 * a + 8, :]
        prev = e_ref[UNIT + 8 * a:UNIT + 8 * a + 8, :]
        e_ref[8 * a:8 * a + 8, :] = pltpu.roll(jnp.where(sub == 7, prev, cur), 1, axis=0)
    e_ref[UNIT:2 * UNIT, :] = z
    for a0 in range(0, SEG, 8):
        accs = [bias] * 8
        for j in range(width):
            w_j = w_ref[width - 1 - j]
            for a in range(a0, a0 + 8):
                src = UNIT + 8 * (a - j) if a >= j else 8 * (a - j + SEG)
                term = w_j * e_ref[src:src + 8, :]
                accs[a - a0] = term if accs[a - a0] is None else accs[a - a0] + term
        for a in range(a0, a0 + 8):
            out_ref[8 * a:8 * a + 8, :] = accs[a - a0]


def _mixers_kernel(x_ref, gin_ref, win_ref,
                   dww_ref, dwb_ref, gng_ref, gnb_ref, scw_ref, lcw_ref, lcb_ref,
                   wax_ref, bax_ref, lam_ref, gm_ref,
                   ya_ref, yb_ref, yd_ref,
                   u_even, u_odd, ea, eb, ed, cbuf, abuf, bbuf, hloc, acum, endb, hcar):
    n_units = x_ref.shape[0] // UNIT
    for e in (ea, eb, ed):
        e[UNIT:2 * UNIT, :] = jnp.zeros((UNIT, GROUP_W), F32)
    hcar[...] = jnp.zeros_like(hcar)

    sub = lax.broadcasted_iota(jnp.int32, (8, GROUP_W), 0)
    bias_a = jnp.broadcast_to(dwb_ref[...], (8, GROUP_W))
    bias_d = jnp.broadcast_to(lcb_ref[...], (8, GROUP_W))
    lam = lam_ref[...]
    neg_c_softplus = -LRU_C * (jnp.maximum(-lam, 0.0) + jnp.log1p(jnp.exp(-jnp.abs(lam))))
    gm = gm_ref[...]

    def project(u, u_ref):
        rows = pl.ds(pl.multiple_of(u * UNIT, UNIT), UNIT)
        h = _rms(x_ref[rows, :], gin_ref[...]).astype(BF16)
        step = 2 * GROUP_W
        for c in range(0, MIX_COLS, step):
            w = min(step, MIX_COLS - c)
            u_ref[:, c:c + w] = _dot(h, win_ref[:, c:c + w])

    def col(u_ref, j):
        return u_ref[:, j * GROUP_W:(j + 1) * GROUP_W]

    def mix(u, u_ref):
        rows = pl.ds(pl.multiple_of(u * UNIT, UNIT), UNIT)

        _seg_conv(col(u_ref, 0) * jax.nn.sigmoid(col(u_ref, 1)), ea, dww_ref, bias_a, CONF_KERNEL, cbuf)
        c = cbuf[...]
        d = c - _group_mean(c, gm)
        var = _group_mean(d * d, gm)
        yn = d * lax.rsqrt(var + GN_EPS) * gng_ref[...] + gnb_ref[...]
        ya_ref[rows, :] = (yn * jax.nn.sigmoid(yn)).astype(ya_ref.dtype)

        _seg_conv(col(u_ref, 3) * col(u_ref, 4), eb, scw_ref, None, SCONV_KERNEL, cbuf)
        yb_ref[rows, :] = (col(u_ref, 2) * cbuf[...]).astype(yb_ref.dtype)

        _seg_conv(col(u_ref, 5), ed, lcw_ref, bias_d, LRU_CONV, cbuf)
        xc = cbuf[...]
        gates = _dot(xc.astype(BF16), wax_ref[...]) + bax_ref[...]
        r = jax.nn.sigmoid(gates[:, :GROUP_W])
        i = jax.nn.sigmoid(gates[:, GROUP_W:])
        log_a = neg_c_softplus * r
        abuf[...] = jnp.exp(log_a)
        th = jnp.tanh(log_a)
        bbuf[...] = jnp.sqrt(-2.0 * th / (1.0 - th)) * (i * xc)
        h_run = bbuf[0:8, :]
        a_run = abuf[0:8, :]
        hloc[0:8, :] = h_run
        acum[0:8, :] = a_run
        for s in range(1, SEG):
            a_s = abuf[8 * s:8 * s + 8, :]
            h_run = a_s * h_run + bbuf[8 * s:8 * s + 8, :]
            a_run = a_s * a_run
            hloc[8 * s:8 * s + 8, :] = h_run
            acum[8 * s:8 * s + 8, :] = a_run
        dist = 1
        while dist < 8:
            keep = sub >= dist
            h_run = jnp.where(keep, a_run * pltpu.roll(h_run, dist, axis=0) + h_run, h_run)
            a_run = jnp.where(keep, a_run * pltpu.roll(a_run, dist, axis=0), a_run)
            dist *= 2
        h_in = hcar[...]
        seg_end = a_run * h_in + h_run
        endb[...] = seg_end
        seg_in = jnp.where(sub == 0, h_in, pltpu.roll(seg_end, 1, axis=0))
        hcar[...] = jnp.broadcast_to(endb[7:8, :], (8, GROUP_W))
        for s in range(SEG):
            hloc[8 * s:8 * s + 8, :] = hloc[8 * s:8 * s + 8, :] + acum[8 * s:8 * s + 8, :] * seg_in
        g = col(u_ref, 6)
        gelu = 0.5 * g * (1.0 + jnp.tanh(math.sqrt(2.0 / math.pi) * (g + 0.044715 * (g * g * g))))
        yd_ref[rows, :] = (hloc[...] * gelu).astype(yd_ref.dtype)

    project(0, u_even)

    def two_units(jj, carry):
        project(2 * jj + 1, u_odd)
        mix(2 * jj, u_even)
        project(jnp.minimum(2 * jj + 2, n_units - 1), u_even)
        mix(2 * jj + 1, u_odd)
        return carry

    lax.fori_loop(0, n_units // 2, two_units, 0)


def _mixers(x3, gin, w_mix, p, layer):
    bsz, seq, dm = x3.shape
    assert (seq // UNIT) % 2 == 0

    def par(shape):
        return pl.BlockSpec((None,) + shape, lambda b: (layer,) + (0,) * len(shape))

    out_spec = pl.BlockSpec((None, seq, GROUP_W), lambda b: (b, 0, 0))
    out_sds = jax.ShapeDtypeStruct((bsz, seq, GROUP_W), BF16)
    return pl.pallas_call(
        _mixers_kernel,
        grid=(bsz,),
        in_specs=[pl.BlockSpec((None, seq, dm), lambda b: (b, 0, 0)), par((1, dm)),
                  pl.BlockSpec((None, dm, MIX_COLS), lambda b: (layer, 0, 0), pipeline_mode=pl.Buffered(1)),
                  par((CONF_KERNEL, 8, GROUP_W)), par((1, GROUP_W)), par((1, GROUP_W)), par((1, GROUP_W)),
                  par((SCONV_KERNEL, 8, GROUP_W)), par((LRU_CONV, 8, GROUP_W)), par((1, GROUP_W)),
                  par((GROUP_W, 2 * GROUP_W)), par((1, 2 * GROUP_W)), par((1, GROUP_W)),
                  pl.BlockSpec((GROUP_W, GROUP_W), lambda b: (0, 0))],
        out_specs=[out_spec, out_spec, out_spec],
        out_shape=[out_sds, out_sds, out_sds],
        scratch_shapes=(
            [pltpu.VMEM((UNIT, MIX_COLS), F32)] * 2
            + [pltpu.VMEM((2 * UNIT, GROUP_W), F32)] * 3
            + [pltpu.VMEM((UNIT, GROUP_W), F32)] * 5
            + [pltpu.VMEM((8, GROUP_W), F32)] * 2
        ),
        compiler_params=pltpu.CompilerParams(
            dimension_semantics=("parallel",), vmem_limit_bytes=VMEM_LIMIT),
        name="mixers",
    )(x3, gin, w_mix,
      p["dww"], p["dwb"], p["gng"], p["gnb"], p["scw"], p["lcw"], p["lcb"],
      p["wax"], p["bax"], p["lam"], p["gm"])


_NT = (((1,), (1,)), ((), ()))


def _attn_kernel(q_ref, k_ref, v_ref, wo_ref, w1_ref, w2_ref, o_ref, wo_b_ref, w1_b_ref, w2_b_ref,
                 kb, vt, kmm, gate_s, cmask, pos_q, rbias, qm, s_buf, acc):
    wo_b_ref[...] = wo_ref[...].astype(BF16)
    w1_b_ref[...] = w1_ref[...].astype(BF16)
    w2_b_ref[...] = w2_ref[...].astype(BF16)
    nb = kb.shape[0]
    blk = MOBA_BLOCK
    heads = range(ATT_HEADS)
    log2e = math.log2(math.e)
    slopes2 = [log2e * 2.0 ** (-8.0 * (h + 1) / ATT_HEADS) for h in heads]
    lane_head = lax.broadcasted_iota(jnp.int32, (1, GROUP_W), 1) // HEAD_DIM
    pos = lambda rho: SEG * (rho & 7) + (rho >> 3)

    pos_s = pos(lax.broadcasted_iota(jnp.int32, (1, blk), 1))
    rel = (pos_s - (blk - 1)).astype(F32)
    first_row = lax.broadcasted_iota(jnp.int32, (V_ROWS - HEAD_DIM, blk), 0) == 0
    for n in range(nb):
        kblk = k_ref[n * blk:(n + 1) * blk, :]
        kb[n] = kblk.astype(BF16)
        v_t = v_ref[n * blk:(n + 1) * blk, :].T
        kmean = jnp.mean(kblk, axis=0, keepdims=True)
        for h in heads:
            kmm[h * nb + n:h * nb + n + 1, :] = jnp.where(lane_head == h, kmean, 0.0)
            f = jnp.exp2(slopes2[h] * rel)
            vt[n, h * V_ROWS:h * V_ROWS + HEAD_DIM, :] = (v_t[h * HEAD_DIM:(h + 1) * HEAD_DIM, :] * f).astype(BF16)
            vt[n, h * V_ROWS + HEAD_DIM:(h + 1) * V_ROWS, :] = jnp.where(first_row, f, 0.0).astype(BF16)
    t_l = pos(lax.broadcasted_iota(jnp.int32, (blk, blk), 1))
    s_l = pos(lax.broadcasted_iota(jnp.int32, (blk, blk), 0))
    cmask[...] = jnp.where(t_l >= s_l, 0.0, NEG_INF)
    pos_q[...] = pos(lax.broadcasted_iota(jnp.int32, (nb, blk), 1)).astype(F32)
    blk_id = lax.broadcasted_iota(jnp.int32, (nb, blk), 0)

    def prepare(i):
        par = i % 2
        q = q_ref[i * blk:(i + 1) * blk, :] * (HEAD_DIM ** -0.5 * log2e)
        for h in heads:
            qm[par, h] = jnp.where(lane_head == h, q, 0.0).astype(BF16)
        gate_s[...] = lax.dot_general(kmm[...], q, _NT, precision=lax.Precision.HIGHEST,
                                      preferred_element_type=F32)
        dist = (i - blk_id).astype(F32) * blk + (pos_q[...] - (blk - 1))
        for h in heads:
            g = gate_s[h * nb:(h + 1) * nb, :]
            rank = jnp.zeros((nb, blk), jnp.int32)
            for m in range(i):
                gm = gate_s[h * nb + m:h * nb + m + 1, :]
                rank = rank + jnp.where((gm > g) | ((gm == g) & (m < blk_id)), 1, 0)
            visible = (blk_id == i) | ((blk_id < i) & (rank < MOBA_TOP_K))
            rbias[par, h] = jnp.where(visible, -slopes2[h] * dist, NEG_INF)

    def score(k, i, n, m_run):
        out = []
        for h in heads:
            s_t = lax.dot_general(kb[n], qm[i % 2, h], _NT, preferred_element_type=F32)
            if n == i:
                s_t = s_t + cmask[...]
            s_buf[k % 2, h] = s_t
            m_blk = jnp.max(s_t, axis=0, keepdims=True) + rbias[i % 2, h, n:n + 1, :]
            out.append(m_blk if n == i else jnp.maximum(m_run[h], m_blk))
        return out

    def fold(k, i, n, m_now, m_acc):
        for h in heads:
            p = jnp.exp2(s_buf[k % 2, h] - (m_now[h] - rbias[i % 2, h, n:n + 1, :]))
            pv = _dot(vt[n, h * V_ROWS:(h + 1) * V_ROWS, :], p.astype(BF16))
            acc[i % 2, h] = pv if n == i else jnp.exp2(m_acc[h] - m_now[h]) * acc[i % 2, h] + pv

    def finalize(i):
        out_t = jnp.concatenate(
            [acc[i % 2, h, 0:HEAD_DIM, :] / acc[i % 2, h, HEAD_DIM:HEAD_DIM + 1, :] for h in heads], axis=0)
        o_ref[i * blk:(i + 1) * blk, :] = out_t.T.astype(o_ref.dtype)

    units = [(i, n) for i in range(nb) for n in [i] + list(range(i))]
    prepare(0)
    m_run = score(0, 0, 0, None)
    prepare(1)
    m_fold = None
    for k in range(1, len(units) + 1):
        pi, pn = units[k - 1]
        m_prev = m_run
        if k < len(units):
            i, n = units[k]
            m_run = score(k, i, n, m_run)
        fold(k - 1, pi, pn, m_prev, m_fold)
        m_fold = m_prev
        if pn == pi - 1 or pi == 0:
            finalize(pi)
        if k < len(units) and n == i and i + 1 < nb:
            prepare(i + 1)


def _attention(u3, w_out, w1, w2, layer):
    bsz, seq, _ = u3.shape
    nb = seq // MOBA_BLOCK
    blk = MOBA_BLOCK

    def ucol(j):
        return pl.BlockSpec((None, seq, GROUP_W), lambda b: (b, 0, j))

    def slab_in(w):
        return pl.BlockSpec((None, w.shape[1] // bsz, w.shape[2]), lambda b: (layer, b, 0))

    def slab_out(w):
        return pl.BlockSpec((w.shape[1] // bsz, w.shape[2]), lambda b: (b, 0))

    weights = (w_out, w1, w2)
    return pl.pallas_call(
        _attn_kernel,
        grid=(bsz,),
        in_specs=[ucol(0), ucol(1), ucol(2)] + [slab_in(w) for w in weights],
        out_specs=[pl.BlockSpec((None, seq, GROUP_W), lambda b: (b, 0, 0))] + [slab_out(w) for w in weights],
        out_shape=[jax.ShapeDtypeStruct((bsz, seq, GROUP_W), BF16)]
        + [jax.ShapeDtypeStruct(w.shape[1:], BF16) for w in weights],
        scratch_shapes=[
            pltpu.VMEM((nb, blk, GROUP_W), BF16),
            pltpu.VMEM((nb, ATT_HEADS * V_ROWS, blk), BF16),
            pltpu.VMEM((ATT_HEADS * nb, GROUP_W), F32),
            pltpu.VMEM((ATT_HEADS * nb, blk), F32),
            pltpu.VMEM((blk, blk), F32),
            pltpu.VMEM((nb, blk), F32),
            pltpu.VMEM((2, ATT_HEADS, nb, blk), F32),
            pltpu.VMEM((2, ATT_HEADS, blk, GROUP_W), BF16),
            pltpu.VMEM((2, ATT_HEADS, blk, blk), F32),
            pltpu.VMEM((2, ATT_HEADS, V_ROWS, blk), F32),
        ],
        compiler_params=pltpu.CompilerParams(
            dimension_semantics=("parallel",), vmem_limit_bytes=VMEM_LIMIT),
        name="moba_attention",
    )(u3, u3, u3, w_out, w1, w2)


def _out_mlp_kernel(ya_ref, yb_ref, yc_ref, yd_ref, x_ref, wo_ref, gpost_ref, gpre_ref, w1_ref, w2_ref, gmlp_ref,
                    xo_ref):
    y = None
    for j, y_ref in enumerate((ya_ref, yb_ref, yc_ref, yd_ref)):
        term = _dot(y_ref[...], wo_ref[j * GROUP_W:(j + 1) * GROUP_W, :])
        y = term if y is None else y + term
    x = x_ref[...] + _rms(y, gpost_ref[...])
    h = _rms(x, gpre_ref[...]).astype(BF16)
    m = None
    for c in range(0, D_FF, FF_CHUNK):
        t = jnp.maximum(_dot(h, w1_ref[:, c:c + FF_CHUNK]), 0.0)
        term = _dot((t * t).astype(BF16), w2_ref[c:c + FF_CHUNK, :])
        m = term if m is None else m + term
    xo_ref[...] = x + _rms(m, gmlp_ref[...])


def _out_mlp(ya, yb, yc, yd, x2, wo, gpost, gpre, w1, w2, gmlp, layer):
    tok = x2.shape[0]
    tm = TOKEN_TILE
    yspec = pl.BlockSpec((tm, GROUP_W), lambda t: (t, 0))
    xspec = pl.BlockSpec((tm, D_MODEL), lambda t: (t, 0))
    gspec = pl.BlockSpec((None, 1, D_MODEL), lambda t: (layer, 0, 0))

    def resident(shape):
        return pl.BlockSpec(shape, lambda t: (0, 0), pipeline_mode=pl.Buffered(1))

    return pl.pallas_call(
        _out_mlp_kernel,
        grid=(tok // tm,),
        in_specs=[yspec, yspec, yspec, yspec, xspec, resident((D_MODEL, D_MODEL)), gspec, gspec,
                  resident((D_MODEL, D_FF)), resident((D_FF, D_MODEL)), gspec],
        out_specs=xspec,
        out_shape=jax.ShapeDtypeStruct((tok, D_MODEL), F32),
        compiler_params=pltpu.CompilerParams(
            dimension_semantics=("parallel",), vmem_limit_bytes=VMEM_LIMIT),
        name="out_mlp",
    )(ya, yb, yc, yd, x2, wo, gpost, gpre, w1, w2, gmlp)


def _block_diag_dense(w):
    depth, nblk, n, _ = w.shape
    eye = jnp.eye(nblk, dtype=w.dtype)
    return jnp.einsum("dgij,gh->dgihj", w, eye).reshape(depth, nblk * n, nblk * n)


def _to_segment_order(x):
    bsz, seq, dm = x.shape
    return x.reshape(bsz, seq // UNIT, 8, SEG, dm).swapaxes(2, 3).reshape(bsz, seq, dm)


def _from_segment_order(x):
    bsz, seq, dm = x.shape
    return x.reshape(bsz, seq // UNIT, SEG, 8, dm).swapaxes(2, 3).reshape(bsz, seq, dm)


def kernel(x, pre_mix_g, w_in, conf_dw_w, conf_dw_b, conf_gn_g, conf_gn_b, sconv_w, lru_conv_w, lru_conv_b,
           lru_wa, lru_ba, lru_wx, lru_bx, lru_lam, w_out, post_mix_g, pre_mlp_g, mlp_w1, mlp_w2, post_mlp_g):
    bsz, seq, dm = x.shape
    depth = w_in.shape[0]
    assert dm == D_MODEL and seq % MOBA_BLOCK == 0 and (bsz * seq) % TOKEN_TILE == 0
    assert seq // MOBA_BLOCK > MOBA_TOP_K
    assert D_MODEL % (16 * bsz) == 0

    row = lambda a: a.reshape(depth, 1, a.shape[-1])
    taps = lambda w: jnp.broadcast_to(w[:, :, None, :], w.shape[:2] + (8, w.shape[-1]))
    group_of = jnp.arange(GROUP_W) // (GROUP_W // CONF_GROUPS)
    params = {
        "dww": taps(conf_dw_w), "dwb": row(conf_dw_b), "gng": row(conf_gn_g), "gnb": row(conf_gn_b),
        "scw": taps(sconv_w), "lcw": taps(lru_conv_w), "lcb": row(lru_conv_b),
        "wax": jnp.concatenate([_block_diag_dense(lru_wa), _block_diag_dense(lru_wx)], axis=-1).astype(BF16),
        "bax": row(jnp.concatenate([lru_ba, lru_bx], axis=-1)),
        "lam": row(lru_lam),
        "gm": ((group_of[:, None] == group_of[None, :]).astype(F32) / (GROUP_W // CONF_GROUPS)).astype(BF16),
    }
    w_att = w_in[:, :, ATT_COL0:ATT_COL1].astype(BF16)
    w_mix = jnp.concatenate([w_in[:, :, :ATT_COL0], w_in[:, :, ATT_COL1:]], axis=-1).astype(BF16)
    g_pre_mix, g_post_mix = row(pre_mix_g), row(post_mix_g)
    g_pre_mlp, g_post_mlp = row(pre_mlp_g), row(post_mlp_g)

    x2 = _to_segment_order(x).reshape(bsz * seq, dm)
    for layer in range(depth):
        ya, yb, yd = _mixers(x2.reshape(bsz, seq, dm), g_pre_mix, w_mix, params, layer)
        u_att = _in_proj(x2, g_pre_mix, w_att, layer).reshape(bsz, seq, ATT_COL1 - ATT_COL0)
        yc, w_out_b, w1_b, w2_b = _attention(u_att, w_out, mlp_w1, mlp_w2, layer)
        flat = lambda a: a.reshape(bsz * seq, GROUP_W)
        x2 = _out_mlp(flat(ya), flat(yb), flat(yc), flat(yd), x2, w_out_b, g_post_mix, g_pre_mlp,
                      w1_b, w2_b, g_post_mlp, layer)
    return _from_segment_order(x2.reshape(bsz, seq, dm))
```

```python
import math

import jax
import jax.numpy as jnp
from jax import lax
from jax.experimental import pallas as pl
from jax.experimental.pallas import tpu as pltpu

F32 = jnp.float32
BF16 = jnp.bfloat16

D_MODEL = 1024
GROUP_W = 256
CONF_KERNEL = 31
CONF_GROUPS = 4
GN_EPS = 1e-5
SCONV_KERNEL = 3
ATT_HEADS = 4
HEAD_DIM = GROUP_W // ATT_HEADS
MOBA_BLOCK = 256
MOBA_TOP_K = 3
LRU_BLOCKS = 4
LRU_CONV = 4
LRU_C = 8.0
D_FF = 4 * D_MODEL
RMS_EPS = 1e-6
IN_COLS = 10 * GROUP_W
ATT_COL0, ATT_COL1 = 5 * GROUP_W, 8 * GROUP_W
MIX_COLS = IN_COLS - (ATT_COL1 - ATT_COL0)

TOKEN_TILE = 1024
UNIT = MOBA_BLOCK
SEG = UNIT // 8
FF_CHUNK = 1024
V_ROWS = HEAD_DIM + 16
VMEM_LIMIT = 56 * 1024 * 1024

NEG_INF = float("-inf")


def _rms(x, g):
    return x * lax.rsqrt(jnp.mean(x * x, axis=-1, keepdims=True) + RMS_EPS) * g


def _dot(a, b):
    return jnp.dot(a, b, preferred_element_type=F32)


def _segment_permutations():
    row = lax.broadcasted_iota(jnp.int32, (UNIT, UNIT), 0)
    col = lax.broadcasted_iota(jnp.int32, (UNIT, UNIT), 1)
    to_seg = jnp.where(col == SEG * (row & 7) + (row >> 3), 1.0, 0.0).astype(BF16)
    from_seg = jnp.where(row == SEG * (col & 7) + (col >> 3), 1.0, 0.0).astype(BF16)
    return to_seg, from_seg


def _in_proj_kernel(x_ref, g_ref, wq_ref, wk_ref, wv_ref, u_ref, hseg_ref):
    h = _rms(x_ref[...], g_ref[...]).astype(BF16)
    for j, w_ref in enumerate((wq_ref, wk_ref, wv_ref)):
        u_ref[:, j * GROUP_W:(j + 1) * GROUP_W] = _dot(h, w_ref[...])
    to_seg, _ = _segment_permutations()
    for r in range(0, TOKEN_TILE, UNIT):
        hseg_ref[r:r + UNIT, :] = _dot(to_seg, h[r:r + UNIT, :]).astype(BF16)


def _in_proj(x2, g, w, layer):
    tok = x2.shape[0]
    cols = ATT_COL1 - ATT_COL0

    def wcol(j):
        return pl.BlockSpec((None, D_MODEL, GROUP_W), lambda t: (layer, 0, ATT_COL0 // GROUP_W + j))

    xspec = pl.BlockSpec((TOKEN_TILE, D_MODEL), lambda t: (t, 0))
    return pl.pallas_call(
        _in_proj_kernel,
        grid=(tok // TOKEN_TILE,),
        in_specs=[xspec, pl.BlockSpec((None, 1, D_MODEL), lambda t: (layer, 0, 0)), wcol(0), wcol(1), wcol(2)],
        out_specs=[pl.BlockSpec((TOKEN_TILE, cols), lambda t: (t, 0)), xspec],
        out_shape=[jax.ShapeDtypeStruct((tok, cols), F32), jax.ShapeDtypeStruct((tok, D_MODEL), BF16)],
        compiler_params=pltpu.CompilerParams(
            dimension_semantics=("parallel",), vmem_limit_bytes=VMEM_LIMIT),
        name="in_proj",
    )(x2, g, w, w, w)


def _group_mean(v, gm):
    hi = v.astype(BF16)
    lo = (v - hi.astype(F32)).astype(BF16)
    return _dot(hi, gm) + _dot(lo, gm)


def _seg_conv(z, e_ref, w_ref, bias, width, out_ref):
    sub = lax.broadcasted_iota(jnp.int32, (8, GROUP_W), 0)
    for a in range(SEG - (width - 1), SEG):
        cur = z[8 * a:8 * a + 8, :]
        prev = e_ref[UNIT + 8 * a:UNIT + 8 * a + 8, :]
        e_ref[8 * a:8 * a + 8, :] = pltpu.roll(jnp.where(sub == 7, prev, cur), 1, axis=0)
    e_ref[UNIT:2 * UNIT, :] = z
    for a0 in range(0, SEG, 8):
        accs = [bias] * 8
        for j in range(width):
            w_j = w_ref[width - 1 - j]
            for a in range(a0, a0 + 8):
                src = UNIT + 8 * (a - j) if a >= j else 8 * (a - j + SEG)
                term = w_j * e_ref[src:src + 8, :]
                accs[a - a0] = term if accs[a - a0] is None else accs[a - a0] + term
        for a in range(a0, a0 + 8):
            out_ref[8 * a:8 * a + 8, :] = accs[a - a0]


def _mixers_kernel(h_ref, win_lo_ref, win_hi_ref,
                   dww_ref, dwb_ref, gng_ref, gnb_ref, scw_ref, lcw_ref, lcb_ref,
                   wax_ref, bax_ref, lam_ref, gm_ref,
                   ya_ref, yb_ref, yd_ref,
                   u_even, u_odd, ea, eb, ed, cbuf, abuf, bbuf, hloc, acum, endb, hcar):
    _, from_seg = _segment_permutations()
    n_units = h_ref.shape[0] // UNIT
    for e in (ea, eb, ed):
        e[UNIT:2 * UNIT, :] = jnp.zeros((UNIT, GROUP_W), F32)
    hcar[...] = jnp.zeros_like(hcar)

    sub = lax.broadcasted_iota(jnp.int32, (8, GROUP_W), 0)
    bias_a = jnp.broadcast_to(dwb_ref[...], (8, GROUP_W))
    bias_d = jnp.broadcast_to(lcb_ref[...], (8, GROUP_W))
    lam = lam_ref[...]
    neg_c_softplus = -LRU_C * (jnp.maximum(-lam, 0.0) + jnp.log1p(jnp.exp(-jnp.abs(lam))))
    gm = gm_ref[...]

    def projection(u, u_ref):
        rows = pl.ds(pl.multiple_of(u * UNIT, UNIT), UNIT)
        step = 2 * GROUP_W

        def lo(c):
            def run():
                w = min(step, ATT_COL0 - c)
                u_ref[:, c:c + w] = _dot(h_ref[rows, :], win_lo_ref[:, c:c + w])
            return run

        def hi():
            u_ref[:, ATT_COL0:MIX_COLS] = _dot(h_ref[rows, :], win_hi_ref[...])

        return [lo(c) for c in range(0, ATT_COL0, step)] + [hi]

    def col(u_ref, j):
        return u_ref[:, j * GROUP_W:(j + 1) * GROUP_W]

    def mix(u, u_ref, ahead):
        rows = pl.ds(pl.multiple_of(u * UNIT, UNIT), UNIT)
        ahead = list(ahead)
        ahead.pop(0)()

        _seg_conv(col(u_ref, 0) * jax.nn.sigmoid(col(u_ref, 1)), ea, dww_ref, bias_a, CONF_KERNEL, cbuf)
        c = cbuf[...]
        d = c - _group_mean(c, gm)
        var = _group_mean(d * d, gm)
        ahead.pop(0)()
        yn = d * lax.rsqrt(var + GN_EPS) * gng_ref[...] + gnb_ref[...]
        ya_seg = (yn * jax.nn.sigmoid(yn)).astype(BF16)

        _seg_conv(col(u_ref, 3) * col(u_ref, 4), eb, scw_ref, None, SCONV_KERNEL, cbuf)
        yb_seg = (col(u_ref, 2) * cbuf[...]).astype(BF16)

        _seg_conv(col(u_ref, 5), ed, lcw_ref, bias_d, LRU_CONV, cbuf)
        xc = cbuf[...]
        gates = _dot(xc.astype(BF16), wax_ref[...]) + bax_ref[...]
        while ahead:
            ahead.pop(0)()
        r = jax.nn.sigmoid(gates[:, :GROUP_W])
        i = jax.nn.sigmoid(gates[:, GROUP_W:])
        log_a = neg_c_softplus * r
        abuf[...] = jnp.exp(log_a)
        th = jnp.tanh(log_a)
        bbuf[...] = jnp.sqrt(-2.0 * th / (1.0 - th)) * (i * xc)
        h_run = bbuf[0:8, :]
        a_run = abuf[0:8, :]
        hloc[0:8, :] = h_run
        acum[0:8, :] = a_run
        for s in range(1, SEG):
            a_s = abuf[8 * s:8 * s + 8, :]
            h_run = a_s * h_run + bbuf[8 * s:8 * s + 8, :]
            a_run = a_s * a_run
            hloc[8 * s:8 * s + 8, :] = h_run
            acum[8 * s:8 * s + 8, :] = a_run
        dist = 1
        while dist < 8:
            keep = sub >= dist
            h_run = jnp.where(keep, a_run * pltpu.roll(h_run, dist, axis=0) + h_run, h_run)
            a_run = jnp.where(keep, a_run * pltpu.roll(a_run, dist, axis=0), a_run)
            dist *= 2
        h_in = hcar[...]
        seg_end = a_run * h_in + h_run
        endb[...] = seg_end
        seg_in = jnp.where(sub == 0, h_in, pltpu.roll(seg_end, 1, axis=0))
        hcar[...] = jnp.broadcast_to(endb[7:8, :], (8, GROUP_W))
        for s in range(SEG):
            hloc[8 * s:8 * s + 8, :] = hloc[8 * s:8 * s + 8, :] + acum[8 * s:8 * s + 8, :] * seg_in
        g = col(u_ref, 6)
        gelu = 0.5 * g * (1.0 + jnp.tanh(math.sqrt(2.0 / math.pi) * (g + 0.044715 * (g * g * g))))
        yd_seg = (hloc[...] * gelu).astype(BF16)
        for y_ref, y_seg in ((ya_ref, ya_seg), (yb_ref, yb_seg), (yd_ref, yd_seg)):
            y_ref[rows, :] = _dot(from_seg, y_seg).astype(y_ref.dtype)

    for run in projection(0, u_even):
        run()

    def two_units(jj, carry):
        mix(2 * jj, u_even, projection(2 * jj + 1, u_odd))
        mix(2 * jj + 1, u_odd, projection(jnp.minimum(2 * jj + 2, n_units - 1), u_even))
        return carry

    lax.fori_loop(0, n_units // 2, two_units, 0)


def _mixers(h3, w_in, p, layer):
    bsz, seq, dm = h3.shape
    assert (seq // UNIT) % 2 == 0
    hi_cols = IN_COLS - ATT_COL1
    assert ATT_COL1 % hi_cols == 0

    def par(shape):
        return pl.BlockSpec((None,) + shape, lambda b: (layer,) + (0,) * len(shape))

    out_spec = pl.BlockSpec((None, seq, GROUP_W), lambda b: (b, 0, 0))
    out_sds = jax.ShapeDtypeStruct((bsz, seq, GROUP_W), BF16)
    return pl.pallas_call(
        _mixers_kernel,
        grid=(bsz,),
        in_specs=[pl.BlockSpec((None, seq, dm), lambda b: (b, 0, 0)),
                  pl.BlockSpec((None, dm, ATT_COL0), lambda b: (layer, 0, 0), pipeline_mode=pl.Buffered(1)),
                  pl.BlockSpec((None, dm, hi_cols), lambda b: (layer, 0, ATT_COL1 // hi_cols),
                               pipeline_mode=pl.Buffered(1)),
                  par((CONF_KERNEL, 8, GROUP_W)), par((1, GROUP_W)), par((1, GROUP_W)), par((1, GROUP_W)),
                  par((SCONV_KERNEL, 8, GROUP_W)), par((LRU_CONV, 8, GROUP_W)), par((1, GROUP_W)),
                  par((GROUP_W, 2 * GROUP_W)), par((1, 2 * GROUP_W)), par((1, GROUP_W)),
                  pl.BlockSpec((GROUP_W, GROUP_W), lambda b: (0, 0))],
        out_specs=[out_spec, out_spec, out_spec],
        out_shape=[out_sds, out_sds, out_sds],
        scratch_shapes=(
            [pltpu.VMEM((UNIT, MIX_COLS), F32)] * 2
            + [pltpu.VMEM((2 * UNIT, GROUP_W), F32)] * 3
            + [pltpu.VMEM((UNIT, GROUP_W), F32)] * 5
            + [pltpu.VMEM((8, GROUP_W), F32)] * 2
        ),
        compiler_params=pltpu.CompilerParams(
            dimension_semantics=("parallel",), vmem_limit_bytes=VMEM_LIMIT),
        name="mixers",
    )(h3, w_in, w_in,
      p["dww"], p["dwb"], p["gng"], p["gnb"], p["scw"], p["lcw"], p["lcb"],
      p["wax"], p["bax"], p["lam"], p["gm"])


_NT = (((1,), (1,)), ((), ()))


def _attn_kernel(q_ref, k_ref, v_ref, wo_ref, w1_ref, w2_ref, o_ref, wo_b_ref, w1_b_ref, w2_b_ref,
                 kb, vt, kmm, gate_s, cmask, pos_q, rbias, qm, s_buf, acc):
    wo_b_ref[...] = wo_ref[...].astype(BF16)
    w1_b_ref[...] = w1_ref[...].astype(BF16)
    w2_b_ref[...] = w2_ref[...].astype(BF16)
    nb = kb.shape[0]
    blk = MOBA_BLOCK
    heads = range(ATT_HEADS)
    log2e = math.log2(math.e)
    slopes2 = [log2e * 2.0 ** (-8.0 * (h + 1) / ATT_HEADS) for h in heads]
    lane_head = lax.broadcasted_iota(jnp.int32, (1, GROUP_W), 1) // HEAD_DIM
    pos = lambda rho: rho

    pos_s = pos(lax.broadcasted_iota(jnp.int32, (1, blk), 1))
    rel = (pos_s - (blk - 1)).astype(F32)
    first_row = lax.broadcasted_iota(jnp.int32, (V_ROWS - HEAD_DIM, blk), 0) == 0
    for n in range(nb):
        kblk = k_ref[n * blk:(n + 1) * blk, :]
        kb[n] = kblk.astype(BF16)
        v_t = v_ref[n * blk:(n + 1) * blk, :].T
        kmean = jnp.mean(kblk, axis=0, keepdims=True)
        for h in heads:
            kmm[h * nb + n:h * nb + n + 1, :] = jnp.where(lane_head == h, kmean, 0.0)
            f = jnp.exp2(slopes2[h] * rel)
            vt[n, h * V_ROWS:h * V_ROWS + HEAD_DIM, :] = (v_t[h * HEAD_DIM:(h + 1) * HEAD_DIM, :] * f).astype(BF16)
            vt[n, h * V_ROWS + HEAD_DIM:(h + 1) * V_ROWS, :] = jnp.where(first_row, f, 0.0).astype(BF16)
    t_l = pos(lax.broadcasted_iota(jnp.int32, (blk, blk), 1))
    s_l = pos(lax.broadcasted_iota(jnp.int32, (blk, blk), 0))
    cmask[...] = jnp.where(t_l >= s_l, 0.0, NEG_INF)
    pos_q[...] = pos(lax.broadcasted_iota(jnp.int32, (nb, blk), 1)).astype(F32)
    blk_id = lax.broadcasted_iota(jnp.int32, (nb, blk), 0)

    def prepare(i):
        par = i % 2
        q = q_ref[i * blk:(i + 1) * blk, :] * (HEAD_DIM ** -0.5 * log2e)
        for h in heads:
            qm[par, h] = jnp.where(lane_head == h, q, 0.0).astype(BF16)
        gate_s[...] = lax.dot_general(kmm[...], q, _NT, precision=lax.Precision.HIGHEST,
                                      preferred_element_type=F32)
        dist = (i - blk_id).astype(F32) * blk + (pos_q[...] - (blk - 1))
        for h in heads:
            g = gate_s[h * nb:(h + 1) * nb, :]
            rank = jnp.zeros((nb, blk), jnp.int32)
            for m in range(i):
                gm = gate_s[h * nb + m:h * nb + m + 1, :]
                rank = rank + jnp.where((gm > g) | ((gm == g) & (m < blk_id)), 1, 0)
            visible = (blk_id == i) | ((blk_id < i) & (rank < MOBA_TOP_K))
            rbias[par, h] = jnp.where(visible, -slopes2[h] * dist, NEG_INF)

    def score(k, i, n, m_run):
        out = []
        for h in heads:
            s_t = lax.dot_general(kb[n], qm[i % 2, h], _NT, preferred_element_type=F32)
            if n == i:
                s_t = s_t + cmask[...]
            s_buf[k % 2, h] = s_t
            m_blk = jnp.max(s_t, axis=0, keepdims=True) + rbias[i % 2, h, n:n + 1, :]
            out.append(m_blk if n == i else jnp.maximum(m_run[h], m_blk))
        return out

    def fold(k, i, n, m_now, m_acc):
        for h in heads:
            p = jnp.exp2(s_buf[k % 2, h] - (m_now[h] - rbias[i % 2, h, n:n + 1, :]))
            pv = _dot(vt[n, h * V_ROWS:(h + 1) * V_ROWS, :], p.astype(BF16))
            acc[i % 2, h] = pv if n == i else jnp.exp2(m_acc[h] - m_now[h]) * acc[i % 2, h] + pv

    def finalize(i):
        out_t = jnp.concatenate(
            [acc[i % 2, h, 0:HEAD_DIM, :] / acc[i % 2, h, HEAD_DIM:HEAD_DIM + 1, :] for h in heads], axis=0)
        o_ref[i * blk:(i + 1) * blk, :] = out_t.T.astype(o_ref.dtype)

    units = [(i, n) for i in range(nb) for n in [i] + list(range(i))]
    prepare(0)
    m_run = score(0, 0, 0, None)
    prepare(1)
    m_fold = None
    for k in range(1, len(units) + 1):
        pi, pn = units[k - 1]
        m_prev = m_run
        if k < len(units):
            i, n = units[k]
            m_run = score(k, i, n, m_run)
        fold(k - 1, pi, pn, m_prev, m_fold)
        m_fold = m_prev
        if pn == pi - 1 or pi == 0:
            finalize(pi)
        if k < len(units) and n == i and i + 1 < nb:
            prepare(i + 1)


def _attention(u3, w_out, w1, w2, layer):
    bsz, seq, _ = u3.shape
    nb = seq // MOBA_BLOCK
    blk = MOBA_BLOCK

    def ucol(j):
        return pl.BlockSpec((None, seq, GROUP_W), lambda b: (b, 0, j))

    def slab_in(w):
        return pl.BlockSpec((None, w.shape[1] // bsz, w.shape[2]), lambda b: (layer, b, 0))

    def slab_out(w):
        return pl.BlockSpec((w.shape[1] // bsz, w.shape[2]), lambda b: (b, 0))

    weights = (w_out, w1, w2)
    return pl.pallas_call(
        _attn_kernel,
        grid=(bsz,),
        in_specs=[ucol(0), ucol(1), ucol(2)] + [slab_in(w) for w in weights],
        out_specs=[pl.BlockSpec((None, seq, GROUP_W), lambda b: (b, 0, 0))] + [slab_out(w) for w in weights],
        out_shape=[jax.ShapeDtypeStruct((bsz, seq, GROUP_W), BF16)]
        + [jax.ShapeDtypeStruct(w.shape[1:], BF16) for w in weights],
        scratch_shapes=[
            pltpu.VMEM((nb, blk, GROUP_W), BF16),
            pltpu.VMEM((nb, ATT_HEADS * V_ROWS, blk), BF16),
            pltpu.VMEM((ATT_HEADS * nb, GROUP_W), F32),
            pltpu.VMEM((ATT_HEADS * nb, blk), F32),
            pltpu.VMEM((blk, blk), F32),
            pltpu.VMEM((nb, blk), F32),
            pltpu.VMEM((2, ATT_HEADS, nb, blk), F32),
            pltpu.VMEM((2, ATT_HEADS, blk, GROUP_W), BF16),
            pltpu.VMEM((2, ATT_HEADS, blk, blk), F32),
            pltpu.VMEM((2, ATT_HEADS, V_ROWS, blk), F32),
        ],
        compiler_params=pltpu.CompilerParams(
            dimension_semantics=("parallel",), vmem_limit_bytes=VMEM_LIMIT),
        name="moba_attention",
    )(u3, u3, u3, w_out, w1, w2)


def _out_mlp_kernel(ya_ref, yb_ref, yc_ref, yd_ref, x_ref, wo_ref, gpost_ref, gpre_ref, w1_ref, w2_ref, gmlp_ref,
                    xo_ref):
    y = None
    for j, y_ref in enumerate((ya_ref, yb_ref, yc_ref, yd_ref)):
        term = _dot(y_ref[...], wo_ref[j * GROUP_W:(j + 1) * GROUP_W, :])
        y = term if y is None else y + term
    x = x_ref[...] + _rms(y, gpost_ref[...])
    h = _rms(x, gpre_ref[...]).astype(BF16)
    m = None
    for c in range(0, D_FF, FF_CHUNK):
        t = jnp.maximum(_dot(h, w1_ref[:, c:c + FF_CHUNK]), 0.0)
        term = _dot((t * t).astype(BF16), w2_ref[c:c + FF_CHUNK, :])
        m = term if m is None else m + term
    xo_ref[...] = x + _rms(m, gmlp_ref[...])


def _out_mlp(ya, yb, yc, yd, x2, wo, gpost, gpre, w1, w2, gmlp, layer):
    tok = x2.shape[0]
    tm = TOKEN_TILE
    yspec = pl.BlockSpec((tm, GROUP_W), lambda t: (t, 0))
    xspec = pl.BlockSpec((tm, D_MODEL), lambda t: (t, 0))
    gspec = pl.BlockSpec((None, 1, D_MODEL), lambda t: (layer, 0, 0))

    def resident(shape):
        return pl.BlockSpec(shape, lambda t: (0, 0), pipeline_mode=pl.Buffered(1))

    return pl.pallas_call(
        _out_mlp_kernel,
        grid=(tok // tm,),
        in_specs=[yspec, yspec, yspec, yspec, xspec, resident((D_MODEL, D_MODEL)), gspec, gspec,
                  resident((D_MODEL, D_FF)), resident((D_FF, D_MODEL)), gspec],
        out_specs=xspec,
        out_shape=jax.ShapeDtypeStruct((tok, D_MODEL), F32),
        compiler_params=pltpu.CompilerParams(
            dimension_semantics=("parallel",), vmem_limit_bytes=VMEM_LIMIT),
        name="out_mlp",
    )(ya, yb, yc, yd, x2, wo, gpost, gpre, w1, w2, gmlp)


def _block_diag_dense(w):
    depth, nblk, n, _ = w.shape
    eye = jnp.eye(nblk, dtype=w.dtype)
    return jnp.einsum("dgij,gh->dgihj", w, eye).reshape(depth, nblk * n, nblk * n)


def kernel(x, pre_mix_g, w_in, conf_dw_w, conf_dw_b, conf_gn_g, conf_gn_b, sconv_w, lru_conv_w, lru_conv_b,
           lru_wa, lru_ba, lru_wx, lru_bx, lru_lam, w_out, post_mix_g, pre_mlp_g, mlp_w1, mlp_w2, post_mlp_g):
    bsz, seq, dm = x.shape
    depth = w_in.shape[0]
    assert dm == D_MODEL and seq % MOBA_BLOCK == 0 and (bsz * seq) % TOKEN_TILE == 0
    assert seq // MOBA_BLOCK > MOBA_TOP_K
    assert D_MODEL % (16 * bsz) == 0

    row = lambda a: a.reshape(depth, 1, a.shape[-1])
    taps = lambda w: jnp.broadcast_to(w[:, :, None, :], w.shape[:2] + (8, w.shape[-1]))
    group_of = jnp.arange(GROUP_W) // (GROUP_W // CONF_GROUPS)
    params = {
        "dww": taps(conf_dw_w), "dwb": row(conf_dw_b), "gng": row(conf_gn_g), "gnb": row(conf_gn_b),
        "scw": taps(sconv_w), "lcw": taps(lru_conv_w), "lcb": row(lru_conv_b),
        "wax": jnp.concatenate([_block_diag_dense(lru_wa), _block_diag_dense(lru_wx)], axis=-1).astype(BF16),
        "bax": row(jnp.concatenate([lru_ba, lru_bx], axis=-1)),
        "lam": row(lru_lam),
        "gm": ((group_of[:, None] == group_of[None, :]).astype(F32) / (GROUP_W // CONF_GROUPS)).astype(BF16),
    }
    w_in_b = w_in.astype(BF16)
    g_pre_mix, g_post_mix = row(pre_mix_g), row(post_mix_g)
    g_pre_mlp, g_post_mlp = row(pre_mlp_g), row(post_mlp_g)

    x2 = x.reshape(bsz * seq, dm)
    for layer in range(depth):
        u_att, h_seg = _in_proj(x2, g_pre_mix, w_in_b, layer)
        ya, yb, yd = _mixers(h_seg.reshape(bsz, seq, dm), w_in_b, params, layer)
        yc, w_out_b, w1_b, w2_b = _attention(u_att.reshape(bsz, seq, ATT_COL1 - ATT_COL0), w_out, mlp_w1, mlp_w2, layer)
        flat = lambda a: a.reshape(bsz * seq, GROUP_W)
        x2 = _out_mlp(flat(ya), flat(yb), flat(yc), flat(yd), x2, w_out_b, g_post_mix, g_pre_mlp,
                      w1_b, w2_b, g_post_mlp, layer)
    return x2.reshape(bsz, seq, dm)
```

```python
import math

import jax
import jax.numpy as jnp
from jax import lax
from jax.experimental import pallas as pl
from jax.experimental.pallas import tpu as pltpu

F32 = jnp.float32
BF16 = jnp.bfloat16

D_MODEL = 1024
GROUP_W = 256
CONF_KERNEL = 31
CONF_GROUPS = 4
GN_EPS = 1e-5
SCONV_KERNEL = 3
ATT_HEADS = 4
HEAD_DIM = GROUP_W // ATT_HEADS
MOBA_BLOCK = 256
MOBA_TOP_K = 3
LRU_BLOCKS = 4
LRU_CONV = 4
LRU_C = 8.0
D_FF = 4 * D_MODEL
RMS_EPS = 1e-6
IN_COLS = 10 * GROUP_W
ATT_COL0, ATT_COL1 = 5 * GROUP_W, 8 * GROUP_W
MIX_COLS = IN_COLS - (ATT_COL1 - ATT_COL0)

TOKEN_TILE = 1024
UNIT = MOBA_BLOCK
SEG = UNIT // 8
FF_CHUNK = 1024
V_ROWS = HEAD_DIM + 16
VMEM_LIMIT = 56 * 1024 * 1024

NEG_INF = float("-inf")


def _rms(x, g):
    return x * lax.rsqrt(jnp.mean(x * x, axis=-1, keepdims=True) + RMS_EPS) * g


def _dot(a, b):
    return jnp.dot(a, b, preferred_element_type=F32)


def _segment_permutations():
    row = lax.broadcasted_iota(jnp.int32, (UNIT, UNIT), 0)
    col = lax.broadcasted_iota(jnp.int32, (UNIT, UNIT), 1)
    to_seg = jnp.where(col == SEG * (row & 7) + (row >> 3), 1.0, 0.0).astype(BF16)
    from_seg = jnp.where(row == SEG * (col & 7) + (col >> 3), 1.0, 0.0).astype(BF16)
    return to_seg, from_seg


def _in_proj_kernel(x_ref, g_ref, wq_ref, wk_ref, wv_ref, u_ref, hseg_ref):
    h = _rms(x_ref[...], g_ref[...]).astype(BF16)
    for j, w_ref in enumerate((wq_ref, wk_ref, wv_ref)):
        u_ref[:, j * GROUP_W:(j + 1) * GROUP_W] = _dot(h, w_ref[...])
    to_seg, _ = _segment_permutations()
    for r in range(0, TOKEN_TILE, UNIT):
        hseg_ref[r:r + UNIT, :] = _dot(to_seg, h[r:r + UNIT, :]).astype(BF16)


def _in_proj(x2, g, w, layer):
    tok = x2.shape[0]
    cols = ATT_COL1 - ATT_COL0

    def wcol(j):
        return pl.BlockSpec((None, D_MODEL, GROUP_W), lambda t: (layer, 0, ATT_COL0 // GROUP_W + j))

    xspec = pl.BlockSpec((TOKEN_TILE, D_MODEL), lambda t: (t, 0))
    return pl.pallas_call(
        _in_proj_kernel,
        grid=(tok // TOKEN_TILE,),
        in_specs=[xspec, pl.BlockSpec((None, 1, D_MODEL), lambda t: (layer, 0, 0)), wcol(0), wcol(1), wcol(2)],
        out_specs=[pl.BlockSpec((TOKEN_TILE, cols), lambda t: (t, 0)), xspec],
        out_shape=[jax.ShapeDtypeStruct((tok, cols), F32), jax.ShapeDtypeStruct((tok, D_MODEL), BF16)],
        compiler_params=pltpu.CompilerParams(
            dimension_semantics=("parallel",), vmem_limit_bytes=VMEM_LIMIT),
        name="in_proj",
    )(x2, g, w, w, w)


def _group_mean(v, gm):
    hi = v.astype(BF16)
    lo = (v - hi.astype(F32)).astype(BF16)
    return _dot(hi, gm) + _dot(lo, gm)


def _seg_conv(z, e_ref, w_ref, bias, width, out_ref):
    sub = lax.broadcasted_iota(jnp.int32, (8, GROUP_W), 0)
    for a in range(SEG - (width - 1), SEG):
        cur = z[8 * a:8 * a + 8, :]
        prev = e_ref[UNIT + 8 * a:UNIT + 8 * a + 8, :]
        e_ref[8 * a:8 * a + 8, :] = pltpu.roll(jnp.where(sub == 7, prev, cur), 1, axis=0)
    e_ref[UNIT:2 * UNIT, :] = z
    for a0 in range(0, SEG, 8):
        accs = [bias] * 8
        for j in range(width):
            w_j = w_ref[width - 1 - j]
            for a in range(a0, a0 + 8):
                src = UNIT + 8 * (a - j) if a >= j else 8 * (a - j + SEG)
                term = w_j * e_ref[src:src + 8, :]
                accs[a - a0] = term if accs[a - a0] is None else accs[a - a0] + term
        for a in range(a0, a0 + 8):
            out_ref[8 * a:8 * a + 8, :] = accs[a - a0]


def _mixers_kernel(h_ref, win_lo_ref, win_hi_ref,
                   dww_ref, dwb_ref, gng_ref, gnb_ref, scw_ref, lcw_ref, lcb_ref,
                   wax_ref, bax_ref, lam_ref, gm_ref,
                   ya_ref, yb_ref, yd_ref,
                   u_even, u_odd, ea, eb, ed, cbuf, abuf, bbuf, hloc, acum, endb, hcar):
    _, from_seg = _segment_permutations()
    n_units = h_ref.shape[0] // UNIT
    for e in (ea, eb, ed):
        e[UNIT:2 * UNIT, :] = jnp.zeros((UNIT, GROUP_W), F32)
    hcar[...] = jnp.zeros_like(hcar)

    sub = lax.broadcasted_iota(jnp.int32, (8, GROUP_W), 0)
    bias_a = jnp.broadcast_to(dwb_ref[...], (8, GROUP_W))
    bias_d = jnp.broadcast_to(lcb_ref[...], (8, GROUP_W))
    lam = lam_ref[...]
    neg_c_softplus = -LRU_C * (jnp.maximum(-lam, 0.0) + jnp.log1p(jnp.exp(-jnp.abs(lam))))
    gm = gm_ref[...]

    def projection(u, u_ref):
        rows = pl.ds(pl.multiple_of(u * UNIT, UNIT), UNIT)
        step = 2 * GROUP_W

        def lo(c):
            def run():
                w = min(step, ATT_COL0 - c)
                u_ref[:, c:c + w] = _dot(h_ref[rows, :], win_lo_ref[:, c:c + w])
            return run

        def hi():
            u_ref[:, ATT_COL0:MIX_COLS] = _dot(h_ref[rows, :], win_hi_ref[...])

        return [lo(c) for c in range(0, ATT_COL0, step)] + [hi]

    def col(u_ref, j):
        return u_ref[:, j * GROUP_W:(j + 1) * GROUP_W]

    def mix(u, u_ref, ahead):
        rows = pl.ds(pl.multiple_of(u * UNIT, UNIT), UNIT)
        ahead = list(ahead)
        ahead.pop(0)()

        _seg_conv(col(u_ref, 0) * jax.nn.sigmoid(col(u_ref, 1)), ea, dww_ref, bias_a, CONF_KERNEL, cbuf)
        c = cbuf[...]
        d = c - _group_mean(c, gm)
        var = _group_mean(d * d, gm)
        ahead.pop(0)()
        yn = d * lax.rsqrt(var + GN_EPS) * gng_ref[...] + gnb_ref[...]
        ya_seg = (yn * jax.nn.sigmoid(yn)).astype(BF16)

        _seg_conv(col(u_ref, 3) * col(u_ref, 4), eb, scw_ref, None, SCONV_KERNEL, cbuf)
        yb_seg = (col(u_ref, 2) * cbuf[...]).astype(BF16)

        _seg_conv(col(u_ref, 5), ed, lcw_ref, bias_d, LRU_CONV, cbuf)
        xc = cbuf[...]
        gates = _dot(xc.astype(BF16), wax_ref[...]) + bax_ref[...]
        while ahead:
            ahead.pop(0)()
        r = jax.nn.sigmoid(gates[:, :GROUP_W])
        i = jax.nn.sigmoid(gates[:, GROUP_W:])
        log_a = neg_c_softplus * r
        abuf[...] = jnp.exp(log_a)
        th = jnp.tanh(log_a)
        bbuf[...] = jnp.sqrt(-2.0 * th / (1.0 - th)) * (i * xc)
        h_run = bbuf[0:8, :]
        a_run = abuf[0:8, :]
        hloc[0:8, :] = h_run
        acum[0:8, :] = a_run
        for s in range(1, SEG):
            a_s = abuf[8 * s:8 * s + 8, :]
            h_run = a_s * h_run + bbuf[8 * s:8 * s + 8, :]
            a_run = a_s * a_run
            hloc[8 * s:8 * s + 8, :] = h_run
            acum[8 * s:8 * s + 8, :] = a_run
        dist = 1
        while dist < 8:
            keep = sub >= dist
            h_run = jnp.where(keep, a_run * pltpu.roll(h_run, dist, axis=0) + h_run, h_run)
            a_run = jnp.where(keep, a_run * pltpu.roll(a_run, dist, axis=0), a_run)
            dist *= 2
        h_in = hcar[...]
        seg_end = a_run * h_in + h_run
        endb[...] = seg_end
        seg_in = jnp.where(sub == 0, h_in, pltpu.roll(seg_end, 1, axis=0))
        hcar[...] = jnp.broadcast_to(endb[7:8, :], (8, GROUP_W))
        for s in range(SEG):
            hloc[8 * s:8 * s + 8, :] = hloc[8 * s:8 * s + 8, :] + acum[8 * s:8 * s + 8, :] * seg_in
        g = col(u_ref, 6)
        gelu = 0.5 * g * (1.0 + jnp.tanh(math.sqrt(2.0 / math.pi) * (g + 0.044715 * (g * g * g))))
        yd_seg = (hloc[...] * gelu).astype(BF16)
        for y_ref, y_seg in ((ya_ref, ya_seg), (yb_ref, yb_seg), (yd_ref, yd_seg)):
            y_ref[rows, :] = _dot(from_seg, y_seg).astype(y_ref.dtype)

    for run in projection(0, u_even):
        run()

    def two_units(jj, carry):
        mix(2 * jj, u_even, projection(2 * jj + 1, u_odd))
        mix(2 * jj + 1, u_odd, projection(jnp.minimum(2 * jj + 2, n_units - 1), u_even))
        return carry

    lax.fori_loop(0, n_units // 2, two_units, 0)


def _mixers(h3, w_in, p, layer):
    bsz, seq, dm = h3.shape
    assert (seq // UNIT) % 2 == 0
    hi_cols = IN_COLS - ATT_COL1
    assert ATT_COL1 % hi_cols == 0

    def par(shape):
        return pl.BlockSpec((None,) + shape, lambda b: (layer,) + (0,) * len(shape))

    out_spec = pl.BlockSpec((None, seq, GROUP_W), lambda b: (b, 0, 0))
    out_sds = jax.ShapeDtypeStruct((bsz, seq, GROUP_W), BF16)
    return pl.pallas_call(
        _mixers_kernel,
        grid=(bsz,),
        in_specs=[pl.BlockSpec((None, seq, dm), lambda b: (b, 0, 0)),
                  pl.BlockSpec((None, dm, ATT_COL0), lambda b: (layer, 0, 0), pipeline_mode=pl.Buffered(1)),
                  pl.BlockSpec((None, dm, hi_cols), lambda b: (layer, 0, ATT_COL1 // hi_cols),
                               pipeline_mode=pl.Buffered(1)),
                  par((CONF_KERNEL, 8, GROUP_W)), par((1, GROUP_W)), par((1, GROUP_W)), par((1, GROUP_W)),
                  par((SCONV_KERNEL, 8, GROUP_W)), par((LRU_CONV, 8, GROUP_W)), par((1, GROUP_W)),
                  par((GROUP_W, 2 * GROUP_W)), par((1, 2 * GROUP_W)), par((1, GROUP_W)),
                  pl.BlockSpec((GROUP_W, GROUP_W), lambda b: (0, 0))],
        out_specs=[out_spec, out_spec, out_spec],
        out_shape=[out_sds, out_sds, out_sds],
        scratch_shapes=(
            [pltpu.VMEM((UNIT, MIX_COLS), F32)] * 2
            + [pltpu.VMEM((2 * UNIT, GROUP_W), F32)] * 3
            + [pltpu.VMEM((UNIT, GROUP_W), F32)] * 5
            + [pltpu.VMEM((8, GROUP_W), F32)] * 2
        ),
        compiler_params=pltpu.CompilerParams(
            dimension_semantics=("parallel",), vmem_limit_bytes=VMEM_LIMIT),
        name="mixers",
    )(h3, w_in, w_in,
      p["dww"], p["dwb"], p["gng"], p["gnb"], p["scw"], p["lcw"], p["lcb"],
      p["wax"], p["bax"], p["lam"], p["gm"])


_NT = (((1,), (1,)), ((), ()))


def _attn_kernel(q_ref, k_ref, v_ref, wo_ref, w1_ref, w2_ref, o_ref, wo_b_ref, w1_b_ref, w2_b_ref,
                 kb, vt, kmm, gate_s, cmask, pos_q, rbias, qm, s_buf, acc):
    wo_b_ref[...] = wo_ref[...].astype(BF16)
    w1_b_ref[...] = w1_ref[...].astype(BF16)
    w2_b_ref[...] = w2_ref[...].astype(BF16)
    nb = kb.shape[0]
    blk = MOBA_BLOCK
    heads = range(ATT_HEADS)
    log2e = math.log2(math.e)
    slopes2 = [log2e * 2.0 ** (-8.0 * (h + 1) / ATT_HEADS) for h in heads]
    lane_head = lax.broadcasted_iota(jnp.int32, (1, GROUP_W), 1) // HEAD_DIM
    pos = lambda rho: rho

    pos_s = pos(lax.broadcasted_iota(jnp.int32, (1, blk), 1))
    rel = (pos_s - (blk - 1)).astype(F32)
    first_row = lax.broadcasted_iota(jnp.int32, (V_ROWS - HEAD_DIM, blk), 0) == 0
    for n in range(nb):
        kblk = k_ref[n * blk:(n + 1) * blk, :]
        kb[n] = kblk.astype(BF16)
        v_t = v_ref[n * blk:(n + 1) * blk, :].T
        kmean = jnp.mean(kblk, axis=0, keepdims=True)
        for h in heads:
            kmm[h * nb + n:h * nb + n + 1, :] = jnp.where(lane_head == h, kmean, 0.0)
            f = jnp.exp2(slopes2[h] * rel)
            vt[n, h * V_ROWS:h * V_ROWS + HEAD_DIM, :] = (v_t[h * HEAD_DIM:(h + 1) * HEAD_DIM, :] * f).astype(BF16)
            vt[n, h * V_ROWS + HEAD_DIM:(h + 1) * V_ROWS, :] = jnp.where(first_row, f, 0.0).astype(BF16)
    t_l = pos(lax.broadcasted_iota(jnp.int32, (blk, blk), 1))
    s_l = pos(lax.broadcasted_iota(jnp.int32, (blk, blk), 0))
    cmask[...] = jnp.where(t_l >= s_l, 0.0, NEG_INF)
    pos_q[...] = pos(lax.broadcasted_iota(jnp.int32, (nb, blk), 1)).astype(F32)
    blk_id = lax.broadcasted_iota(jnp.int32, (nb, blk), 0)

    def prepare(i):
        par = i % 2
        q = q_ref[i * blk:(i + 1) * blk, :] * (HEAD_DIM ** -0.5 * log2e)
        for h in heads:
            qm[par, h] = jnp.where(lane_head == h, q, 0.0).astype(BF16)
        gate_s[...] = lax.dot_general(kmm[...], q, _NT, precision=lax.Precision.HIGHEST,
                                      preferred_element_type=F32)
        dist = (i - blk_id).astype(F32) * blk + (pos_q[...] - (blk - 1))
        for h in heads:
            g = gate_s[h * nb:(h + 1) * nb, :]
            rank = jnp.zeros((nb, blk), jnp.int32)
            for m in range(i):
                gm = gate_s[h * nb + m:h * nb + m + 1, :]
                rank = rank + jnp.where((gm > g) | ((gm == g) & (m < blk_id)), 1, 0)
            visible = (blk_id == i) | ((blk_id < i) & (rank < MOBA_TOP_K))
            rbias[par, h] = jnp.where(visible, -slopes2[h] * dist, NEG_INF)

    def score(k, i, n, m_run):
        out = []
        for h in heads:
            s_t = lax.dot_general(kb[n], qm[i % 2, h], _NT, preferred_element_type=F32)
            if n == i:
                s_t = s_t + cmask[...]
            s_buf[k % 2, h] = s_t
            m_blk = jnp.max(s_t, axis=0, keepdims=True) + rbias[i % 2, h, n:n + 1, :]
            out.append(m_blk if n == i else jnp.maximum(m_run[h], m_blk))
        return out

    def fold(k, i, n, m_now, m_acc):
        for h in heads:
            p = jnp.exp2(s_buf[k % 2, h] - (m_now[h] - rbias[i % 2, h, n:n + 1, :]))
            pv = _dot(vt[n, h * V_ROWS:(h + 1) * V_ROWS, :], p.astype(BF16))
            acc[i % 2, h] = pv if n == i else jnp.exp2(m_acc[h] - m_now[h]) * acc[i % 2, h] + pv

    def finalize(i):
        out_t = jnp.concatenate(
            [acc[i % 2, h, 0:HEAD_DIM, :] / acc[i % 2, h, HEAD_DIM:HEAD_DIM + 1, :] for h in heads], axis=0)
        o_ref[i * blk:(i + 1) * blk, :] = out_t.T.astype(o_ref.dtype)

    units = [(i, n) for i in range(nb) for n in [i] + list(range(i))]
    prepare(0)
    m_run = score(0, 0, 0, None)
    prepare(1)
    m_fold = None
    for k in range(1, len(units) + 1):
        pi, pn = units[k - 1]
        m_prev = m_run
        if k < len(units):
            i, n = units[k]
            m_run = score(k, i, n, m_run)
        fold(k - 1, pi, pn, m_prev, m_fold)
        m_fold = m_prev
        if pn == pi - 1 or pi == 0:
            finalize(pi)
        if k < len(units) and n == i and i + 1 < nb:
            prepare(i + 1)


def _attention(u3, w_out, w1, w2, layer):
    bsz, seq, _ = u3.shape
    nb = seq // MOBA_BLOCK
    blk = MOBA_BLOCK

    def ucol(j):
        return pl.BlockSpec((None, seq, GROUP_W), lambda b: (b, 0, j))

    def slab_in(w):
        return pl.BlockSpec((None, w.shape[1] // bsz, w.shape[2]), lambda b: (layer, b, 0))

    def slab_out(w):
        return pl.BlockSpec((w.shape[1] // bsz, w.shape[2]), lambda b: (b, 0))

    weights = (w_out, w1, w2)
    return pl.pallas_call(
        _attn_kernel,
        grid=(bsz,),
        in_specs=[ucol(0), ucol(1), ucol(2)] + [slab_in(w) for w in weights],
        out_specs=[pl.BlockSpec((None, seq, GROUP_W), lambda b: (b, 0, 0))] + [slab_out(w) for w in weights],
        out_shape=[jax.ShapeDtypeStruct((bsz, seq, GROUP_W), BF16)]
        + [jax.ShapeDtypeStruct(w.shape[1:], BF16) for w in weights],
        scratch_shapes=[
            pltpu.VMEM((nb, blk, GROUP_W), BF16),
            pltpu.VMEM((nb, ATT_HEADS * V_ROWS, blk), BF16),
            pltpu.VMEM((ATT_HEADS * nb, GROUP_W), F32),
            pltpu.VMEM((ATT_HEADS * nb, blk), F32),
            pltpu.VMEM((blk, blk), F32),
            pltpu.VMEM((nb, blk), F32),
            pltpu.VMEM((2, ATT_HEADS, nb, blk), F32),
            pltpu.VMEM((2, ATT_HEADS, blk, GROUP_W), BF16),
            pltpu.VMEM((2, ATT_HEADS, blk, blk), F32),
            pltpu.VMEM((2, ATT_HEADS, V_ROWS, blk), F32),
        ],
        compiler_params=pltpu.CompilerParams(
            dimension_semantics=("parallel",), vmem_limit_bytes=VMEM_LIMIT),
        name="moba_attention",
    )(u3, u3, u3, w_out, w1, w2)


def _out_mlp_kernel(ya_ref, yb_ref, yc_ref, yd_ref, x_ref, wo_ref, gpost_ref, gpre_ref, w1_ref, w2_ref, gmlp_ref,
                    xo_ref):
    half = x_ref.shape[0] // 2
    halves = [slice(0, half), slice(half, 2 * half)]

    def out_projection(rows):
        y = None
        for j, y_ref in enumerate((ya_ref, yb_ref, yc_ref, yd_ref)):
            term = _dot(y_ref[rows, :], wo_ref[j * GROUP_W:(j + 1) * GROUP_W, :])
            y = term if y is None else y + term
        return y

    def mlp(h):
        m = None
        for c in range(0, D_FF, FF_CHUNK):
            t = jnp.maximum(_dot(h, w1_ref[:, c:c + FF_CHUNK]), 0.0)
            term = _dot((t * t).astype(BF16), w2_ref[c:c + FF_CHUNK, :])
            m = term if m is None else m + term
        return m

    ys = [out_projection(rows) for rows in halves]
    for rows, y in zip(halves, ys):
        xo_ref[rows, :] = x_ref[rows, :] + _rms(y, gpost_ref[...])
        m = mlp(_rms(xo_ref[rows, :], gpre_ref[...]).astype(BF16))
        xo_ref[rows, :] = xo_ref[rows, :] + _rms(m, gmlp_ref[...])


def _out_mlp(ya, yb, yc, yd, x2, wo, gpost, gpre, w1, w2, gmlp, layer):
    tok = x2.shape[0]
    tm = TOKEN_TILE
    yspec = pl.BlockSpec((tm, GROUP_W), lambda t: (t, 0))
    xspec = pl.BlockSpec((tm, D_MODEL), lambda t: (t, 0))
    gspec = pl.BlockSpec((None, 1, D_MODEL), lambda t: (layer, 0, 0))

    def resident(shape):
        return pl.BlockSpec(shape, lambda t: (0, 0), pipeline_mode=pl.Buffered(1))

    return pl.pallas_call(
        _out_mlp_kernel,
        grid=(tok // tm,),
        in_specs=[yspec, yspec, yspec, yspec, xspec, resident((D_MODEL, D_MODEL)), gspec, gspec,
                  resident((D_MODEL, D_FF)), resident((D_FF, D_MODEL)), gspec],
        out_specs=xspec,
        out_shape=jax.ShapeDtypeStruct((tok, D_MODEL), F32),
        compiler_params=pltpu.CompilerParams(
            dimension_semantics=("parallel",), vmem_limit_bytes=VMEM_LIMIT),
        name="out_mlp",
    )(ya, yb, yc, yd, x2, wo, gpost, gpre, w1, w2, gmlp)


def _block_diag_dense(w):
    depth, nblk, n, _ = w.shape
    eye = jnp.eye(nblk, dtype=w.dtype)
    return jnp.einsum("dgij,gh->dgihj", w, eye).reshape(depth, nblk * n, nblk * n)


def kernel(x, pre_mix_g, w_in, conf_dw_w, conf_dw_b, conf_gn_g, conf_gn_b, sconv_w, lru_conv_w, lru_conv_b,
           lru_wa, lru_ba, lru_wx, lru_bx, lru_lam, w_out, post_mix_g, pre_mlp_g, mlp_w1, mlp_w2, post_mlp_g):
    bsz, seq, dm = x.shape
    depth = w_in.shape[0]
    assert dm == D_MODEL and seq % MOBA_BLOCK == 0 and (bsz * seq) % TOKEN_TILE == 0
    assert seq // MOBA_BLOCK > MOBA_TOP_K
    assert D_MODEL % (16 * bsz) == 0

    row = lambda a: a.reshape(depth, 1, a.shape[-1])
    taps = lambda w: jnp.broadcast_to(w[:, :, None, :], w.shape[:2] + (8, w.shape[-1]))
    group_of = jnp.arange(GROUP_W) // (GROUP_W // CONF_GROUPS)
    params = {
        "dww": taps(conf_dw_w), "dwb": row(conf_dw_b), "gng": row(conf_gn_g), "gnb": row(conf_gn_b),
        "scw": taps(sconv_w), "lcw": taps(lru_conv_w), "lcb": row(lru_conv_b),
        "wax": jnp.concatenate([_block_diag_dense(lru_wa), _block_diag_dense(lru_wx)], axis=-1).astype(BF16),
        "bax": row(jnp.concatenate([lru_ba, lru_bx], axis=-1)),
        "lam": row(lru_lam),
        "gm": ((group_of[:, None] == group_of[None, :]).astype(F32) / (GROUP_W // CONF_GROUPS)).astype(BF16),
    }
    w_in_b = w_in.astype(BF16)
    g_pre_mix, g_post_mix = row(pre_mix_g), row(post_mix_g)
    g_pre_mlp, g_post_mlp = row(pre_mlp_g), row(post_mlp_g)

    x2 = x.reshape(bsz * seq, dm)
    for layer in range(depth):
        u_att, h_seg = _in_proj(x2, g_pre_mix, w_in_b, layer)
        ya, yb, yd = _mixers(h_seg.reshape(bsz, seq, dm), w_in_b, params, layer)
        yc, w_out_b, w1_b, w2_b = _attention(u_att.reshape(bsz, seq, ATT_COL1 - ATT_COL0), w_out, mlp_w1, mlp_w2, layer)
        flat = lambda a: a.reshape(bsz * seq, GROUP_W)
        x2 = _out_mlp(flat(ya), flat(yb), flat(yc), flat(yd), x2, w_out_b, g_post_mix, g_pre_mlp,
                      w1_b, w2_b, g_post_mlp, layer)
    return x2.reshape(bsz, seq, dm)
```

```python
import functools
import math

import jax
import jax.numpy as jnp
from jax import lax
from jax.experimental import pallas as pl
from jax.experimental.pallas import tpu as pltpu

F32 = jnp.float32
BF16 = jnp.bfloat16

D_MODEL = 1024
GROUP_W = 256
CONF_KERNEL = 31
CONF_GROUPS = 4
GN_EPS = 1e-5
SCONV_KERNEL = 3
ATT_HEADS = 4
HEAD_DIM = GROUP_W // ATT_HEADS
MOBA_BLOCK = 256
MOBA_TOP_K = 3
LRU_BLOCKS = 4
LRU_CONV = 4
LRU_C = 8.0
D_FF = 4 * D_MODEL
RMS_EPS = 1e-6
IN_COLS = 10 * GROUP_W
ATT_COL0, ATT_COL1 = 5 * GROUP_W, 8 * GROUP_W
MIX_COLS = IN_COLS - (ATT_COL1 - ATT_COL0)

TOKEN_TILE = 1024
UNIT = MOBA_BLOCK
SEG = UNIT // 8
FF_CHUNK = 1024
V_ROWS = HEAD_DIM + 16
VMEM_LIMIT = 56 * 1024 * 1024
VMEM_LIMIT_TAIL = 62 * 1024 * 1024

NEG_INF = float("-inf")


def _rms(x, g):
    return x * lax.rsqrt(jnp.mean(x * x, axis=-1, keepdims=True) + RMS_EPS) * g


def _dot(a, b):
    return jnp.dot(a, b, preferred_element_type=F32)


def _segment_permutations():
    row = lax.broadcasted_iota(jnp.int32, (UNIT, UNIT), 0)
    col = lax.broadcasted_iota(jnp.int32, (UNIT, UNIT), 1)
    to_seg = jnp.where(col == SEG * (row & 7) + (row >> 3), 1.0, 0.0).astype(BF16)
    from_seg = jnp.where(row == SEG * (col & 7) + (col >> 3), 1.0, 0.0).astype(BF16)
    return to_seg, from_seg


def _in_proj_tile(x, g_ref, w_refs, u_ref, hseg_ref, row0):
    rows = x.shape[0]
    h = _rms(x, g_ref[...]).astype(BF16)
    for j, w_ref in enumerate(w_refs):
        u_ref[row0:row0 + rows, j * GROUP_W:(j + 1) * GROUP_W] = _dot(h, w_ref[...])
    to_seg, _ = _segment_permutations()
    for r in range(0, rows, UNIT):
        hseg_ref[row0 + r:row0 + r + UNIT, :] = _dot(to_seg, h[r:r + UNIT, :]).astype(BF16)


def _in_proj_kernel(x_ref, g_ref, wq_ref, wk_ref, wv_ref, u_ref, hseg_ref):
    _in_proj_tile(x_ref[...], g_ref, (wq_ref, wk_ref, wv_ref), u_ref, hseg_ref, 0)


def _in_proj(x2, g, w, layer):
    tok = x2.shape[0]
    cols = ATT_COL1 - ATT_COL0

    def wcol(j):
        return pl.BlockSpec((None, D_MODEL, GROUP_W), lambda t: (layer, 0, ATT_COL0 // GROUP_W + j))

    xspec = pl.BlockSpec((TOKEN_TILE, D_MODEL), lambda t: (t, 0))
    return pl.pallas_call(
        _in_proj_kernel,
        grid=(tok // TOKEN_TILE,),
        in_specs=[xspec, pl.BlockSpec((None, 1, D_MODEL), lambda t: (layer, 0, 0)), wcol(0), wcol(1), wcol(2)],
        out_specs=[pl.BlockSpec((TOKEN_TILE, cols), lambda t: (t, 0)), xspec],
        out_shape=[jax.ShapeDtypeStruct((tok, cols), F32), jax.ShapeDtypeStruct((tok, D_MODEL), BF16)],
        compiler_params=pltpu.CompilerParams(
            dimension_semantics=("parallel",), vmem_limit_bytes=VMEM_LIMIT),
        name="in_proj",
    )(x2, g, w, w, w)


def _group_mean(v, gm):
    hi = v.astype(BF16)
    lo = (v - hi.astype(F32)).astype(BF16)
    return _dot(hi, gm) + _dot(lo, gm)


def _seg_conv(z, e_ref, w_ref, bias, width, out_ref):
    sub = lax.broadcasted_iota(jnp.int32, (8, GROUP_W), 0)
    for a in range(SEG - (width - 1), SEG):
        cur = z[8 * a:8 * a + 8, :]
        prev = e_ref[UNIT + 8 * a:UNIT + 8 * a + 8, :]
        e_ref[8 * a:8 * a + 8, :] = pltpu.roll(jnp.where(sub == 7, prev, cur), 1, axis=0)
    e_ref[UNIT:2 * UNIT, :] = z
    for a0 in range(0, SEG, 8):
        accs = [bias] * 8
        for j in range(width):
            w_j = w_ref[width - 1 - j]
            for a in range(a0, a0 + 8):
                src = UNIT + 8 * (a - j) if a >= j else 8 * (a - j + SEG)
                term = w_j * e_ref[src:src + 8, :]
                accs[a - a0] = term if accs[a - a0] is None else accs[a - a0] + term
        for a in range(a0, a0 + 8):
            out_ref[8 * a:8 * a + 8, :] = accs[a - a0]


def _mixers_kernel(h_ref, win_lo_ref, win_hi_ref,
                   dww_ref, dwb_ref, gng_ref, gnb_ref, scw_ref, lcw_ref, lcb_ref,
                   wax_ref, bax_ref, lam_ref, gm_ref,
                   ya_ref, yb_ref, yd_ref,
                   u_even, u_odd, ea, eb, ed, cbuf, abuf, bbuf, hloc, acum, endb, hcar):
    _, from_seg = _segment_permutations()
    n_units = h_ref.shape[0] // UNIT
    for e in (ea, eb, ed):
        e[UNIT:2 * UNIT, :] = jnp.zeros((UNIT, GROUP_W), F32)
    hcar[...] = jnp.zeros_like(hcar)

    sub = lax.broadcasted_iota(jnp.int32, (8, GROUP_W), 0)
    bias_a = jnp.broadcast_to(dwb_ref[...], (8, GROUP_W))
    bias_d = jnp.broadcast_to(lcb_ref[...], (8, GROUP_W))
    lam = lam_ref[...]
    neg_c_softplus = -LRU_C * (jnp.maximum(-lam, 0.0) + jnp.log1p(jnp.exp(-jnp.abs(lam))))
    gm = gm_ref[...]

    def projection(u, u_ref):
        rows = pl.ds(pl.multiple_of(u * UNIT, UNIT), UNIT)
        step = 2 * GROUP_W

        def lo(c):
            def run():
                w = min(step, ATT_COL0 - c)
                u_ref[:, c:c + w] = _dot(h_ref[rows, :], win_lo_ref[:, c:c + w])
            return run

        def hi():
            u_ref[:, ATT_COL0:MIX_COLS] = _dot(h_ref[rows, :], win_hi_ref[...])

        return [lo(c) for c in range(0, ATT_COL0, step)] + [hi]

    def col(u_ref, j):
        return u_ref[:, j * GROUP_W:(j + 1) * GROUP_W]

    def mix(u, u_ref, ahead):
        rows = pl.ds(pl.multiple_of(u * UNIT, UNIT), UNIT)
        ahead = list(ahead)
        ahead.pop(0)()

        _seg_conv(col(u_ref, 0) * jax.nn.sigmoid(col(u_ref, 1)), ea, dww_ref, bias_a, CONF_KERNEL, cbuf)
        c = cbuf[...]
        d = c - _group_mean(c, gm)
        var = _group_mean(d * d, gm)
        ahead.pop(0)()
        yn = d * lax.rsqrt(var + GN_EPS) * gng_ref[...] + gnb_ref[...]
        ya_seg = (yn * jax.nn.sigmoid(yn)).astype(BF16)

        _seg_conv(col(u_ref, 3) * col(u_ref, 4), eb, scw_ref, None, SCONV_KERNEL, cbuf)
        yb_seg = (col(u_ref, 2) * cbuf[...]).astype(BF16)

        _seg_conv(col(u_ref, 5), ed, lcw_ref, bias_d, LRU_CONV, cbuf)
        xc = cbuf[...]
        gates = _dot(xc.astype(BF16), wax_ref[...]) + bax_ref[...]
        while ahead:
            ahead.pop(0)()
        r = jax.nn.sigmoid(gates[:, :GROUP_W])
        i = jax.nn.sigmoid(gates[:, GROUP_W:])
        log_a = neg_c_softplus * r
        abuf[...] = jnp.exp(log_a)
        th = jnp.tanh(log_a)
        bbuf[...] = jnp.sqrt(-2.0 * th / (1.0 - th)) * (i * xc)
        h_run = bbuf[0:8, :]
        a_run = abuf[0:8, :]
        hloc[0:8, :] = h_run
        acum[0:8, :] = a_run
        for s in range(1, SEG):
            a_s = abuf[8 * s:8 * s + 8, :]
            h_run = a_s * h_run + bbuf[8 * s:8 * s + 8, :]
            a_run = a_s * a_run
            hloc[8 * s:8 * s + 8, :] = h_run
            acum[8 * s:8 * s + 8, :] = a_run
        dist = 1
        while dist < 8:
            keep = sub >= dist
            h_run = jnp.where(keep, a_run * pltpu.roll(h_run, dist, axis=0) + h_run, h_run)
            a_run = jnp.where(keep, a_run * pltpu.roll(a_run, dist, axis=0), a_run)
            dist *= 2
        h_in = hcar[...]
        seg_end = a_run * h_in + h_run
        endb[...] = seg_end
        seg_in = jnp.where(sub == 0, h_in, pltpu.roll(seg_end, 1, axis=0))
        hcar[...] = jnp.broadcast_to(endb[7:8, :], (8, GROUP_W))
        for s in range(SEG):
            hloc[8 * s:8 * s + 8, :] = hloc[8 * s:8 * s + 8, :] + acum[8 * s:8 * s + 8, :] * seg_in
        g = col(u_ref, 6)
        gelu = 0.5 * g * (1.0 + jnp.tanh(math.sqrt(2.0 / math.pi) * (g + 0.044715 * (g * g * g))))
        yd_seg = (hloc[...] * gelu).astype(BF16)
        for y_ref, y_seg in ((ya_ref, ya_seg), (yb_ref, yb_seg), (yd_ref, yd_seg)):
            y_ref[rows, :] = _dot(from_seg, y_seg).astype(y_ref.dtype)

    for run in projection(0, u_even):
        run()

    def two_units(jj, carry):
        mix(2 * jj, u_even, projection(2 * jj + 1, u_odd))
        mix(2 * jj + 1, u_odd, projection(jnp.minimum(2 * jj + 2, n_units - 1), u_even))
        return carry

    lax.fori_loop(0, n_units // 2, two_units, 0)


def _mixers(h3, w_in, p, layer):
    bsz, seq, dm = h3.shape
    assert (seq // UNIT) % 2 == 0
    hi_cols = IN_COLS - ATT_COL1
    assert ATT_COL1 % hi_cols == 0

    def par(shape):
        return pl.BlockSpec((None,) + shape, lambda b: (layer,) + (0,) * len(shape))

    out_spec = pl.BlockSpec((None, seq, GROUP_W), lambda b: (b, 0, 0))
    out_sds = jax.ShapeDtypeStruct((bsz, seq, GROUP_W), BF16)
    return pl.pallas_call(
        _mixers_kernel,
        grid=(bsz,),
        in_specs=[pl.BlockSpec((None, seq, dm), lambda b: (b, 0, 0)),
                  pl.BlockSpec((None, dm, ATT_COL0), lambda b: (layer, 0, 0), pipeline_mode=pl.Buffered(1)),
                  pl.BlockSpec((None, dm, hi_cols), lambda b: (layer, 0, ATT_COL1 // hi_cols),
                               pipeline_mode=pl.Buffered(1)),
                  par((CONF_KERNEL, 8, GROUP_W)), par((1, GROUP_W)), par((1, GROUP_W)), par((1, GROUP_W)),
                  par((SCONV_KERNEL, 8, GROUP_W)), par((LRU_CONV, 8, GROUP_W)), par((1, GROUP_W)),
                  par((GROUP_W, 2 * GROUP_W)), par((1, 2 * GROUP_W)), par((1, GROUP_W)),
                  pl.BlockSpec((GROUP_W, GROUP_W), lambda b: (0, 0))],
        out_specs=[out_spec, out_spec, out_spec],
        out_shape=[out_sds, out_sds, out_sds],
        scratch_shapes=(
            [pltpu.VMEM((UNIT, MIX_COLS), F32)] * 2
            + [pltpu.VMEM((2 * UNIT, GROUP_W), F32)] * 3
            + [pltpu.VMEM((UNIT, GROUP_W), F32)] * 5
            + [pltpu.VMEM((8, GROUP_W), F32)] * 2
        ),
        compiler_params=pltpu.CompilerParams(
            dimension_semantics=("parallel",), vmem_limit_bytes=VMEM_LIMIT),
        name="mixers",
    )(h3, w_in, w_in,
      p["dww"], p["dwb"], p["gng"], p["gnb"], p["scw"], p["lcw"], p["lcb"],
      p["wax"], p["bax"], p["lam"], p["gm"])


_NT = (((1,), (1,)), ((), ()))


def _attn_kernel(q_ref, k_ref, v_ref, wo_ref, w1_ref, w2_ref, o_ref, wo_b_ref, w1_b_ref, w2_b_ref,
                 kb, vt, kmm, gate_s, cmask, pos_q, rbias, qm, s_buf, acc):
    wo_b_ref[...] = wo_ref[...].astype(BF16)
    w1_b_ref[...] = w1_ref[...].astype(BF16)
    w2_b_ref[...] = w2_ref[...].astype(BF16)
    nb = kb.shape[0]
    blk = MOBA_BLOCK
    heads = range(ATT_HEADS)
    log2e = math.log2(math.e)
    slopes2 = [log2e * 2.0 ** (-8.0 * (h + 1) / ATT_HEADS) for h in heads]
    lane_head = lax.broadcasted_iota(jnp.int32, (1, GROUP_W), 1) // HEAD_DIM
    pos = lambda rho: rho

    pos_s = pos(lax.broadcasted_iota(jnp.int32, (1, blk), 1))
    rel = (pos_s - (blk - 1)).astype(F32)
    first_row = lax.broadcasted_iota(jnp.int32, (V_ROWS - HEAD_DIM, blk), 0) == 0
    for n in range(nb):
        kblk = k_ref[n * blk:(n + 1) * blk, :]
        kb[n] = kblk.astype(BF16)
        v_t = v_ref[n * blk:(n + 1) * blk, :].T
        kmean = jnp.mean(kblk, axis=0, keepdims=True)
        for h in heads:
            kmm[h * nb + n:h * nb + n + 1, :] = jnp.where(lane_head == h, kmean, 0.0)
            f = jnp.exp2(slopes2[h] * rel)
            vt[n, h * V_ROWS:h * V_ROWS + HEAD_DIM, :] = (v_t[h * HEAD_DIM:(h + 1) * HEAD_DIM, :] * f).astype(BF16)
            vt[n, h * V_ROWS + HEAD_DIM:(h + 1) * V_ROWS, :] = jnp.where(first_row, f, 0.0).astype(BF16)
    t_l = pos(lax.broadcasted_iota(jnp.int32, (blk, blk), 1))
    s_l = pos(lax.broadcasted_iota(jnp.int32, (blk, blk), 0))
    cmask[...] = jnp.where(t_l >= s_l, 0.0, NEG_INF)
    pos_q[...] = pos(lax.broadcasted_iota(jnp.int32, (nb, blk), 1)).astype(F32)
    blk_id = lax.broadcasted_iota(jnp.int32, (nb, blk), 0)

    def prepare(i):
        par = i % 2
        q = q_ref[i * blk:(i + 1) * blk, :] * (HEAD_DIM ** -0.5 * log2e)
        for h in heads:
            qm[par, h] = jnp.where(lane_head == h, q, 0.0).astype(BF16)
        gate_s[...] = lax.dot_general(kmm[...], q, _NT, precision=lax.Precision.HIGHEST,
                                      preferred_element_type=F32)
        dist = (i - blk_id).astype(F32) * blk + (pos_q[...] - (blk - 1))
        for h in heads:
            g = gate_s[h * nb:(h + 1) * nb, :]
            rank = jnp.zeros((nb, blk), jnp.int32)
            for m in range(i):
                gm = gate_s[h * nb + m:h * nb + m + 1, :]
                rank = rank + jnp.where((gm > g) | ((gm == g) & (m < blk_id)), 1, 0)
            visible = (blk_id == i) | ((blk_id < i) & (rank < MOBA_TOP_K))
            rbias[par, h] = jnp.where(visible, -slopes2[h] * dist, NEG_INF)

    def score(k, i, n, m_run):
        out = []
        for h in heads:
            s_t = lax.dot_general(kb[n], qm[i % 2, h], _NT, preferred_element_type=F32)
            if n == i:
                s_t = s_t + cmask[...]
            s_buf[k % 2, h] = s_t
            m_blk = jnp.max(s_t, axis=0, keepdims=True) + rbias[i % 2, h, n:n + 1, :]
            out.append(m_blk if n == i else jnp.maximum(m_run[h], m_blk))
        return out

    def fold(k, i, n, m_now, m_acc):
        for h in heads:
            p = jnp.exp2(s_buf[k % 2, h] - (m_now[h] - rbias[i % 2, h, n:n + 1, :]))
            pv = _dot(vt[n, h * V_ROWS:(h + 1) * V_ROWS, :], p.astype(BF16))
            acc[i % 2, h] = pv if n == i else jnp.exp2(m_acc[h] - m_now[h]) * acc[i % 2, h] + pv

    def finalize(i):
        out_t = jnp.concatenate(
            [acc[i % 2, h, 0:HEAD_DIM, :] / acc[i % 2, h, HEAD_DIM:HEAD_DIM + 1, :] for h in heads], axis=0)
        o_ref[i * blk:(i + 1) * blk, :] = out_t.T.astype(o_ref.dtype)

    units = [(i, n) for i in range(nb) for n in [i] + list(range(i))]
    prepare(0)
    m_run = score(0, 0, 0, None)
    prepare(1)
    m_fold = None
    for k in range(1, len(units) + 1):
        pi, pn = units[k - 1]
        m_prev = m_run
        if k < len(units):
            i, n = units[k]
            m_run = score(k, i, n, m_run)
        fold(k - 1, pi, pn, m_prev, m_fold)
        m_fold = m_prev
        if pn == pi - 1 or pi == 0:
            finalize(pi)
        if k < len(units) and n == i and i + 1 < nb:
            prepare(i + 1)


def _attention(u3, w_out, w1, w2, layer):
    bsz, seq, _ = u3.shape
    nb = seq // MOBA_BLOCK
    blk = MOBA_BLOCK

    def ucol(j):
        return pl.BlockSpec((None, seq, GROUP_W), lambda b: (b, 0, j))

    def slab_in(w):
        return pl.BlockSpec((None, w.shape[1] // bsz, w.shape[2]), lambda b: (layer, b, 0))

    def slab_out(w):
        return pl.BlockSpec((w.shape[1] // bsz, w.shape[2]), lambda b: (b, 0))

    weights = (w_out, w1, w2)
    return pl.pallas_call(
        _attn_kernel,
        grid=(bsz,),
        in_specs=[ucol(0), ucol(1), ucol(2)] + [slab_in(w) for w in weights],
        out_specs=[pl.BlockSpec((None, seq, GROUP_W), lambda b: (b, 0, 0))] + [slab_out(w) for w in weights],
        out_shape=[jax.ShapeDtypeStruct((bsz, seq, GROUP_W), BF16)]
        + [jax.ShapeDtypeStruct(w.shape[1:], BF16) for w in weights],
        scratch_shapes=[
            pltpu.VMEM((nb, blk, GROUP_W), BF16),
            pltpu.VMEM((nb, ATT_HEADS * V_ROWS, blk), BF16),
            pltpu.VMEM((ATT_HEADS * nb, GROUP_W), F32),
            pltpu.VMEM((ATT_HEADS * nb, blk), F32),
            pltpu.VMEM((blk, blk), F32),
            pltpu.VMEM((nb, blk), F32),
            pltpu.VMEM((2, ATT_HEADS, nb, blk), F32),
            pltpu.VMEM((2, ATT_HEADS, blk, GROUP_W), BF16),
            pltpu.VMEM((2, ATT_HEADS, blk, blk), F32),
            pltpu.VMEM((2, ATT_HEADS, V_ROWS, blk), F32),
        ],
        compiler_params=pltpu.CompilerParams(
            dimension_semantics=("parallel",), vmem_limit_bytes=VMEM_LIMIT),
        name="moba_attention",
    )(u3, u3, u3, w_out, w1, w2)


def _out_mlp_kernel(with_next, ya_ref, yb_ref, yc_ref, yd_ref, x_ref, wo_ref, gpost_ref, gpre_ref, w1_ref, w2_ref,
                    gmlp_ref, *rest):
    if with_next:
        gnext_ref, wq_ref, wk_ref, wv_ref, xo_ref, u_ref, hseg_ref = rest
    else:
        (xo_ref,) = rest
    half = x_ref.shape[0] // 2
    halves = [slice(0, half), slice(half, 2 * half)]

    def out_projection(rows):
        y = None
        for j, y_ref in enumerate((ya_ref, yb_ref, yc_ref, yd_ref)):
            term = _dot(y_ref[rows, :], wo_ref[j * GROUP_W:(j + 1) * GROUP_W, :])
            y = term if y is None else y + term
        return y

    def mlp(h):
        m = None
        for c in range(0, D_FF, FF_CHUNK):
            t = jnp.maximum(_dot(h, w1_ref[:, c:c + FF_CHUNK]), 0.0)
            term = _dot((t * t).astype(BF16), w2_ref[c:c + FF_CHUNK, :])
            m = term if m is None else m + term
        return m

    ys = [out_projection(rows) for rows in halves]
    for rows, y in zip(halves, ys):
        xo_ref[rows, :] = x_ref[rows, :] + _rms(y, gpost_ref[...])
        m = mlp(_rms(xo_ref[rows, :], gpre_ref[...]).astype(BF16))
        xo_ref[rows, :] = xo_ref[rows, :] + _rms(m, gmlp_ref[...])
    if with_next:
        for rows in halves:
            _in_proj_tile(xo_ref[rows, :], gnext_ref, (wq_ref, wk_ref, wv_ref), u_ref, hseg_ref, rows.start)


def _out_mlp(ya, yb, yc, yd, x2, wo, gpost, gpre, w1, w2, gmlp, layer, w_in=None, gnext=None):
    tok = x2.shape[0]
    tm = TOKEN_TILE
    with_next = w_in is not None
    yspec = pl.BlockSpec((tm, GROUP_W), lambda t: (t, 0))
    xspec = pl.BlockSpec((tm, D_MODEL), lambda t: (t, 0))
    gspec = pl.BlockSpec((None, 1, D_MODEL), lambda t: (layer, 0, 0))

    def resident(shape):
        return pl.BlockSpec(shape, lambda t: (0, 0), pipeline_mode=pl.Buffered(1))

    in_specs = [yspec, yspec, yspec, yspec, xspec, resident((D_MODEL, D_MODEL)), gspec, gspec,
                resident((D_MODEL, D_FF)), resident((D_FF, D_MODEL)), gspec]
    operands = [ya, yb, yc, yd, x2, wo, gpost, gpre, w1, w2, gmlp]
    out_specs = [xspec]
    out_shape = [jax.ShapeDtypeStruct((tok, D_MODEL), F32)]
    if with_next:
        cols = ATT_COL1 - ATT_COL0
        in_specs += [pl.BlockSpec((None, 1, D_MODEL), lambda t: (layer + 1, 0, 0))]
        in_specs += [pl.BlockSpec((None, D_MODEL, GROUP_W), lambda t, j=j: (layer + 1, 0, ATT_COL0 // GROUP_W + j),
                                  pipeline_mode=pl.Buffered(1)) for j in range(3)]
        operands += [gnext, w_in, w_in, w_in]
        out_specs += [pl.BlockSpec((tm, cols), lambda t: (t, 0)), xspec]
        out_shape += [jax.ShapeDtypeStruct((tok, cols), F32), jax.ShapeDtypeStruct((tok, D_MODEL), BF16)]
    outs = pl.pallas_call(
        functools.partial(_out_mlp_kernel, with_next),
        grid=(tok // tm,),
        in_specs=in_specs,
        out_specs=out_specs,
        out_shape=out_shape,
        compiler_params=pltpu.CompilerParams(
            dimension_semantics=("parallel",), vmem_limit_bytes=VMEM_LIMIT_TAIL),
        name="out_mlp",
    )(*operands)
    return outs if with_next else outs[0]


def _block_diag_dense(w):
    depth, nblk, n, _ = w.shape
    eye = jnp.eye(nblk, dtype=w.dtype)
    return jnp.einsum("dgij,gh->dgihj", w, eye).reshape(depth, nblk * n, nblk * n)


def kernel(x, pre_mix_g, w_in, conf_dw_w, conf_dw_b, conf_gn_g, conf_gn_b, sconv_w, lru_conv_w, lru_conv_b,
           lru_wa, lru_ba, lru_wx, lru_bx, lru_lam, w_out, post_mix_g, pre_mlp_g, mlp_w1, mlp_w2, post_mlp_g):
    bsz, seq, dm = x.shape
    depth = w_in.shape[0]
    assert dm == D_MODEL and seq % MOBA_BLOCK == 0 and (bsz * seq) % TOKEN_TILE == 0
    assert seq // MOBA_BLOCK > MOBA_TOP_K
    assert D_MODEL % (16 * bsz) == 0

    row = lambda a: a.reshape(depth, 1, a.shape[-1])
    taps = lambda w: jnp.broadcast_to(w[:, :, None, :], w.shape[:2] + (8, w.shape[-1]))
    group_of = jnp.arange(GROUP_W) // (GROUP_W // CONF_GROUPS)
    params = {
        "dww": taps(conf_dw_w), "dwb": row(conf_dw_b), "gng": row(conf_gn_g), "gnb": row(conf_gn_b),
        "scw": taps(sconv_w), "lcw": taps(lru_conv_w), "lcb": row(lru_conv_b),
        "wax": jnp.concatenate([_block_diag_dense(lru_wa), _block_diag_dense(lru_wx)], axis=-1).astype(BF16),
        "bax": row(jnp.concatenate([lru_ba, lru_bx], axis=-1)),
        "lam": row(lru_lam),
        "gm": ((group_of[:, None] == group_of[None, :]).astype(F32) / (GROUP_W // CONF_GROUPS)).astype(BF16),
    }
    w_in_b = w_in.astype(BF16)
    g_pre_mix, g_post_mix = row(pre_mix_g), row(post_mix_g)
    g_pre_mlp, g_post_mlp = row(pre_mlp_g), row(post_mlp_g)

    x2 = x.reshape(bsz * seq, dm)
    u_att, h_seg = _in_proj(x2, g_pre_mix, w_in_b, 0)
    for layer in range(depth):
        ya, yb, yd = _mixers(h_seg.reshape(bsz, seq, dm), w_in_b, params, layer)
        yc, w_out_b, w1_b, w2_b = _attention(u_att.reshape(bsz, seq, ATT_COL1 - ATT_COL0), w_out, mlp_w1, mlp_w2, layer)
        flat = lambda a: a.reshape(bsz * seq, GROUP_W)
        args = (flat(ya), flat(yb), flat(yc), flat(yd), x2, w_out_b, g_post_mix, g_pre_mlp, w1_b, w2_b, g_post_mlp)
        if layer + 1 < depth:
            x2, u_att, h_seg = _out_mlp(*args, layer, w_in=w_in_b, gnext=g_pre_mix)
        else:
            x2 = _out_mlp(*args, layer)
    return x2.reshape(bsz, seq, dm)
```

```python
import functools
import math

import jax
import jax.numpy as jnp
from jax import lax
from jax.experimental import pallas as pl
from jax.experimental.pallas import tpu as pltpu

F32 = jnp.float32
BF16 = jnp.bfloat16

D_MODEL = 1024
GROUP_W = 256
CONF_KERNEL = 31
CONF_GROUPS = 4
GN_EPS = 1e-5
SCONV_KERNEL = 3
ATT_HEADS = 4
HEAD_DIM = GROUP_W // ATT_HEADS
MOBA_BLOCK = 256
MOBA_TOP_K = 3
LRU_BLOCKS = 4
LRU_CONV = 4
LRU_C = 8.0
D_FF = 4 * D_MODEL
RMS_EPS = 1e-6
IN_COLS = 10 * GROUP_W
ATT_COL0, ATT_COL1 = 5 * GROUP_W, 8 * GROUP_W
MIX_COLS = IN_COLS - (ATT_COL1 - ATT_COL0)

TOKEN_TILE = 1024
UNIT = MOBA_BLOCK
MIX_ROWS = 2
SEG = UNIT // 8
FF_CHUNK = 1024
V_ROWS = HEAD_DIM + 16
VMEM_LIMIT = 56 * 1024 * 1024
VMEM_LIMIT_TAIL = 62 * 1024 * 1024

NEG_INF = float("-inf")


def _rms(x, g):
    return x * lax.rsqrt(jnp.mean(x * x, axis=-1, keepdims=True) + RMS_EPS) * g


def _dot(a, b):
    return jnp.dot(a, b, preferred_element_type=F32)


def _segment_permutations():
    row = lax.broadcasted_iota(jnp.int32, (UNIT, UNIT), 0)
    col = lax.broadcasted_iota(jnp.int32, (UNIT, UNIT), 1)
    to_seg = jnp.where(col == SEG * (row & 7) + (row >> 3), 1.0, 0.0).astype(BF16)
    from_seg = jnp.where(row == SEG * (col & 7) + (col >> 3), 1.0, 0.0).astype(BF16)
    return to_seg, from_seg


def _in_proj_tile(x, g_ref, w_refs, u_ref, hseg_ref, row0):
    rows = x.shape[0]
    h = _rms(x, g_ref[...]).astype(BF16)
    for j, w_ref in enumerate(w_refs):
        u_ref[row0:row0 + rows, j * GROUP_W:(j + 1) * GROUP_W] = _dot(h, w_ref[...])
    to_seg, _ = _segment_permutations()
    for r in range(0, rows, UNIT):
        hseg_ref[row0 + r:row0 + r + UNIT, :] = _dot(to_seg, h[r:r + UNIT, :]).astype(BF16)


def _in_proj_kernel(x_ref, g_ref, wq_ref, wk_ref, wv_ref, u_ref, hseg_ref):
    _in_proj_tile(x_ref[...], g_ref, (wq_ref, wk_ref, wv_ref), u_ref, hseg_ref, 0)


def _in_proj(x2, g, w, layer):
    tok = x2.shape[0]
    cols = ATT_COL1 - ATT_COL0

    def wcol(j):
        return pl.BlockSpec((None, D_MODEL, GROUP_W), lambda t: (layer, 0, ATT_COL0 // GROUP_W + j))

    xspec = pl.BlockSpec((TOKEN_TILE, D_MODEL), lambda t: (t, 0))
    return pl.pallas_call(
        _in_proj_kernel,
        grid=(tok // TOKEN_TILE,),
        in_specs=[xspec, pl.BlockSpec((None, 1, D_MODEL), lambda t: (layer, 0, 0)), wcol(0), wcol(1), wcol(2)],
        out_specs=[pl.BlockSpec((TOKEN_TILE, cols), lambda t: (t, 0)), xspec],
        out_shape=[jax.ShapeDtypeStruct((tok, cols), F32), jax.ShapeDtypeStruct((tok, D_MODEL), BF16)],
        compiler_params=pltpu.CompilerParams(
            dimension_semantics=("parallel",), vmem_limit_bytes=VMEM_LIMIT),
        name="in_proj",
    )(x2, g, w, w, w)


def _group_mean(v, gm):
    hi = v.astype(BF16)
    lo = (v - hi.astype(F32)).astype(BF16)
    return _dot(hi, gm) + _dot(lo, gm)


def _seg_conv(z, e_ref, w_ref, bias, width, out_ref, fresh):
    sub = lax.broadcasted_iota(jnp.int32, (8, GROUP_W), 0)
    for a in range(SEG - (width - 1), SEG):
        cur = z[8 * a:8 * a + 8, :]
        prev = jnp.where(fresh, 0.0, e_ref[UNIT + 8 * a:UNIT + 8 * a + 8, :])
        e_ref[8 * a:8 * a + 8, :] = pltpu.roll(jnp.where(sub == 7, prev, cur), 1, axis=0)
    e_ref[UNIT:2 * UNIT, :] = z
    for a0 in range(0, SEG, 8):
        accs = [bias] * 8
        for j in range(width):
            w_j = w_ref[width - 1 - j]
            for a in range(a0, a0 + 8):
                src = UNIT + 8 * (a - j) if a >= j else 8 * (a - j + SEG)
                term = w_j * e_ref[src:src + 8, :]
                accs[a - a0] = term if accs[a - a0] is None else accs[a - a0] + term
        for a in range(a0, a0 + 8):
            out_ref[8 * a:8 * a + 8, :] = accs[a - a0]


def _mixers_kernel(h_ref, win_lo_ref, win_hi_ref,
                   dww_ref, dwb_ref, gng_ref, gnb_ref, scw_ref, lcw_ref, lcb_ref,
                   wax_ref, bax_ref, lam_ref, gm_ref,
                   ya_ref, yb_ref, yd_ref,
                   u_even, u_odd, ea, eb, ed, cbuf, abuf, bbuf, hloc, acum, endb, hcar):
    _, from_seg = _segment_permutations()
    n_units = h_ref.shape[0] // UNIT
    units_per_seq = n_units // MIX_ROWS
    for e in (ea, eb, ed):
        e[UNIT:2 * UNIT, :] = jnp.zeros((UNIT, GROUP_W), F32)
    hcar[...] = jnp.zeros_like(hcar)

    sub = lax.broadcasted_iota(jnp.int32, (8, GROUP_W), 0)
    bias_a = jnp.broadcast_to(dwb_ref[...], (8, GROUP_W))
    bias_d = jnp.broadcast_to(lcb_ref[...], (8, GROUP_W))
    lam = lam_ref[...]
    neg_c_softplus = -LRU_C * (jnp.maximum(-lam, 0.0) + jnp.log1p(jnp.exp(-jnp.abs(lam))))
    gm = gm_ref[...]

    def projection(u, u_ref):
        rows = pl.ds(pl.multiple_of(u * UNIT, UNIT), UNIT)
        step = 2 * GROUP_W

        def lo(c):
            def run():
                w = min(step, ATT_COL0 - c)
                u_ref[:, c:c + w] = _dot(h_ref[rows, :], win_lo_ref[:, c:c + w])
            return run

        def hi():
            u_ref[:, ATT_COL0:MIX_COLS] = _dot(h_ref[rows, :], win_hi_ref[...])

        return [lo(c) for c in range(0, ATT_COL0, step)] + [hi]

    def col(u_ref, j):
        return u_ref[:, j * GROUP_W:(j + 1) * GROUP_W]

    def mix(u, u_ref, ahead):
        rows = pl.ds(pl.multiple_of(u * UNIT, UNIT), UNIT)
        fresh = u % units_per_seq == 0
        ahead = list(ahead)
        ahead.pop(0)()

        _seg_conv(col(u_ref, 0) * jax.nn.sigmoid(col(u_ref, 1)), ea, dww_ref, bias_a, CONF_KERNEL, cbuf, fresh)
        c = cbuf[...]
        d = c - _group_mean(c, gm)
        var = _group_mean(d * d, gm)
        ahead.pop(0)()
        yn = d * lax.rsqrt(var + GN_EPS) * gng_ref[...] + gnb_ref[...]
        ya_seg = (yn * jax.nn.sigmoid(yn)).astype(BF16)

        _seg_conv(col(u_ref, 3) * col(u_ref, 4), eb, scw_ref, None, SCONV_KERNEL, cbuf, fresh)
        yb_seg = (col(u_ref, 2) * cbuf[...]).astype(BF16)

        _seg_conv(col(u_ref, 5), ed, lcw_ref, bias_d, LRU_CONV, cbuf, fresh)
        xc = cbuf[...]
        gates = _dot(xc.astype(BF16), wax_ref[...]) + bax_ref[...]
        while ahead:
            ahead.pop(0)()
        r = jax.nn.sigmoid(gates[:, :GROUP_W])
        i = jax.nn.sigmoid(gates[:, GROUP_W:])
        log_a = neg_c_softplus * r
        abuf[...] = jnp.exp(log_a)
        th = jnp.tanh(log_a)
        bbuf[...] = jnp.sqrt(-2.0 * th / (1.0 - th)) * (i * xc)
        h_run = bbuf[0:8, :]
        a_run = abuf[0:8, :]
        hloc[0:8, :] = h_run
        acum[0:8, :] = a_run
        for s in range(1, SEG):
            a_s = abuf[8 * s:8 * s + 8, :]
            h_run = a_s * h_run + bbuf[8 * s:8 * s + 8, :]
            a_run = a_s * a_run
            hloc[8 * s:8 * s + 8, :] = h_run
            acum[8 * s:8 * s + 8, :] = a_run
        dist = 1
        while dist < 8:
            keep = sub >= dist
            h_run = jnp.where(keep, a_run * pltpu.roll(h_run, dist, axis=0) + h_run, h_run)
            a_run = jnp.where(keep, a_run * pltpu.roll(a_run, dist, axis=0), a_run)
            dist *= 2
        h_in = jnp.where(fresh, 0.0, hcar[...])
        seg_end = a_run * h_in + h_run
        endb[...] = seg_end
        seg_in = jnp.where(sub == 0, h_in, pltpu.roll(seg_end, 1, axis=0))
        hcar[...] = jnp.broadcast_to(endb[7:8, :], (8, GROUP_W))
        for s in range(SEG):
            hloc[8 * s:8 * s + 8, :] = hloc[8 * s:8 * s + 8, :] + acum[8 * s:8 * s + 8, :] * seg_in
        g = col(u_ref, 6)
        gelu = 0.5 * g * (1.0 + jnp.tanh(math.sqrt(2.0 / math.pi) * (g + 0.044715 * (g * g * g))))
        yd_seg = (hloc[...] * gelu).astype(BF16)
        for y_ref, y_seg in ((ya_ref, ya_seg), (yb_ref, yb_seg), (yd_ref, yd_seg)):
            y_ref[rows, :] = _dot(from_seg, y_seg).astype(y_ref.dtype)

    for run in projection(0, u_even):
        run()

    def two_units(jj, carry):
        mix(2 * jj, u_even, projection(2 * jj + 1, u_odd))
        mix(2 * jj + 1, u_odd, projection(jnp.minimum(2 * jj + 2, n_units - 1), u_even))
        return carry

    lax.fori_loop(0, n_units // 2, two_units, 0)


def _mixers(h3, w_in, p, layer):
    bsz, seq, dm = h3.shape
    assert (seq // UNIT) % 2 == 0 and bsz % MIX_ROWS == 0
    steps, rows = bsz // MIX_ROWS, MIX_ROWS * seq
    h3 = h3.reshape(steps, rows, dm)
    hi_cols = IN_COLS - ATT_COL1
    assert ATT_COL1 % hi_cols == 0

    def par(shape):
        return pl.BlockSpec((None,) + shape, lambda b: (layer,) + (0,) * len(shape))

    out_spec = pl.BlockSpec((None, rows, GROUP_W), lambda b: (b, 0, 0))
    out_sds = jax.ShapeDtypeStruct((steps, rows, GROUP_W), BF16)
    outs = pl.pallas_call(
        _mixers_kernel,
        grid=(steps,),
        in_specs=[pl.BlockSpec((None, rows, dm), lambda b: (b, 0, 0)),
                  pl.BlockSpec((None, dm, ATT_COL0), lambda b: (layer, 0, 0), pipeline_mode=pl.Buffered(1)),
                  pl.BlockSpec((None, dm, hi_cols), lambda b: (layer, 0, ATT_COL1 // hi_cols),
                               pipeline_mode=pl.Buffered(1)),
                  par((CONF_KERNEL, 8, GROUP_W)), par((1, GROUP_W)), par((1, GROUP_W)), par((1, GROUP_W)),
                  par((SCONV_KERNEL, 8, GROUP_W)), par((LRU_CONV, 8, GROUP_W)), par((1, GROUP_W)),
                  par((GROUP_W, 2 * GROUP_W)), par((1, 2 * GROUP_W)), par((1, GROUP_W)),
                  pl.BlockSpec((GROUP_W, GROUP_W), lambda b: (0, 0))],
        out_specs=[out_spec, out_spec, out_spec],
        out_shape=[out_sds, out_sds, out_sds],
        scratch_shapes=(
            [pltpu.VMEM((UNIT, MIX_COLS), F32)] * 2
            + [pltpu.VMEM((2 * UNIT, GROUP_W), F32)] * 3
            + [pltpu.VMEM((UNIT, GROUP_W), F32)] * 5
            + [pltpu.VMEM((8, GROUP_W), F32)] * 2
        ),
        compiler_params=pltpu.CompilerParams(
            dimension_semantics=("parallel",), vmem_limit_bytes=VMEM_LIMIT),
        name="mixers",
    )(h3, w_in, w_in,
      p["dww"], p["dwb"], p["gng"], p["gnb"], p["scw"], p["lcw"], p["lcb"],
      p["wax"], p["bax"], p["lam"], p["gm"])
    return [y.reshape(bsz, seq, GROUP_W) for y in outs]


_NT = (((1,), (1,)), ((), ()))


def _attn_kernel(q_ref, k_ref, v_ref, wo_ref, w1_ref, w2_ref, o_ref, wo_b_ref, w1_b_ref, w2_b_ref,
                 kb, vt, kmm, gate_s, cmask, pos_q, rbias, qm, s_buf, acc):
    wo_b_ref[...] = wo_ref[...].astype(BF16)
    w1_b_ref[...] = w1_ref[...].astype(BF16)
    w2_b_ref[...] = w2_ref[...].astype(BF16)
    nb = kb.shape[0]
    blk = MOBA_BLOCK
    heads = range(ATT_HEADS)
    log2e = math.log2(math.e)
    slopes2 = [log2e * 2.0 ** (-8.0 * (h + 1) / ATT_HEADS) for h in heads]
    lane_head = lax.broadcasted_iota(jnp.int32, (1, GROUP_W), 1) // HEAD_DIM
    pos = lambda rho: rho

    pos_s = pos(lax.broadcasted_iota(jnp.int32, (1, blk), 1))
    rel = (pos_s - (blk - 1)).astype(F32)
    first_row = lax.broadcasted_iota(jnp.int32, (V_ROWS - HEAD_DIM, blk), 0) == 0
    for n in range(nb):
        kblk = k_ref[n * blk:(n + 1) * blk, :]
        kb[n] = kblk.astype(BF16)
        v_t = v_ref[n * blk:(n + 1) * blk, :].T
        kmean = jnp.mean(kblk, axis=0, keepdims=True)
        for h in heads:
            kmm[h * nb + n:h * nb + n + 1, :] = jnp.where(lane_head == h, kmean, 0.0)
            f = jnp.exp2(slopes2[h] * rel)
            vt[n, h * V_ROWS:h * V_ROWS + HEAD_DIM, :] = (v_t[h * HEAD_DIM:(h + 1) * HEAD_DIM, :] * f).astype(BF16)
            vt[n, h * V_ROWS + HEAD_DIM:(h + 1) * V_ROWS, :] = jnp.where(first_row, f, 0.0).astype(BF16)
    t_l = pos(lax.broadcasted_iota(jnp.int32, (blk, blk), 1))
    s_l = pos(lax.broadcasted_iota(jnp.int32, (blk, blk), 0))
    cmask[...] = jnp.where(t_l >= s_l, 0.0, NEG_INF)
    pos_q[...] = pos(lax.broadcasted_iota(jnp.int32, (nb, blk), 1)).astype(F32)
    blk_id = lax.broadcasted_iota(jnp.int32, (nb, blk), 0)

    def prepare(i):
        par = i % 2
        q = q_ref[i * blk:(i + 1) * blk, :] * (HEAD_DIM ** -0.5 * log2e)
        for h in heads:
            qm[par, h] = jnp.where(lane_head == h, q, 0.0).astype(BF16)
        gate_s[...] = lax.dot_general(kmm[...], q, _NT, precision=lax.Precision.HIGHEST,
                                      preferred_element_type=F32)
        dist = (i - blk_id).astype(F32) * blk + (pos_q[...] - (blk - 1))
        for h in heads:
            g = gate_s[h * nb:(h + 1) * nb, :]
            rank = jnp.zeros((nb, blk), jnp.int32)
            for m in range(i):
                gm = gate_s[h * nb + m:h * nb + m + 1, :]
                rank = rank + jnp.where((gm > g) | ((gm == g) & (m < blk_id)), 1, 0)
            visible = (blk_id == i) | ((blk_id < i) & (rank < MOBA_TOP_K))
            rbias[par, h] = jnp.where(visible, -slopes2[h] * dist, NEG_INF)

    def score(k, i, n, m_run):
        out = []
        for h in heads:
            s_t = lax.dot_general(kb[n], qm[i % 2, h], _NT, preferred_element_type=F32)
            if n == i:
                s_t = s_t + cmask[...]
            s_buf[k % 2, h] = s_t
            m_blk = jnp.max(s_t, axis=0, keepdims=True) + rbias[i % 2, h, n:n + 1, :]
            out.append(m_blk if n == i else jnp.maximum(m_run[h], m_blk))
        return out

    def fold(k, i, n, m_now, m_acc):
        for h in heads:
            p = jnp.exp2(s_buf[k % 2, h] - (m_now[h] - rbias[i % 2, h, n:n + 1, :]))
            pv = _dot(vt[n, h * V_ROWS:(h + 1) * V_ROWS, :], p.astype(BF16))
            acc[i % 2, h] = pv if n == i else jnp.exp2(m_acc[h] - m_now[h]) * acc[i % 2, h] + pv

    def finalize(i):
        out_t = jnp.concatenate(
            [acc[i % 2, h, 0:HEAD_DIM, :] / acc[i % 2, h, HEAD_DIM:HEAD_DIM + 1, :] for h in heads], axis=0)
        o_ref[i * blk:(i + 1) * blk, :] = out_t.T.astype(o_ref.dtype)

    units = [(i, n) for i in range(nb) for n in [i] + list(range(i))]
    prepare(0)
    m_run = score(0, 0, 0, None)
    prepare(1)
    m_fold = None
    for k in range(1, len(units) + 1):
        pi, pn = units[k - 1]
        m_prev = m_run
        if k < len(units):
            i, n = units[k]
            m_run = score(k, i, n, m_run)
        fold(k - 1, pi, pn, m_prev, m_fold)
        m_fold = m_prev
        if pn == pi - 1 or pi == 0:
            finalize(pi)
        if k < len(units) and n == i and i + 1 < nb:
            prepare(i + 1)


def _attention(u3, w_out, w1, w2, layer):
    bsz, seq, _ = u3.shape
    nb = seq // MOBA_BLOCK
    blk = MOBA_BLOCK

    def ucol(j):
        return pl.BlockSpec((None, seq, GROUP_W), lambda b: (b, 0, j))

    def slab_in(w):
        return pl.BlockSpec((None, w.shape[1] // bsz, w.shape[2]), lambda b: (layer, b, 0))

    def slab_out(w):
        return pl.BlockSpec((w.shape[1] // bsz, w.shape[2]), lambda b: (b, 0))

    weights = (w_out, w1, w2)
    return pl.pallas_call(
        _attn_kernel,
        grid=(bsz,),
        in_specs=[ucol(0), ucol(1), ucol(2)] + [slab_in(w) for w in weights],
        out_specs=[pl.BlockSpec((None, seq, GROUP_W), lambda b: (b, 0, 0))] + [slab_out(w) for w in weights],
        out_shape=[jax.ShapeDtypeStruct((bsz, seq, GROUP_W), BF16)]
        + [jax.ShapeDtypeStruct(w.shape[1:], BF16) for w in weights],
        scratch_shapes=[
            pltpu.VMEM((nb, blk, GROUP_W), BF16),
            pltpu.VMEM((nb, ATT_HEADS * V_ROWS, blk), BF16),
            pltpu.VMEM((ATT_HEADS * nb, GROUP_W), F32),
            pltpu.VMEM((ATT_HEADS * nb, blk), F32),
            pltpu.VMEM((blk, blk), F32),
            pltpu.VMEM((nb, blk), F32),
            pltpu.VMEM((2, ATT_HEADS, nb, blk), F32),
            pltpu.VMEM((2, ATT_HEADS, blk, GROUP_W), BF16),
            pltpu.VMEM((2, ATT_HEADS, blk, blk), F32),
            pltpu.VMEM((2, ATT_HEADS, V_ROWS, blk), F32),
        ],
        compiler_params=pltpu.CompilerParams(
            dimension_semantics=("parallel",), vmem_limit_bytes=VMEM_LIMIT),
        name="moba_attention",
    )(u3, u3, u3, w_out, w1, w2)


def _out_mlp_kernel(with_next, ya_ref, yb_ref, yc_ref, yd_ref, x_ref, wo_ref, gpost_ref, gpre_ref, w1_ref, w2_ref,
                    gmlp_ref, *rest):
    if with_next:
        gnext_ref, wq_ref, wk_ref, wv_ref, xo_ref, u_ref, hseg_ref = rest
    else:
        (xo_ref,) = rest
    half = x_ref.shape[0] // 2
    halves = [slice(0, half), slice(half, 2 * half)]

    def out_projection(rows):
        y = None
        for j, y_ref in enumerate((ya_ref, yb_ref, yc_ref, yd_ref)):
            term = _dot(y_ref[rows, :], wo_ref[j * GROUP_W:(j + 1) * GROUP_W, :])
            y = term if y is None else y + term
        return y

    def mlp(h):
        m = None
        for c in range(0, D_FF, FF_CHUNK):
            t = jnp.maximum(_dot(h, w1_ref[:, c:c + FF_CHUNK]), 0.0)
            term = _dot((t * t).astype(BF16), w2_ref[c:c + FF_CHUNK, :])
            m = term if m is None else m + term
        return m

    ys = [out_projection(rows) for rows in halves]
    for rows, y in zip(halves, ys):
        xo_ref[rows, :] = x_ref[rows, :] + _rms(y, gpost_ref[...])
        m = mlp(_rms(xo_ref[rows, :], gpre_ref[...]).astype(BF16))
        xo_ref[rows, :] = xo_ref[rows, :] + _rms(m, gmlp_ref[...])
    if with_next:
        for rows in halves:
            _in_proj_tile(xo_ref[rows, :], gnext_ref, (wq_ref, wk_ref, wv_ref), u_ref, hseg_ref, rows.start)


def _out_mlp(ya, yb, yc, yd, x2, wo, gpost, gpre, w1, w2, gmlp, layer, w_in=None, gnext=None):
    tok = x2.shape[0]
    tm = TOKEN_TILE
    with_next = w_in is not None
    yspec = pl.BlockSpec((tm, GROUP_W), lambda t: (t, 0))
    xspec = pl.BlockSpec((tm, D_MODEL), lambda t: (t, 0))
    gspec = pl.BlockSpec((None, 1, D_MODEL), lambda t: (layer, 0, 0))

    def resident(shape):
        return pl.BlockSpec(shape, lambda t: (0, 0), pipeline_mode=pl.Buffered(1))

    in_specs = [yspec, yspec, yspec, yspec, xspec, resident((D_MODEL, D_MODEL)), gspec, gspec,
                resident((D_MODEL, D_FF)), resident((D_FF, D_MODEL)), gspec]
    operands = [ya, yb, yc, yd, x2, wo, gpost, gpre, w1, w2, gmlp]
    out_specs = [xspec]
    out_shape = [jax.ShapeDtypeStruct((tok, D_MODEL), F32)]
    if with_next:
        cols = ATT_COL1 - ATT_COL0
        in_specs += [pl.BlockSpec((None, 1, D_MODEL), lambda t: (layer + 1, 0, 0))]
        in_specs += [pl.BlockSpec((None, D_MODEL, GROUP_W), lambda t, j=j: (layer + 1, 0, ATT_COL0 // GROUP_W + j),
                                  pipeline_mode=pl.Buffered(1)) for j in range(3)]
        operands += [gnext, w_in, w_in, w_in]
        out_specs += [pl.BlockSpec((tm, cols), lambda t: (t, 0)), xspec]
        out_shape += [jax.ShapeDtypeStruct((tok, cols), F32), jax.ShapeDtypeStruct((tok, D_MODEL), BF16)]
    outs = pl.pallas_call(
        functools.partial(_out_mlp_kernel, with_next),
        grid=(tok // tm,),
        in_specs=in_specs,
        out_specs=out_specs,
        out_shape=out_shape,
        compiler_params=pltpu.CompilerParams(
            dimension_semantics=("parallel",), vmem_limit_bytes=VMEM_LIMIT_TAIL),
        name="out_mlp",
    )(*operands)
    return outs if with_next else outs[0]


def _block_diag_dense(w):
    depth, nblk, n, _ = w.shape
    eye = jnp.eye(nblk, dtype=w.dtype)
    return jnp.einsum("dgij,gh->dgihj", w, eye).reshape(depth, nblk * n, nblk * n)


def kernel(x, pre_mix_g, w_in, conf_dw_w, conf_dw_b, conf_gn_g, conf_gn_b, sconv_w, lru_conv_w, lru_conv_b,
           lru_wa, lru_ba, lru_wx, lru_bx, lru_lam, w_out, post_mix_g, pre_mlp_g, mlp_w1, mlp_w2, post_mlp_g):
    bsz, seq, dm = x.shape
    depth = w_in.shape[0]
    assert dm == D_MODEL and seq % MOBA_BLOCK == 0 and (bsz * seq) % TOKEN_TILE == 0
    assert seq // MOBA_BLOCK > MOBA_TOP_K
    assert D_MODEL % (16 * bsz) == 0

    row = lambda a: a.reshape(depth, 1, a.shape[-1])
    taps = lambda w: jnp.broadcast_to(w[:, :, None, :], w.shape[:2] + (8, w.shape[-1]))
    group_of = jnp.arange(GROUP_W) // (GROUP_W // CONF_GROUPS)
    params = {
        "dww": taps(conf_dw_w), "dwb": row(conf_dw_b), "gng": row(conf_gn_g), "gnb": row(conf_gn_b),
        "scw": taps(sconv_w), "lcw": taps(lru_conv_w), "lcb": row(lru_conv_b),
        "wax": jnp.concatenate([_block_diag_dense(lru_wa), _block_diag_dense(lru_wx)], axis=-1).astype(BF16),
        "bax": row(jnp.concatenate([lru_ba, lru_bx], axis=-1)),
        "lam": row(lru_lam),
        "gm": ((group_of[:, None] == group_of[None, :]).astype(F32) / (GROUP_W // CONF_GROUPS)).astype(BF16),
    }
    w_in_b = w_in.astype(BF16)
    g_pre_mix, g_post_mix = row(pre_mix_g), row(post_mix_g)
    g_pre_mlp, g_post_mlp = row(pre_mlp_g), row(post_mlp_g)

    x2 = x.reshape(bsz * seq, dm)
    u_att, h_seg = _in_proj(x2, g_pre_mix, w_in_b, 0)
    for layer in range(depth):
        ya, yb, yd = _mixers(h_seg.reshape(bsz, seq, dm), w_in_b, params, layer)
        yc, w_out_b, w1_b, w2_b = _attention(u_att.reshape(bsz, seq, ATT_COL1 - ATT_COL0), w_out, mlp_w1, mlp_w2, layer)
        flat = lambda a: a.reshape(bsz * seq, GROUP_W)
        args = (flat(ya), flat(yb), flat(yc), flat(yd), x2, w_out_b, g_post_mix, g_pre_mlp, w1_b, w2_b, g_post_mlp)
        if layer + 1 < depth:
            x2, u_att, h_seg = _out_mlp(*args, layer, w_in=w_in_b, gnext=g_pre_mix)
        else:
            x2 = _out_mlp(*args, layer)
    return x2.reshape(bsz, seq, dm)
```

```python
import functools
import math

import jax
import jax.numpy as jnp
from jax import lax
from jax.experimental import pallas as pl
from jax.experimental.pallas import tpu as pltpu

F32 = jnp.float32
BF16 = jnp.bfloat16

D_MODEL = 1024
GROUP_W = 256
CONF_KERNEL = 31
CONF_GROUPS = 4
GN_EPS = 1e-5
SCONV_KERNEL = 3
ATT_HEADS = 4
HEAD_DIM = GROUP_W // ATT_HEADS
MOBA_BLOCK = 256
MOBA_TOP_K = 3
LRU_BLOCKS = 4
LRU_CONV = 4
LRU_C = 8.0
D_FF = 4 * D_MODEL
RMS_EPS = 1e-6
IN_COLS = 10 * GROUP_W
ATT_COL0, ATT_COL1 = 5 * GROUP_W, 8 * GROUP_W
MIX_COLS = IN_COLS - (ATT_COL1 - ATT_COL0)

TOKEN_TILE = 1024
UNIT = MOBA_BLOCK
MIX_ROWS = 2
SEG = UNIT // 8
FF_CHUNK = 1024
V_ROWS = HEAD_DIM + 16
VMEM_LIMIT = 56 * 1024 * 1024
VMEM_LIMIT_TAIL = 62 * 1024 * 1024

NEG_INF = float("-inf")


def _rms(x, g):
    return x * lax.rsqrt(jnp.mean(x * x, axis=-1, keepdims=True) + RMS_EPS) * g


def _dot(a, b):
    return jnp.dot(a, b, preferred_element_type=F32)


def _segment_permutations():
    row = lax.broadcasted_iota(jnp.int32, (UNIT, UNIT), 0)
    col = lax.broadcasted_iota(jnp.int32, (UNIT, UNIT), 1)
    to_seg = jnp.where(col == SEG * (row & 7) + (row >> 3), 1.0, 0.0).astype(BF16)
    from_seg = jnp.where(row == SEG * (col & 7) + (col >> 3), 1.0, 0.0).astype(BF16)
    return to_seg, from_seg


def _in_proj_tile(x, g_ref, w_refs, u_ref, hseg_ref, row0):
    rows = x.shape[0]
    h = _rms(x, g_ref[...]).astype(BF16)
    for j, w_ref in enumerate(w_refs):
        u_ref[row0:row0 + rows, j * GROUP_W:(j + 1) * GROUP_W] = _dot(h, w_ref[...])
    to_seg, _ = _segment_permutations()
    for r in range(0, rows, UNIT):
        hseg_ref[row0 + r:row0 + r + UNIT, :] = _dot(to_seg, h[r:r + UNIT, :]).astype(BF16)


def _in_proj_kernel(x_ref, g_ref, wq_ref, wk_ref, wv_ref, u_ref, hseg_ref):
    _in_proj_tile(x_ref[...], g_ref, (wq_ref, wk_ref, wv_ref), u_ref, hseg_ref, 0)


def _in_proj(x2, g, w, layer):
    tok = x2.shape[0]
    cols = ATT_COL1 - ATT_COL0

    def wcol(j):
        return pl.BlockSpec((D_MODEL, GROUP_W), lambda t: (0, ATT_COL0 // GROUP_W + j))

    xspec = pl.BlockSpec((TOKEN_TILE, D_MODEL), lambda t: (t, 0))
    return pl.pallas_call(
        _in_proj_kernel,
        grid=(tok // TOKEN_TILE,),
        in_specs=[xspec, pl.BlockSpec((None, 1, D_MODEL), lambda t: (layer, 0, 0)), wcol(0), wcol(1), wcol(2)],
        out_specs=[pl.BlockSpec((TOKEN_TILE, cols), lambda t: (t, 0)), xspec],
        out_shape=[jax.ShapeDtypeStruct((tok, cols), F32), jax.ShapeDtypeStruct((tok, D_MODEL), BF16)],
        compiler_params=pltpu.CompilerParams(
            dimension_semantics=("parallel",), vmem_limit_bytes=VMEM_LIMIT),
        name="in_proj",
    )(x2, g, w, w, w)


def _group_mean(v, gm):
    hi = v.astype(BF16)
    lo = (v - hi.astype(F32)).astype(BF16)
    return _dot(hi, gm) + _dot(lo, gm)


def _seg_conv(z, e_ref, w_ref, bias, width, out_ref, fresh):
    sub = lax.broadcasted_iota(jnp.int32, (8, GROUP_W), 0)
    for a in range(SEG - (width - 1), SEG):
        cur = z[8 * a:8 * a + 8, :]
        prev = jnp.where(fresh, 0.0, e_ref[UNIT + 8 * a:UNIT + 8 * a + 8, :])
        e_ref[8 * a:8 * a + 8, :] = pltpu.roll(jnp.where(sub == 7, prev, cur), 1, axis=0)
    e_ref[UNIT:2 * UNIT, :] = z
    for a0 in range(0, SEG, 8):
        accs = [bias] * 8
        for j in range(width):
            w_j = w_ref[width - 1 - j]
            for a in range(a0, a0 + 8):
                src = UNIT + 8 * (a - j) if a >= j else 8 * (a - j + SEG)
                term = w_j * e_ref[src:src + 8, :]
                accs[a - a0] = term if accs[a - a0] is None else accs[a - a0] + term
        for a in range(a0, a0 + 8):
            out_ref[8 * a:8 * a + 8, :] = accs[a - a0]


def _mixers_kernel(h_ref, win_lo_ref, win_hi_ref,
                   dww_ref, dwb_ref, gng_ref, gnb_ref, scw_ref, lcw_ref, lcb_ref,
                   wax_ref, bax_ref, lam_ref, gm_ref,
                   ya_ref, yb_ref, yd_ref,
                   u_even, u_odd, ea, eb, ed, cbuf, abuf, bbuf, hloc, acum, endb, hcar):
    _, from_seg = _segment_permutations()
    n_units = h_ref.shape[0] // UNIT
    units_per_seq = n_units // MIX_ROWS
    for e in (ea, eb, ed):
        e[UNIT:2 * UNIT, :] = jnp.zeros((UNIT, GROUP_W), F32)
    hcar[...] = jnp.zeros_like(hcar)

    sub = lax.broadcasted_iota(jnp.int32, (8, GROUP_W), 0)
    bias_a = jnp.broadcast_to(dwb_ref[...], (8, GROUP_W))
    bias_d = jnp.broadcast_to(lcb_ref[...], (8, GROUP_W))
    lam = lam_ref[...]
    neg_c_softplus = -LRU_C * (jnp.maximum(-lam, 0.0) + jnp.log1p(jnp.exp(-jnp.abs(lam))))
    gm = gm_ref[...]

    def projection(u, u_ref):
        rows = pl.ds(pl.multiple_of(u * UNIT, UNIT), UNIT)
        step = 2 * GROUP_W

        def lo(c):
            def run():
                w = min(step, ATT_COL0 - c)
                u_ref[:, c:c + w] = _dot(h_ref[rows, :], win_lo_ref[:, c:c + w])
            return run

        def hi():
            u_ref[:, ATT_COL0:MIX_COLS] = _dot(h_ref[rows, :], win_hi_ref[...])

        return [lo(c) for c in range(0, ATT_COL0, step)] + [hi]

    def col(u_ref, j):
        return u_ref[:, j * GROUP_W:(j + 1) * GROUP_W]

    def mix(u, u_ref, ahead):
        rows = pl.ds(pl.multiple_of(u * UNIT, UNIT), UNIT)
        fresh = u % units_per_seq == 0
        ahead = list(ahead)
        ahead.pop(0)()

        _seg_conv(col(u_ref, 0) * jax.nn.sigmoid(col(u_ref, 1)), ea, dww_ref, bias_a, CONF_KERNEL, cbuf, fresh)
        c = cbuf[...]
        d = c - _group_mean(c, gm)
        var = _group_mean(d * d, gm)
        ahead.pop(0)()
        yn = d * lax.rsqrt(var + GN_EPS) * gng_ref[...] + gnb_ref[...]
        ya_seg = (yn * jax.nn.sigmoid(yn)).astype(BF16)

        _seg_conv(col(u_ref, 3) * col(u_ref, 4), eb, scw_ref, None, SCONV_KERNEL, cbuf, fresh)
        yb_seg = (col(u_ref, 2) * cbuf[...]).astype(BF16)

        _seg_conv(col(u_ref, 5), ed, lcw_ref, bias_d, LRU_CONV, cbuf, fresh)
        xc = cbuf[...]
        gates = _dot(xc.astype(BF16), wax_ref[...]) + bax_ref[...]
        while ahead:
            ahead.pop(0)()
        r = jax.nn.sigmoid(gates[:, :GROUP_W])
        i = jax.nn.sigmoid(gates[:, GROUP_W:])
        log_a = neg_c_softplus * r
        abuf[...] = jnp.exp(log_a)
        th = jnp.tanh(log_a)
        bbuf[...] = jnp.sqrt(-2.0 * th / (1.0 - th)) * (i * xc)
        h_run = bbuf[0:8, :]
        a_run = abuf[0:8, :]
        hloc[0:8, :] = h_run
        acum[0:8, :] = a_run
        for s in range(1, SEG):
            a_s = abuf[8 * s:8 * s + 8, :]
            h_run = a_s * h_run + bbuf[8 * s:8 * s + 8, :]
            a_run = a_s * a_run
            hloc[8 * s:8 * s + 8, :] = h_run
            acum[8 * s:8 * s + 8, :] = a_run
        dist = 1
        while dist < 8:
            keep = sub >= dist
            h_run = jnp.where(keep, a_run * pltpu.roll(h_run, dist, axis=0) + h_run, h_run)
            a_run = jnp.where(keep, a_run * pltpu.roll(a_run, dist, axis=0), a_run)
            dist *= 2
        h_in = jnp.where(fresh, 0.0, hcar[...])
        seg_end = a_run * h_in + h_run
        endb[...] = seg_end
        seg_in = jnp.where(sub == 0, h_in, pltpu.roll(seg_end, 1, axis=0))
        hcar[...] = jnp.broadcast_to(endb[7:8, :], (8, GROUP_W))
        for s in range(SEG):
            hloc[8 * s:8 * s + 8, :] = hloc[8 * s:8 * s + 8, :] + acum[8 * s:8 * s + 8, :] * seg_in
        g = col(u_ref, 6)
        gelu = 0.5 * g * (1.0 + jnp.tanh(math.sqrt(2.0 / math.pi) * (g + 0.044715 * (g * g * g))))
        yd_seg = (hloc[...] * gelu).astype(BF16)
        for y_ref, y_seg in ((ya_ref, ya_seg), (yb_ref, yb_seg), (yd_ref, yd_seg)):
            y_ref[rows, :] = _dot(from_seg, y_seg).astype(y_ref.dtype)

    for run in projection(0, u_even):
        run()

    def two_units(jj, carry):
        mix(2 * jj, u_even, projection(2 * jj + 1, u_odd))
        mix(2 * jj + 1, u_odd, projection(jnp.minimum(2 * jj + 2, n_units - 1), u_even))
        return carry

    lax.fori_loop(0, n_units // 2, two_units, 0)


def _mixers(h3, w_in, p, layer):
    bsz, seq, dm = h3.shape
    assert (seq // UNIT) % 2 == 0 and bsz % MIX_ROWS == 0
    steps, rows = bsz // MIX_ROWS, MIX_ROWS * seq
    h3 = h3.reshape(steps, rows, dm)
    hi_cols = IN_COLS - ATT_COL1
    assert ATT_COL1 % hi_cols == 0

    def par(shape):
        return pl.BlockSpec((None,) + shape, lambda b: (layer,) + (0,) * len(shape))

    out_spec = pl.BlockSpec((None, rows, GROUP_W), lambda b: (b, 0, 0))
    out_sds = jax.ShapeDtypeStruct((steps, rows, GROUP_W), BF16)
    outs = pl.pallas_call(
        _mixers_kernel,
        grid=(steps,),
        in_specs=[pl.BlockSpec((None, rows, dm), lambda b: (b, 0, 0)),
                  pl.BlockSpec((dm, ATT_COL0), lambda b: (0, 0), pipeline_mode=pl.Buffered(1)),
                  pl.BlockSpec((dm, hi_cols), lambda b: (0, ATT_COL1 // hi_cols), pipeline_mode=pl.Buffered(1)),
                  par((CONF_KERNEL, 8, GROUP_W)), par((1, GROUP_W)), par((1, GROUP_W)), par((1, GROUP_W)),
                  par((SCONV_KERNEL, 8, GROUP_W)), par((LRU_CONV, 8, GROUP_W)), par((1, GROUP_W)),
                  par((GROUP_W, 2 * GROUP_W)), par((1, 2 * GROUP_W)), par((1, GROUP_W)),
                  pl.BlockSpec((GROUP_W, GROUP_W), lambda b: (0, 0))],
        out_specs=[out_spec, out_spec, out_spec],
        out_shape=[out_sds, out_sds, out_sds],
        scratch_shapes=(
            [pltpu.VMEM((UNIT, MIX_COLS), F32)] * 2
            + [pltpu.VMEM((2 * UNIT, GROUP_W), F32)] * 3
            + [pltpu.VMEM((UNIT, GROUP_W), F32)] * 5
            + [pltpu.VMEM((8, GROUP_W), F32)] * 2
        ),
        compiler_params=pltpu.CompilerParams(
            dimension_semantics=("parallel",), vmem_limit_bytes=VMEM_LIMIT),
        name="mixers",
    )(h3, w_in, w_in,
      p["dww"], p["dwb"], p["gng"], p["gnb"], p["scw"], p["lcw"], p["lcb"],
      p["wax"], p["bax"], p["lam"], p["gm"])
    return [y.reshape(bsz, seq, GROUP_W) for y in outs]


_NT = (((1,), (1,)), ((), ()))


def _attn_kernel(n_cast, q_ref, k_ref, v_ref, *rest):
    w_refs, o_ref, w_b_refs = rest[:n_cast], rest[n_cast], rest[n_cast + 1:2 * n_cast + 1]
    kb, vt, kmm, gate_s, cmask, pos_q, rbias, qm, s_buf, acc = rest[2 * n_cast + 1:]
    for w_ref, w_b_ref in zip(w_refs, w_b_refs):
        w_b_ref[...] = w_ref[...].astype(BF16)
    nb = kb.shape[0]
    blk = MOBA_BLOCK
    heads = range(ATT_HEADS)
    log2e = math.log2(math.e)
    slopes2 = [log2e * 2.0 ** (-8.0 * (h + 1) / ATT_HEADS) for h in heads]
    lane_head = lax.broadcasted_iota(jnp.int32, (1, GROUP_W), 1) // HEAD_DIM
    pos = lambda rho: rho

    pos_s = pos(lax.broadcasted_iota(jnp.int32, (1, blk), 1))
    rel = (pos_s - (blk - 1)).astype(F32)
    first_row = lax.broadcasted_iota(jnp.int32, (V_ROWS - HEAD_DIM, blk), 0) == 0
    for n in range(nb):
        kblk = k_ref[n * blk:(n + 1) * blk, :]
        kb[n] = kblk.astype(BF16)
        v_t = v_ref[n * blk:(n + 1) * blk, :].T
        kmean = jnp.mean(kblk, axis=0, keepdims=True)
        for h in heads:
            kmm[h * nb + n:h * nb + n + 1, :] = jnp.where(lane_head == h, kmean, 0.0)
            f = jnp.exp2(slopes2[h] * rel)
            vt[n, h * V_ROWS:h * V_ROWS + HEAD_DIM, :] = (v_t[h * HEAD_DIM:(h + 1) * HEAD_DIM, :] * f).astype(BF16)
            vt[n, h * V_ROWS + HEAD_DIM:(h + 1) * V_ROWS, :] = jnp.where(first_row, f, 0.0).astype(BF16)
    t_l = pos(lax.broadcasted_iota(jnp.int32, (blk, blk), 1))
    s_l = pos(lax.broadcasted_iota(jnp.int32, (blk, blk), 0))
    cmask[...] = jnp.where(t_l >= s_l, 0.0, NEG_INF)
    pos_q[...] = pos(lax.broadcasted_iota(jnp.int32, (nb, blk), 1)).astype(F32)
    blk_id = lax.broadcasted_iota(jnp.int32, (nb, blk), 0)

    def prepare(i):
        par = i % 2
        q = q_ref[i * blk:(i + 1) * blk, :] * (HEAD_DIM ** -0.5 * log2e)
        for h in heads:
            qm[par, h] = jnp.where(lane_head == h, q, 0.0).astype(BF16)
        gate_s[...] = lax.dot_general(kmm[...], q, _NT, precision=lax.Precision.HIGHEST,
                                      preferred_element_type=F32)
        dist = (i - blk_id).astype(F32) * blk + (pos_q[...] - (blk - 1))
        for h in heads:
            g = gate_s[h * nb:(h + 1) * nb, :]
            rank = jnp.zeros((nb, blk), jnp.int32)
            for m in range(i):
                gm = gate_s[h * nb + m:h * nb + m + 1, :]
                rank = rank + jnp.where((gm > g) | ((gm == g) & (m < blk_id)), 1, 0)
            visible = (blk_id == i) | ((blk_id < i) & (rank < MOBA_TOP_K))
            rbias[par, h] = jnp.where(visible, -slopes2[h] * dist, NEG_INF)

    def score(k, i, n, m_run):
        out = []
        for h in heads:
            s_t = lax.dot_general(kb[n], qm[i % 2, h], _NT, preferred_element_type=F32)
            if n == i:
                s_t = s_t + cmask[...]
            s_buf[k % 2, h] = s_t
            m_blk = jnp.max(s_t, axis=0, keepdims=True) + rbias[i % 2, h, n:n + 1, :]
            out.append(m_blk if n == i else jnp.maximum(m_run[h], m_blk))
        return out

    def fold(k, i, n, m_now, m_acc):
        for h in heads:
            p = jnp.exp2(s_buf[k % 2, h] - (m_now[h] - rbias[i % 2, h, n:n + 1, :]))
            pv = _dot(vt[n, h * V_ROWS:(h + 1) * V_ROWS, :], p.astype(BF16))
            acc[i % 2, h] = pv if n == i else jnp.exp2(m_acc[h] - m_now[h]) * acc[i % 2, h] + pv

    def finalize(i):
        out_t = jnp.concatenate(
            [acc[i % 2, h, 0:HEAD_DIM, :] / acc[i % 2, h, HEAD_DIM:HEAD_DIM + 1, :] for h in heads], axis=0)
        o_ref[i * blk:(i + 1) * blk, :] = out_t.T.astype(o_ref.dtype)

    units = [(i, n) for i in range(nb) for n in [i] + list(range(i))]
    prepare(0)
    m_run = score(0, 0, 0, None)
    prepare(1)
    m_fold = None
    for k in range(1, len(units) + 1):
        pi, pn = units[k - 1]
        m_prev = m_run
        if k < len(units):
            i, n = units[k]
            m_run = score(k, i, n, m_run)
        fold(k - 1, pi, pn, m_prev, m_fold)
        m_fold = m_prev
        if pn == pi - 1 or pi == 0:
            finalize(pi)
        if k < len(units) and n == i and i + 1 < nb:
            prepare(i + 1)


def _attention(u3, weights):
    bsz, seq, _ = u3.shape
    nb = seq // MOBA_BLOCK
    blk = MOBA_BLOCK

    def ucol(j):
        return pl.BlockSpec((None, seq, GROUP_W), lambda b: (b, 0, j))

    def slab_in(w, layer):
        return pl.BlockSpec((None, w.shape[1] // bsz, w.shape[2]), lambda b: (layer, b, 0))

    def slab_out(w):
        return pl.BlockSpec((w.shape[1] // bsz, w.shape[2]), lambda b: (b, 0))

    return pl.pallas_call(
        functools.partial(_attn_kernel, len(weights)),
        grid=(bsz,),
        in_specs=[ucol(0), ucol(1), ucol(2)] + [slab_in(w, layer) for w, layer in weights],
        out_specs=[pl.BlockSpec((None, seq, GROUP_W), lambda b: (b, 0, 0))] + [slab_out(w) for w, _ in weights],
        out_shape=[jax.ShapeDtypeStruct((bsz, seq, GROUP_W), BF16)]
        + [jax.ShapeDtypeStruct(w.shape[1:], BF16) for w, _ in weights],
        scratch_shapes=[
            pltpu.VMEM((nb, blk, GROUP_W), BF16),
            pltpu.VMEM((nb, ATT_HEADS * V_ROWS, blk), BF16),
            pltpu.VMEM((ATT_HEADS * nb, GROUP_W), F32),
            pltpu.VMEM((ATT_HEADS * nb, blk), F32),
            pltpu.VMEM((blk, blk), F32),
            pltpu.VMEM((nb, blk), F32),
            pltpu.VMEM((2, ATT_HEADS, nb, blk), F32),
            pltpu.VMEM((2, ATT_HEADS, blk, GROUP_W), BF16),
            pltpu.VMEM((2, ATT_HEADS, blk, blk), F32),
            pltpu.VMEM((2, ATT_HEADS, V_ROWS, blk), F32),
        ],
        compiler_params=pltpu.CompilerParams(
            dimension_semantics=("parallel",), vmem_limit_bytes=VMEM_LIMIT),
        name="moba_attention",
    )(u3, u3, u3, *[w for w, _ in weights])


def _out_mlp_kernel(with_next, ya_ref, yb_ref, yc_ref, yd_ref, x_ref, wo_ref, gpost_ref, gpre_ref, w1_ref, w2_ref,
                    gmlp_ref, *rest):
    if with_next:
        gnext_ref, wq_ref, wk_ref, wv_ref, xo_ref, u_ref, hseg_ref = rest
    else:
        (xo_ref,) = rest
    half = x_ref.shape[0] // 2
    halves = [slice(0, half), slice(half, 2 * half)]

    def out_projection(rows):
        y = None
        for j, y_ref in enumerate((ya_ref, yb_ref, yc_ref, yd_ref)):
            term = _dot(y_ref[rows, :], wo_ref[j * GROUP_W:(j + 1) * GROUP_W, :])
            y = term if y is None else y + term
        return y

    def mlp(h):
        m = None
        for c in range(0, D_FF, FF_CHUNK):
            t = jnp.maximum(_dot(h, w1_ref[:, c:c + FF_CHUNK]), 0.0)
            term = _dot((t * t).astype(BF16), w2_ref[c:c + FF_CHUNK, :])
            m = term if m is None else m + term
        return m

    ys = [out_projection(rows) for rows in halves]
    for rows, y in zip(halves, ys):
        xo_ref[rows, :] = x_ref[rows, :] + _rms(y, gpost_ref[...])
        m = mlp(_rms(xo_ref[rows, :], gpre_ref[...]).astype(BF16))
        xo_ref[rows, :] = xo_ref[rows, :] + _rms(m, gmlp_ref[...])
    if with_next:
        for rows in halves:
            _in_proj_tile(xo_ref[rows, :], gnext_ref, (wq_ref, wk_ref, wv_ref), u_ref, hseg_ref, rows.start)


def _out_mlp(ya, yb, yc, yd, x2, wo, gpost, gpre, w1, w2, gmlp, layer, w_in=None, gnext=None):
    tok = x2.shape[0]
    tm = TOKEN_TILE
    with_next = w_in is not None
    yspec = pl.BlockSpec((tm, GROUP_W), lambda t: (t, 0))
    xspec = pl.BlockSpec((tm, D_MODEL), lambda t: (t, 0))
    gspec = pl.BlockSpec((None, 1, D_MODEL), lambda t: (layer, 0, 0))

    def resident(shape):
        return pl.BlockSpec(shape, lambda t: (0, 0), pipeline_mode=pl.Buffered(1))

    in_specs = [yspec, yspec, yspec, yspec, xspec, resident((D_MODEL, D_MODEL)), gspec, gspec,
                resident((D_MODEL, D_FF)), resident((D_FF, D_MODEL)), gspec]
    operands = [ya, yb, yc, yd, x2, wo, gpost, gpre, w1, w2, gmlp]
    out_specs = [xspec]
    out_shape = [jax.ShapeDtypeStruct((tok, D_MODEL), F32)]
    if with_next:
        cols = ATT_COL1 - ATT_COL0
        in_specs += [pl.BlockSpec((None, 1, D_MODEL), lambda t: (layer + 1, 0, 0))]
        in_specs += [pl.BlockSpec((D_MODEL, GROUP_W), lambda t, j=j: (0, ATT_COL0 // GROUP_W + j),
                                  pipeline_mode=pl.Buffered(1)) for j in range(3)]
        operands += [gnext, w_in, w_in, w_in]
        out_specs += [pl.BlockSpec((tm, cols), lambda t: (t, 0)), xspec]
        out_shape += [jax.ShapeDtypeStruct((tok, cols), F32), jax.ShapeDtypeStruct((tok, D_MODEL), BF16)]
    outs = pl.pallas_call(
        functools.partial(_out_mlp_kernel, with_next),
        grid=(tok // tm,),
        in_specs=in_specs,
        out_specs=out_specs,
        out_shape=out_shape,
        compiler_params=pltpu.CompilerParams(
            dimension_semantics=("parallel",), vmem_limit_bytes=VMEM_LIMIT_TAIL),
        name="out_mlp",
    )(*operands)
    return outs if with_next else outs[0]


def _block_diag_dense(w):
    depth, nblk, n, _ = w.shape
    eye = jnp.eye(nblk, dtype=w.dtype)
    return jnp.einsum("dgij,gh->dgihj", w, eye).reshape(depth, nblk * n, nblk * n)


def kernel(x, pre_mix_g, w_in, conf_dw_w, conf_dw_b, conf_gn_g, conf_gn_b, sconv_w, lru_conv_w, lru_conv_b,
           lru_wa, lru_ba, lru_wx, lru_bx, lru_lam, w_out, post_mix_g, pre_mlp_g, mlp_w1, mlp_w2, post_mlp_g):
    bsz, seq, dm = x.shape
    depth = w_in.shape[0]
    assert dm == D_MODEL and seq % MOBA_BLOCK == 0 and (bsz * seq) % TOKEN_TILE == 0
    assert seq // MOBA_BLOCK > MOBA_TOP_K
    assert D_MODEL % (16 * bsz) == 0

    row = lambda a: a.reshape(depth, 1, a.shape[-1])
    taps = lambda w: jnp.broadcast_to(w[:, :, None, :], w.shape[:2] + (8, w.shape[-1]))
    group_of = jnp.arange(GROUP_W) // (GROUP_W // CONF_GROUPS)
    params = {
        "dww": taps(conf_dw_w), "dwb": row(conf_dw_b), "gng": row(conf_gn_g), "gnb": row(conf_gn_b),
        "scw": taps(sconv_w), "lcw": taps(lru_conv_w), "lcb": row(lru_conv_b),
        "wax": jnp.concatenate([_block_diag_dense(lru_wa), _block_diag_dense(lru_wx)], axis=-1).astype(BF16),
        "bax": row(jnp.concatenate([lru_ba, lru_bx], axis=-1)),
        "lam": row(lru_lam),
        "gm": ((group_of[:, None] == group_of[None, :]).astype(F32) / (GROUP_W // CONF_GROUPS)).astype(BF16),
    }
    w_in_b = w_in[0].astype(BF16)
    g_pre_mix, g_post_mix = row(pre_mix_g), row(post_mix_g)
    g_pre_mlp, g_post_mlp = row(pre_mlp_g), row(post_mlp_g)

    x2 = x.reshape(bsz * seq, dm)
    u_att, h_seg = _in_proj(x2, g_pre_mix, w_in_b, 0)
    for layer in range(depth):
        ya, yb, yd = _mixers(h_seg.reshape(bsz, seq, dm), w_in_b, params, layer)
        to_cast = [(w_out, layer), (mlp_w1, layer), (mlp_w2, layer)] + ([(w_in, layer + 1)] if layer + 1 < depth else [])
        yc, w_out_b, w1_b, w2_b, *w_in_next = _attention(u_att.reshape(bsz, seq, ATT_COL1 - ATT_COL0), to_cast)
        flat = lambda a: a.reshape(bsz * seq, GROUP_W)
        args = (flat(ya), flat(yb), flat(yc), flat(yd), x2, w_out_b, g_post_mix, g_pre_mlp, w1_b, w2_b, g_post_mlp)
        if layer + 1 < depth:
            (w_in_b,) = w_in_next
            x2, u_att, h_seg = _out_mlp(*args, layer, w_in=w_in_b, gnext=g_pre_mix)
        else:
            x2 = _out_mlp(*args, layer)
    return x2.reshape(bsz, seq, dm)
```

```python
import functools
import math

import jax
import jax.numpy as jnp
from jax import lax
from jax.experimental import pallas as pl
from jax.experimental.pallas import tpu as pltpu

F32 = jnp.float32
BF16 = jnp.bfloat16

D_MODEL = 1024
GROUP_W = 256
CONF_KERNEL = 31
CONF_GROUPS = 4
GN_EPS = 1e-5
SCONV_KERNEL = 3
ATT_HEADS = 4
HEAD_DIM = GROUP_W // ATT_HEADS
MOBA_BLOCK = 256
MOBA_TOP_K = 3
LRU_BLOCKS = 4
LRU_CONV = 4
LRU_C = 8.0
D_FF = 4 * D_MODEL
RMS_EPS = 1e-6
IN_COLS = 10 * GROUP_W
ATT_COL0, ATT_COL1 = 5 * GROUP_W, 8 * GROUP_W
MIX_COLS = IN_COLS - (ATT_COL1 - ATT_COL0)

SUBLANES = 8
BF16_TILE_ROWS = 16

TOKEN_TILE = 1024
UNIT = MOBA_BLOCK
MIX_ROWS = 2
SEG = UNIT // SUBLANES
CONV_GROUPS = 8
FF_CHUNK = 1024
V_ROWS = HEAD_DIM + BF16_TILE_ROWS
VMEM_LIMIT = 56 * 1024 * 1024
VMEM_LIMIT_TAIL = 62 * 1024 * 1024

NEG_INF = float("-inf")


def _rms(x, g):
    return x * lax.rsqrt(jnp.mean(x * x, axis=-1, keepdims=True) + RMS_EPS) * g


def _dot(a, b):
    return jnp.dot(a, b, preferred_element_type=F32)


def _grp(a):
    return slice(SUBLANES * a, SUBLANES * (a + 1))


def _segment_permutations():
    row = lax.broadcasted_iota(jnp.int32, (UNIT, UNIT), 0)
    col = lax.broadcasted_iota(jnp.int32, (UNIT, UNIT), 1)
    token_of = lambda rho: SEG * (rho & (SUBLANES - 1)) + (rho >> (SUBLANES.bit_length() - 1))
    to_seg = jnp.where(col == token_of(row), 1.0, 0.0).astype(BF16)
    from_seg = jnp.where(row == token_of(col), 1.0, 0.0).astype(BF16)
    return to_seg, from_seg


def _in_proj_tile(x, g_ref, w_refs, u_ref, hseg_ref, row0):
    rows = x.shape[0]
    h = _rms(x, g_ref[...]).astype(BF16)
    for j, w_ref in enumerate(w_refs):
        u_ref[row0:row0 + rows, j * GROUP_W:(j + 1) * GROUP_W] = _dot(h, w_ref[...])
    to_seg, _ = _segment_permutations()
    for r in range(0, rows, UNIT):
        hseg_ref[row0 + r:row0 + r + UNIT, :] = _dot(to_seg, h[r:r + UNIT, :]).astype(BF16)


def _in_proj_kernel(x_ref, g_ref, wq_ref, wk_ref, wv_ref, u_ref, hseg_ref):
    _in_proj_tile(x_ref[...], g_ref, (wq_ref, wk_ref, wv_ref), u_ref, hseg_ref, 0)


def _in_proj(x2, g, w, layer):
    tok = x2.shape[0]
    cols = ATT_COL1 - ATT_COL0

    def wcol(j):
        return pl.BlockSpec((D_MODEL, GROUP_W), lambda t: (0, ATT_COL0 // GROUP_W + j))

    xspec = pl.BlockSpec((TOKEN_TILE, D_MODEL), lambda t: (t, 0))
    return pl.pallas_call(
        _in_proj_kernel,
        grid=(tok // TOKEN_TILE,),
        in_specs=[xspec, pl.BlockSpec((None, 1, D_MODEL), lambda t: (layer, 0, 0)), wcol(0), wcol(1), wcol(2)],
        out_specs=[pl.BlockSpec((TOKEN_TILE, cols), lambda t: (t, 0)), xspec],
        out_shape=[jax.ShapeDtypeStruct((tok, cols), F32), jax.ShapeDtypeStruct((tok, D_MODEL), BF16)],
        compiler_params=pltpu.CompilerParams(
            dimension_semantics=("parallel",), vmem_limit_bytes=VMEM_LIMIT),
        name="in_proj",
    )(x2, g, w, w, w)


def _group_mean(v, gm):
    hi = v.astype(BF16)
    lo = (v - hi.astype(F32)).astype(BF16)
    return _dot(hi, gm) + _dot(lo, gm)


def _seg_conv(z, e_ref, w_ref, bias, width, out_ref, fresh):
    sub = lax.broadcasted_iota(jnp.int32, (SUBLANES, GROUP_W), 0)
    for a in range(SEG - (width - 1), SEG):
        cur = z[_grp(a), :]
        prev = jnp.where(fresh, 0.0, e_ref[_grp(SEG + a), :])
        e_ref[_grp(a), :] = pltpu.roll(jnp.where(sub == SUBLANES - 1, prev, cur), 1, axis=0)
    e_ref[UNIT:2 * UNIT, :] = z
    for a0 in range(0, SEG, CONV_GROUPS):
        accs = [bias] * CONV_GROUPS
        for j in range(width):
            w_j = w_ref[width - 1 - j]
            for a in range(a0, a0 + CONV_GROUPS):
                term = w_j * e_ref[_grp(SEG + a - j), :]
                accs[a - a0] = term if accs[a - a0] is None else accs[a - a0] + term
        for a in range(a0, a0 + CONV_GROUPS):
            out_ref[_grp(a), :] = accs[a - a0]


def _mixers_kernel(h_ref, win_lo_ref, win_hi_ref,
                   dww_ref, dwb_ref, gng_ref, gnb_ref, scw_ref, lcw_ref, lcb_ref,
                   wax_ref, bax_ref, lam_ref, gm_ref,
                   ya_ref, yb_ref, yd_ref,
                   u_even, u_odd, ea, eb, ed, cbuf, abuf, bbuf, hloc, acum, endb, hcar):
    _, from_seg = _segment_permutations()
    n_units = h_ref.shape[0] // UNIT
    units_per_seq = n_units // MIX_ROWS
    for e in (ea, eb, ed):
        e[UNIT:2 * UNIT, :] = jnp.zeros((UNIT, GROUP_W), F32)
    hcar[...] = jnp.zeros_like(hcar)

    sub = lax.broadcasted_iota(jnp.int32, (SUBLANES, GROUP_W), 0)
    bias_a = jnp.broadcast_to(dwb_ref[...], (SUBLANES, GROUP_W))
    bias_d = jnp.broadcast_to(lcb_ref[...], (SUBLANES, GROUP_W))
    lam = lam_ref[...]
    neg_c_softplus = -LRU_C * (jnp.maximum(-lam, 0.0) + jnp.log1p(jnp.exp(-jnp.abs(lam))))
    gm = gm_ref[...]

    def projection(u, u_ref):
        rows = pl.ds(pl.multiple_of(u * UNIT, UNIT), UNIT)
        step = 2 * GROUP_W

        def lo(c):
            def run():
                w = min(step, ATT_COL0 - c)
                u_ref[:, c:c + w] = _dot(h_ref[rows, :], win_lo_ref[:, c:c + w])
            return run

        def hi():
            u_ref[:, ATT_COL0:MIX_COLS] = _dot(h_ref[rows, :], win_hi_ref[...])

        return [lo(c) for c in range(0, ATT_COL0, step)] + [hi]

    def col(u_ref, j):
        return u_ref[:, j * GROUP_W:(j + 1) * GROUP_W]

    def mix(u, u_ref, ahead):
        rows = pl.ds(pl.multiple_of(u * UNIT, UNIT), UNIT)
        fresh = u % units_per_seq == 0
        ahead = list(ahead)
        ahead.pop(0)()

        _seg_conv(col(u_ref, 0) * jax.nn.sigmoid(col(u_ref, 1)), ea, dww_ref, bias_a, CONF_KERNEL, cbuf, fresh)
        c = cbuf[...]
        d = c - _group_mean(c, gm)
        var = _group_mean(d * d, gm)
        ahead.pop(0)()
        yn = d * lax.rsqrt(var + GN_EPS) * gng_ref[...] + gnb_ref[...]
        ya_seg = (yn * jax.nn.sigmoid(yn)).astype(BF16)

        _seg_conv(col(u_ref, 3) * col(u_ref, 4), eb, scw_ref, None, SCONV_KERNEL, cbuf, fresh)
        yb_seg = (col(u_ref, 2) * cbuf[...]).astype(BF16)

        _seg_conv(col(u_ref, 5), ed, lcw_ref, bias_d, LRU_CONV, cbuf, fresh)
        xc = cbuf[...]
        gates = _dot(xc.astype(BF16), wax_ref[...]) + bax_ref[...]
        while ahead:
            ahead.pop(0)()
        r = jax.nn.sigmoid(gates[:, :GROUP_W])
        i = jax.nn.sigmoid(gates[:, GROUP_W:])
        log_a = neg_c_softplus * r
        abuf[...] = jnp.exp(log_a)
        th = jnp.tanh(log_a)
        bbuf[...] = jnp.sqrt(-2.0 * th / (1.0 - th)) * (i * xc)
        h_run = bbuf[_grp(0), :]
        a_run = abuf[_grp(0), :]
        hloc[_grp(0), :] = h_run
        acum[_grp(0), :] = a_run
        for s in range(1, SEG):
            a_s = abuf[_grp(s), :]
            h_run = a_s * h_run + bbuf[_grp(s), :]
            a_run = a_s * a_run
            hloc[_grp(s), :] = h_run
            acum[_grp(s), :] = a_run
        dist = 1
        while dist < SUBLANES:
            keep = sub >= dist
            h_run = jnp.where(keep, a_run * pltpu.roll(h_run, dist, axis=0) + h_run, h_run)
            a_run = jnp.where(keep, a_run * pltpu.roll(a_run, dist, axis=0), a_run)
            dist *= 2
        h_in = jnp.where(fresh, 0.0, hcar[...])
        seg_end = a_run * h_in + h_run
        endb[...] = seg_end
        seg_in = jnp.where(sub == 0, h_in, pltpu.roll(seg_end, 1, axis=0))
        hcar[...] = jnp.broadcast_to(endb[SUBLANES - 1:SUBLANES, :], (SUBLANES, GROUP_W))
        for s in range(SEG):
            hloc[_grp(s), :] = hloc[_grp(s), :] + acum[_grp(s), :] * seg_in
        g = col(u_ref, 6)
        gelu = 0.5 * g * (1.0 + jnp.tanh(math.sqrt(2.0 / math.pi) * (g + 0.044715 * (g * g * g))))
        yd_seg = (hloc[...] * gelu).astype(BF16)
        for y_ref, y_seg in ((ya_ref, ya_seg), (yb_ref, yb_seg), (yd_ref, yd_seg)):
            y_ref[rows, :] = _dot(from_seg, y_seg).astype(y_ref.dtype)

    for run in projection(0, u_even):
        run()

    def two_units(jj, carry):
        mix(2 * jj, u_even, projection(2 * jj + 1, u_odd))
        mix(2 * jj + 1, u_odd, projection(jnp.minimum(2 * jj + 2, n_units - 1), u_even))
        return carry

    lax.fori_loop(0, n_units // 2, two_units, 0)


def _mixers(h3, w_in, p, layer):
    bsz, seq, dm = h3.shape
    assert (seq // UNIT) % 2 == 0 and bsz % MIX_ROWS == 0
    steps, rows = bsz // MIX_ROWS, MIX_ROWS * seq
    h3 = h3.reshape(steps, rows, dm)
    hi_cols = IN_COLS - ATT_COL1
    assert ATT_COL1 % hi_cols == 0

    def par(shape):
        return pl.BlockSpec((None,) + shape, lambda b: (layer,) + (0,) * len(shape))

    out_spec = pl.BlockSpec((None, rows, GROUP_W), lambda b: (b, 0, 0))
    out_sds = jax.ShapeDtypeStruct((steps, rows, GROUP_W), BF16)
    outs = pl.pallas_call(
        _mixers_kernel,
        grid=(steps,),
        in_specs=[pl.BlockSpec((None, rows, dm), lambda b: (b, 0, 0)),
                  pl.BlockSpec((dm, ATT_COL0), lambda b: (0, 0), pipeline_mode=pl.Buffered(1)),
                  pl.BlockSpec((dm, hi_cols), lambda b: (0, ATT_COL1 // hi_cols), pipeline_mode=pl.Buffered(1)),
                  par((CONF_KERNEL, SUBLANES, GROUP_W)), par((1, GROUP_W)), par((1, GROUP_W)), par((1, GROUP_W)),
                  par((SCONV_KERNEL, SUBLANES, GROUP_W)), par((LRU_CONV, SUBLANES, GROUP_W)), par((1, GROUP_W)),
                  par((GROUP_W, 2 * GROUP_W)), par((1, 2 * GROUP_W)), par((1, GROUP_W)),
                  pl.BlockSpec((GROUP_W, GROUP_W), lambda b: (0, 0))],
        out_specs=[out_spec, out_spec, out_spec],
        out_shape=[out_sds, out_sds, out_sds],
        scratch_shapes=(
            [pltpu.VMEM((UNIT, MIX_COLS), F32)] * 2
            + [pltpu.VMEM((2 * UNIT, GROUP_W), F32)] * 3
            + [pltpu.VMEM((UNIT, GROUP_W), F32)] * 5
            + [pltpu.VMEM((SUBLANES, GROUP_W), F32)] * 2
        ),
        compiler_params=pltpu.CompilerParams(
            dimension_semantics=("parallel",), vmem_limit_bytes=VMEM_LIMIT),
        name="mixers",
    )(h3, w_in, w_in,
      p["dww"], p["dwb"], p["gng"], p["gnb"], p["scw"], p["lcw"], p["lcb"],
      p["wax"], p["bax"], p["lam"], p["gm"])
    return [y.reshape(bsz, seq, GROUP_W) for y in outs]


_NT = (((1,), (1,)), ((), ()))


def _attn_kernel(n_cast, q_ref, k_ref, v_ref, *rest):
    w_refs, o_ref, w_b_refs = rest[:n_cast], rest[n_cast], rest[n_cast + 1:2 * n_cast + 1]
    kb, vt, kmm, gate_s, cmask, pos_q, rbias, qm, s_buf, acc = rest[2 * n_cast + 1:]
    for w_ref, w_b_ref in zip(w_refs, w_b_refs):
        w_b_ref[...] = w_ref[...].astype(BF16)
    nb = kb.shape[0]
    blk = MOBA_BLOCK
    heads = range(ATT_HEADS)
    log2e = math.log2(math.e)
    slopes2 = [log2e * 2.0 ** (-8.0 * (h + 1) / ATT_HEADS) for h in heads]
    lane_head = lax.broadcasted_iota(jnp.int32, (1, GROUP_W), 1) // HEAD_DIM
    pos = lambda rho: rho

    pos_s = pos(lax.broadcasted_iota(jnp.int32, (1, blk), 1))
    rel = (pos_s - (blk - 1)).astype(F32)
    first_row = lax.broadcasted_iota(jnp.int32, (V_ROWS - HEAD_DIM, blk), 0) == 0
    for n in range(nb):
        kblk = k_ref[n * blk:(n + 1) * blk, :]
        kb[n] = kblk.astype(BF16)
        v_t = v_ref[n * blk:(n + 1) * blk, :].T
        kmean = jnp.mean(kblk, axis=0, keepdims=True)
        for h in heads:
            kmm[h * nb + n:h * nb + n + 1, :] = jnp.where(lane_head == h, kmean, 0.0)
            f = jnp.exp2(slopes2[h] * rel)
            vt[n, h * V_ROWS:h * V_ROWS + HEAD_DIM, :] = (v_t[h * HEAD_DIM:(h + 1) * HEAD_DIM, :] * f).astype(BF16)
            vt[n, h * V_ROWS + HEAD_DIM:(h + 1) * V_ROWS, :] = jnp.where(first_row, f, 0.0).astype(BF16)
    t_l = pos(lax.broadcasted_iota(jnp.int32, (blk, blk), 1))
    s_l = pos(lax.broadcasted_iota(jnp.int32, (blk, blk), 0))
    cmask[...] = jnp.where(t_l >= s_l, 0.0, NEG_INF)
    pos_q[...] = pos(lax.broadcasted_iota(jnp.int32, (nb, blk), 1)).astype(F32)
    blk_id = lax.broadcasted_iota(jnp.int32, (nb, blk), 0)

    def prepare(i):
        par = i % 2
        q = q_ref[i * blk:(i + 1) * blk, :] * (HEAD_DIM ** -0.5 * log2e)
        for h in heads:
            qm[par, h] = jnp.where(lane_head == h, q, 0.0).astype(BF16)
        gate_s[...] = lax.dot_general(kmm[...], q, _NT, precision=lax.Precision.HIGHEST,
                                      preferred_element_type=F32)
        dist = (i - blk_id).astype(F32) * blk + (pos_q[...] - (blk - 1))
        for h in heads:
            g = gate_s[h * nb:(h + 1) * nb, :]
            rank = jnp.zeros((nb, blk), jnp.int32)
            for m in range(i):
                gm = gate_s[h * nb + m:h * nb + m + 1, :]
                rank = rank + jnp.where((gm > g) | ((gm == g) & (m < blk_id)), 1, 0)
            visible = (blk_id == i) | ((blk_id < i) & (rank < MOBA_TOP_K))
            rbias[par, h] = jnp.where(visible, -slopes2[h] * dist, NEG_INF)

    def score(k, i, n, m_run):
        out = []
        for h in heads:
            s_t = lax.dot_general(kb[n], qm[i % 2, h], _NT, preferred_element_type=F32)
            if n == i:
                s_t = s_t + cmask[...]
            s_buf[k % 2, h] = s_t
            m_blk = jnp.max(s_t, axis=0, keepdims=True) + rbias[i % 2, h, n:n + 1, :]
            out.append(m_blk if n == i else jnp.maximum(m_run[h], m_blk))
        return out

    def fold(k, i, n, m_now, m_acc):
        for h in heads:
            p = jnp.exp2(s_buf[k % 2, h] - (m_now[h] - rbias[i % 2, h, n:n + 1, :]))
            pv = _dot(vt[n, h * V_ROWS:(h + 1) * V_ROWS, :], p.astype(BF16))
            acc[i % 2, h] = pv if n == i else jnp.exp2(m_acc[h] - m_now[h]) * acc[i % 2, h] + pv

    def finalize(i):
        out_t = jnp.concatenate(
            [acc[i % 2, h, 0:HEAD_DIM, :] / acc[i % 2, h, HEAD_DIM:HEAD_DIM + 1, :] for h in heads], axis=0)
        o_ref[i * blk:(i + 1) * blk, :] = out_t.T.astype(o_ref.dtype)

    units = [(i, n) for i in range(nb) for n in [i] + list(range(i))]
    prepare(0)
    m_run = score(0, 0, 0, None)
    prepare(1)
    m_fold = None
    for k in range(1, len(units) + 1):
        pi, pn = units[k - 1]
        m_prev = m_run
        if k < len(units):
            i, n = units[k]
            m_run = score(k, i, n, m_run)
        fold(k - 1, pi, pn, m_prev, m_fold)
        m_fold = m_prev
        if pn == pi - 1 or pi == 0:
            finalize(pi)
        if k < len(units) and n == i and i + 1 < nb:
            prepare(i + 1)


def _attention(u3, weights):
    bsz, seq, _ = u3.shape
    nb = seq // MOBA_BLOCK
    blk = MOBA_BLOCK

    def ucol(j):
        return pl.BlockSpec((None, seq, GROUP_W), lambda b: (b, 0, j))

    def slab_in(w, layer):
        return pl.BlockSpec((None, w.shape[1] // bsz, w.shape[2]), lambda b: (layer, b, 0))

    def slab_out(w):
        return pl.BlockSpec((w.shape[1] // bsz, w.shape[2]), lambda b: (b, 0))

    return pl.pallas_call(
        functools.partial(_attn_kernel, len(weights)),
        grid=(bsz,),
        in_specs=[ucol(0), ucol(1), ucol(2)] + [slab_in(w, layer) for w, layer in weights],
        out_specs=[pl.BlockSpec((None, seq, GROUP_W), lambda b: (b, 0, 0))] + [slab_out(w) for w, _ in weights],
        out_shape=[jax.ShapeDtypeStruct((bsz, seq, GROUP_W), BF16)]
        + [jax.ShapeDtypeStruct(w.shape[1:], BF16) for w, _ in weights],
        scratch_shapes=[
            pltpu.VMEM((nb, blk, GROUP_W), BF16),
            pltpu.VMEM((nb, ATT_HEADS * V_ROWS, blk), BF16),
            pltpu.VMEM((ATT_HEADS * nb, GROUP_W), F32),
            pltpu.VMEM((ATT_HEADS * nb, blk), F32),
            pltpu.VMEM((blk, blk), F32),
            pltpu.VMEM((nb, blk), F32),
            pltpu.VMEM((2, ATT_HEADS, nb, blk), F32),
            pltpu.VMEM((2, ATT_HEADS, blk, GROUP_W), BF16),
            pltpu.VMEM((2, ATT_HEADS, blk, blk), F32),
            pltpu.VMEM((2, ATT_HEADS, V_ROWS, blk), F32),
        ],
        compiler_params=pltpu.CompilerParams(
            dimension_semantics=("parallel",), vmem_limit_bytes=VMEM_LIMIT),
        name="moba_attention",
    )(u3, u3, u3, *[w for w, _ in weights])


def _out_mlp_kernel(with_next, ya_ref, yb_ref, yc_ref, yd_ref, x_ref, wo_ref, gpost_ref, gpre_ref, w1_ref, w2_ref,
                    gmlp_ref, *rest):
    if with_next:
        gnext_ref, wq_ref, wk_ref, wv_ref, xo_ref, u_ref, hseg_ref = rest
    else:
        (xo_ref,) = rest
    half = x_ref.shape[0] // 2
    halves = [slice(0, half), slice(half, 2 * half)]

    def out_projection(rows):
        y = None
        for j, y_ref in enumerate((ya_ref, yb_ref, yc_ref, yd_ref)):
            term = _dot(y_ref[rows, :], wo_ref[j * GROUP_W:(j + 1) * GROUP_W, :])
            y = term if y is None else y + term
        return y

    def mlp(h):
        m = None
        for c in range(0, D_FF, FF_CHUNK):
            t = jnp.maximum(_dot(h, w1_ref[:, c:c + FF_CHUNK]), 0.0)
            term = _dot((t * t).astype(BF16), w2_ref[c:c + FF_CHUNK, :])
            m = term if m is None else m + term
        return m

    ys = [out_projection(rows) for rows in halves]
    for rows, y in zip(halves, ys):
        xo_ref[rows, :] = x_ref[rows, :] + _rms(y, gpost_ref[...])
        m = mlp(_rms(xo_ref[rows, :], gpre_ref[...]).astype(BF16))
        xo_ref[rows, :] = xo_ref[rows, :] + _rms(m, gmlp_ref[...])
    if with_next:
        for rows in halves:
            _in_proj_tile(xo_ref[rows, :], gnext_ref, (wq_ref, wk_ref, wv_ref), u_ref, hseg_ref, rows.start)


def _out_mlp(ya, yb, yc, yd, x2, wo, gpost, gpre, w1, w2, gmlp, layer, w_in=None, gnext=None):
    tok = x2.shape[0]
    tm = TOKEN_TILE
    with_next = w_in is not None
    yspec = pl.BlockSpec((tm, GROUP_W), lambda t: (t, 0))
    xspec = pl.BlockSpec((tm, D_MODEL), lambda t: (t, 0))
    gspec = pl.BlockSpec((None, 1, D_MODEL), lambda t: (layer, 0, 0))

    def resident(shape):
        return pl.BlockSpec(shape, lambda t: (0, 0), pipeline_mode=pl.Buffered(1))

    in_specs = [yspec, yspec, yspec, yspec, xspec, resident((D_MODEL, D_MODEL)), gspec, gspec,
                resident((D_MODEL, D_FF)), resident((D_FF, D_MODEL)), gspec]
    operands = [ya, yb, yc, yd, x2, wo, gpost, gpre, w1, w2, gmlp]
    out_specs = [xspec]
    out_shape = [jax.ShapeDtypeStruct((tok, D_MODEL), F32)]
    if with_next:
        cols = ATT_COL1 - ATT_COL0
        in_specs += [pl.BlockSpec((None, 1, D_MODEL), lambda t: (layer + 1, 0, 0))]
        in_specs += [pl.BlockSpec((D_MODEL, GROUP_W), lambda t, j=j: (0, ATT_COL0 // GROUP_W + j),
                                  pipeline_mode=pl.Buffered(1)) for j in range(3)]
        operands += [gnext, w_in, w_in, w_in]
        out_specs += [pl.BlockSpec((tm, cols), lambda t: (t, 0)), xspec]
        out_shape += [jax.ShapeDtypeStruct((tok, cols), F32), jax.ShapeDtypeStruct((tok, D_MODEL), BF16)]
    outs = pl.pallas_call(
        functools.partial(_out_mlp_kernel, with_next),
        grid=(tok // tm,),
        in_specs=in_specs,
        out_specs=out_specs,
        out_shape=out_shape,
        compiler_params=pltpu.CompilerParams(
            dimension_semantics=("parallel",), vmem_limit_bytes=VMEM_LIMIT_TAIL),
        name="out_mlp",
    )(*operands)
    return outs if with_next else outs[0]


def _block_diag_dense(w):
    depth, nblk, n, _ = w.shape
    eye = jnp.eye(nblk, dtype=w.dtype)
    return jnp.einsum("dgij,gh->dgihj", w, eye).reshape(depth, nblk * n, nblk * n)


def kernel(x, pre_mix_g, w_in, conf_dw_w, conf_dw_b, conf_gn_g, conf_gn_b, sconv_w, lru_conv_w, lru_conv_b,
           lru_wa, lru_ba, lru_wx, lru_bx, lru_lam, w_out, post_mix_g, pre_mlp_g, mlp_w1, mlp_w2, post_mlp_g):
    bsz, seq, dm = x.shape
    depth = w_in.shape[0]
    assert dm == D_MODEL and seq % MOBA_BLOCK == 0 and (bsz * seq) % TOKEN_TILE == 0
    assert seq // MOBA_BLOCK > MOBA_TOP_K
    assert D_MODEL % (BF16_TILE_ROWS * bsz) == 0

    row = lambda a: a.reshape(depth, 1, a.shape[-1])
    taps = lambda w: jnp.broadcast_to(w[:, :, None, :], w.shape[:2] + (SUBLANES, w.shape[-1]))
    group_of = jnp.arange(GROUP_W) // (GROUP_W // CONF_GROUPS)
    params = {
        "dww": taps(conf_dw_w), "dwb": row(conf_dw_b), "gng": row(conf_gn_g), "gnb": row(conf_gn_b),
        "scw": taps(sconv_w), "lcw": taps(lru_conv_w), "lcb": row(lru_conv_b),
        "wax": jnp.concatenate([_block_diag_dense(lru_wa), _block_diag_dense(lru_wx)], axis=-1).astype(BF16),
        "bax": row(jnp.concatenate([lru_ba, lru_bx], axis=-1)),
        "lam": row(lru_lam),
        "gm": ((group_of[:, None] == group_of[None, :]).astype(F32) / (GROUP_W // CONF_GROUPS)).astype(BF16),
    }
    w_in_b = w_in[0].astype(BF16)
    g_pre_mix, g_post_mix = row(pre_mix_g), row(post_mix_g)
    g_pre_mlp, g_post_mlp = row(pre_mlp_g), row(post_mlp_g)

    x2 = x.reshape(bsz * seq, dm)
    u_att, h_seg = _in_proj(x2, g_pre_mix, w_in_b, 0)
    for layer in range(depth):
        ya, yb, yd = _mixers(h_seg.reshape(bsz, seq, dm), w_in_b, params, layer)
        to_cast = [(w_out, layer), (mlp_w1, layer), (mlp_w2, layer)] + ([(w_in, layer + 1)] if layer + 1 < depth else [])
        yc, w_out_b, w1_b, w2_b, *w_in_next = _attention(u_att.reshape(bsz, seq, ATT_COL1 - ATT_COL0), to_cast)
        flat = lambda a: a.reshape(bsz * seq, GROUP_W)
        args = (flat(ya), flat(yb), flat(yc), flat(yd), x2, w_out_b, g_post_mix, g_pre_mlp, w1_b, w2_b, g_post_mlp)
        if layer + 1 < depth:
            (w_in_b,) = w_in_next
            x2, u_att, h_seg = _out_mlp(*args, layer, w_in=w_in_b, gnext=g_pre_mix)
        else:
            x2 = _out_mlp(*args, layer)
    return x2.reshape(bsz, seq, dm)
```

```python
import functools
import math

import jax
import jax.numpy as jnp
from jax import lax
from jax.experimental import pallas as pl
from jax.experimental.pallas import tpu as pltpu

F32 = jnp.float32
BF16 = jnp.bfloat16

D_MODEL = 1024
GROUP_W = 256
CONF_KERNEL = 31
CONF_GROUPS = 4
GN_EPS = 1e-5
SCONV_KERNEL = 3
ATT_HEADS = 4
HEAD_DIM = GROUP_W // ATT_HEADS
MOBA_BLOCK = 256
MOBA_TOP_K = 3
LRU_BLOCKS = 4
LRU_CONV = 4
LRU_C = 8.0
D_FF = 4 * D_MODEL
RMS_EPS = 1e-6
IN_COLS = 10 * GROUP_W
ATT_COL0, ATT_COL1 = 5 * GROUP_W, 8 * GROUP_W
MIX_COLS = IN_COLS - (ATT_COL1 - ATT_COL0)

SUBLANES = 8
LANES = 128
BF16_TILE_ROWS = 16

TOKEN_TILE = 1024
UNIT = MOBA_BLOCK
MIX_ROWS = 2
SEG = UNIT // SUBLANES
CONV_GROUPS = 8
FF_CHUNK = 1024
V_ROWS = HEAD_DIM + BF16_TILE_ROWS
VMEM_LIMIT = 56 * 1024 * 1024
VMEM_LIMIT_TAIL = 62 * 1024 * 1024

NEG_INF = float("-inf")


def _rms(x, g):
    return x * lax.rsqrt(jnp.mean(x * x, axis=-1, keepdims=True) + RMS_EPS) * g


def _dot(a, b):
    return jnp.dot(a, b, preferred_element_type=F32)


def _grp(a):
    return slice(SUBLANES * a, SUBLANES * (a + 1))


def _segment_permutations():
    row = lax.broadcasted_iota(jnp.int32, (UNIT, UNIT), 0)
    col = lax.broadcasted_iota(jnp.int32, (UNIT, UNIT), 1)
    token_of = lambda rho: SEG * (rho & (SUBLANES - 1)) + (rho >> (SUBLANES.bit_length() - 1))
    to_seg = jnp.where(col == token_of(row), 1.0, 0.0).astype(BF16)
    from_seg = jnp.where(row == token_of(col), 1.0, 0.0).astype(BF16)
    return to_seg, from_seg


def _in_proj_tile(x, g_ref, w_refs, u_ref, hseg_ref, row0):
    rows = x.shape[0]
    h = _rms(x, g_ref[...]).astype(BF16)
    for j, w_ref in enumerate(w_refs):
        u_ref[row0:row0 + rows, j * GROUP_W:(j + 1) * GROUP_W] = _dot(h, w_ref[...])
    to_seg, _ = _segment_permutations()
    for r in range(0, rows, UNIT):
        hseg_ref[row0 + r:row0 + r + UNIT, :] = _dot(to_seg, h[r:r + UNIT, :]).astype(BF16)


def _in_proj_kernel(x_ref, g_ref, wq_ref, wk_ref, wv_ref, u_ref, hseg_ref):
    _in_proj_tile(x_ref[...], g_ref, (wq_ref, wk_ref, wv_ref), u_ref, hseg_ref, 0)


def _in_proj(x2, g, w, layer):
    tok = x2.shape[0]
    cols = ATT_COL1 - ATT_COL0

    def wcol(j):
        return pl.BlockSpec((D_MODEL, GROUP_W), lambda t: (0, ATT_COL0 // GROUP_W + j))

    xspec = pl.BlockSpec((TOKEN_TILE, D_MODEL), lambda t: (t, 0))
    return pl.pallas_call(
        _in_proj_kernel,
        grid=(tok // TOKEN_TILE,),
        in_specs=[xspec, pl.BlockSpec((None, 1, D_MODEL), lambda t: (layer, 0, 0)), wcol(0), wcol(1), wcol(2)],
        out_specs=[pl.BlockSpec((TOKEN_TILE, cols), lambda t: (t, 0)), xspec],
        out_shape=[jax.ShapeDtypeStruct((tok, cols), F32), jax.ShapeDtypeStruct((tok, D_MODEL), BF16)],
        compiler_params=pltpu.CompilerParams(
            dimension_semantics=("parallel",), vmem_limit_bytes=VMEM_LIMIT),
        name="in_proj",
    )(x2, g, w, w, w)


def _group_mean(v, gm):
    hi = v.astype(BF16)
    lo = (v - hi.astype(F32)).astype(BF16)
    return _dot(hi, gm) + _dot(lo, gm)


def _seg_conv(z, e_ref, w_ref, bias, width, out_ref, fresh, ls):
    sub = lax.broadcasted_iota(jnp.int32, (SUBLANES, LANES), 0)
    for a in range(SEG - (width - 1), SEG):
        cur = z[_grp(a), :]
        prev = jnp.where(fresh, 0.0, e_ref[_grp(SEG + a), ls])
        e_ref[_grp(a), ls] = pltpu.roll(jnp.where(sub == SUBLANES - 1, prev, cur), 1, axis=0)
    e_ref[UNIT:2 * UNIT, ls] = z
    for a0 in range(0, SEG, CONV_GROUPS):
        accs = [bias] * CONV_GROUPS
        for j in range(width):
            w_j = w_ref[width - 1 - j, :, ls]
            for a in range(a0, a0 + CONV_GROUPS):
                term = w_j * e_ref[_grp(SEG + a - j), ls]
                accs[a - a0] = term if accs[a - a0] is None else accs[a - a0] + term
        for a in range(a0, a0 + CONV_GROUPS):
            out_ref[_grp(a), ls] = accs[a - a0]


def _mixers_kernel(h_ref, win_lo_ref, win_hi_ref,
                   dww_ref, dwb_ref, gng_ref, gnb_ref, scw_ref, lcw_ref, lcb_ref,
                   wax_ref, bax_ref, lam_ref, gm_ref,
                   ya_ref, yb_ref, yd_ref,
                   u_even, u_odd, ea, eb, ed, cbuf, abuf, bbuf, hloc, acum, endb, hcar):
    _, from_seg = _segment_permutations()
    n_units = h_ref.shape[0] // UNIT
    units_per_seq = n_units // MIX_ROWS
    for e in (ea, eb, ed):
        e[UNIT:2 * UNIT, :] = jnp.zeros((UNIT, GROUP_W), F32)
    hcar[...] = jnp.zeros_like(hcar)


    def projection(u, u_ref):
        rows = pl.ds(pl.multiple_of(u * UNIT, UNIT), UNIT)
        step = 2 * GROUP_W

        def lo(c):
            def run():
                w = min(step, ATT_COL0 - c)
                u_ref[:, c:c + w] = _dot(h_ref[rows, :], win_lo_ref[:, c:c + w])
            return run

        def hi():
            u_ref[:, ATT_COL0:MIX_COLS] = _dot(h_ref[rows, :], win_hi_ref[...])

        return [lo(c) for c in range(0, ATT_COL0, step)] + [hi]

    def colh(u_ref, j, ls):
        return u_ref[:, j * GROUP_W + ls.start:j * GROUP_W + ls.stop]

    def mix(u, u_ref, ahead):
        rows = pl.ds(pl.multiple_of(u * UNIT, UNIT), UNIT)
        fresh = u % units_per_seq == 0
        ahead = list(ahead)
        ahead.pop(0)()
        halves = [slice(h * LANES, (h + 1) * LANES) for h in range(GROUP_W // LANES)]
        sub = lax.broadcasted_iota(jnp.int32, (SUBLANES, LANES), 0)
        ya_seg, yb_seg, yd_seg = [], [], []

        for ls in halves:
            bias_a = jnp.broadcast_to(dwb_ref[:, ls], (SUBLANES, LANES))
            _seg_conv(colh(u_ref, 0, ls) * jax.nn.sigmoid(colh(u_ref, 1, ls)), ea, dww_ref, bias_a, CONF_KERNEL,
                      cbuf, fresh, ls)
            c = cbuf[:, ls]
            gm = gm_ref[ls, ls]
            d = c - _group_mean(c, gm)
            var = _group_mean(d * d, gm)
            if len(ahead) > 2:
                ahead.pop(0)()
            yn = d * lax.rsqrt(var + GN_EPS) * gng_ref[:, ls] + gnb_ref[:, ls]
            ya_seg.append((yn * jax.nn.sigmoid(yn)).astype(BF16))

        for ls in halves:
            _seg_conv(colh(u_ref, 3, ls) * colh(u_ref, 4, ls), eb, scw_ref, None, SCONV_KERNEL, cbuf, fresh, ls)
            yb_seg.append((colh(u_ref, 2, ls) * cbuf[:, ls]).astype(BF16))

        for ls in halves:
            bias_d = jnp.broadcast_to(lcb_ref[:, ls], (SUBLANES, LANES))
            _seg_conv(colh(u_ref, 5, ls), ed, lcw_ref, bias_d, LRU_CONV, cbuf, fresh, ls)
            xc = cbuf[:, ls]
            xc_b = xc.astype(BF16)
            gs = slice(GROUP_W + ls.start, GROUP_W + ls.stop)
            r = jax.nn.sigmoid(_dot(xc_b, wax_ref[ls, ls]) + bax_ref[:, ls])
            i = jax.nn.sigmoid(_dot(xc_b, wax_ref[ls, gs]) + bax_ref[:, gs])
            while ahead:
                ahead.pop(0)()
            lam = lam_ref[:, ls]
            log_a = (-LRU_C * (jnp.maximum(-lam, 0.0) + jnp.log1p(jnp.exp(-jnp.abs(lam))))) * r
            abuf[:, ls] = jnp.exp(log_a)
            th = jnp.tanh(log_a)
            bbuf[:, ls] = jnp.sqrt(-2.0 * th / (1.0 - th)) * (i * xc)
            h_run = bbuf[_grp(0), ls]
            a_run = abuf[_grp(0), ls]
            hloc[_grp(0), ls] = h_run
            acum[_grp(0), ls] = a_run
            for s in range(1, SEG):
                a_s = abuf[_grp(s), ls]
                h_run = a_s * h_run + bbuf[_grp(s), ls]
                a_run = a_s * a_run
                hloc[_grp(s), ls] = h_run
                acum[_grp(s), ls] = a_run
            dist = 1
            while dist < SUBLANES:
                keep = sub >= dist
                h_run = jnp.where(keep, a_run * pltpu.roll(h_run, dist, axis=0) + h_run, h_run)
                a_run = jnp.where(keep, a_run * pltpu.roll(a_run, dist, axis=0), a_run)
                dist *= 2
            h_in = jnp.where(fresh, 0.0, hcar[:, ls])
            seg_end = a_run * h_in + h_run
            endb[:, ls] = seg_end
            seg_in = jnp.where(sub == 0, h_in, pltpu.roll(seg_end, 1, axis=0))
            hcar[:, ls] = jnp.broadcast_to(endb[SUBLANES - 1:SUBLANES, ls], (SUBLANES, LANES))
            for s in range(SEG):
                hloc[_grp(s), ls] = hloc[_grp(s), ls] + acum[_grp(s), ls] * seg_in
            g = colh(u_ref, 6, ls)
            gelu = 0.5 * g * (1.0 + jnp.tanh(math.sqrt(2.0 / math.pi) * (g + 0.044715 * (g * g * g))))
            yd_seg.append((hloc[:, ls] * gelu).astype(BF16))
        for y_ref, y_seg in ((ya_ref, ya_seg), (yb_ref, yb_seg), (yd_ref, yd_seg)):
            y_ref[rows, :] = _dot(from_seg, jnp.concatenate(y_seg, axis=1)).astype(y_ref.dtype)

    for run in projection(0, u_even):
        run()

    def two_units(jj, carry):
        mix(2 * jj, u_even, projection(2 * jj + 1, u_odd))
        mix(2 * jj + 1, u_odd, projection(jnp.minimum(2 * jj + 2, n_units - 1), u_even))
        return carry

    lax.fori_loop(0, n_units // 2, two_units, 0)


def _mixers(h3, w_in, p, layer):
    bsz, seq, dm = h3.shape
    assert (seq // UNIT) % 2 == 0 and bsz % MIX_ROWS == 0
    steps, rows = bsz // MIX_ROWS, MIX_ROWS * seq
    h3 = h3.reshape(steps, rows, dm)
    hi_cols = IN_COLS - ATT_COL1
    assert ATT_COL1 % hi_cols == 0

    def par(shape):
        return pl.BlockSpec((None,) + shape, lambda b: (layer,) + (0,) * len(shape))

    out_spec = pl.BlockSpec((None, rows, GROUP_W), lambda b: (b, 0, 0))
    out_sds = jax.ShapeDtypeStruct((steps, rows, GROUP_W), BF16)
    outs = pl.pallas_call(
        _mixers_kernel,
        grid=(steps,),
        in_specs=[pl.BlockSpec((None, rows, dm), lambda b: (b, 0, 0)),
                  pl.BlockSpec((dm, ATT_COL0), lambda b: (0, 0), pipeline_mode=pl.Buffered(1)),
                  pl.BlockSpec((dm, hi_cols), lambda b: (0, ATT_COL1 // hi_cols), pipeline_mode=pl.Buffered(1)),
                  par((CONF_KERNEL, SUBLANES, GROUP_W)), par((1, GROUP_W)), par((1, GROUP_W)), par((1, GROUP_W)),
                  par((SCONV_KERNEL, SUBLANES, GROUP_W)), par((LRU_CONV, SUBLANES, GROUP_W)), par((1, GROUP_W)),
                  par((GROUP_W, 2 * GROUP_W)), par((1, 2 * GROUP_W)), par((1, GROUP_W)),
                  pl.BlockSpec((GROUP_W, GROUP_W), lambda b: (0, 0))],
        out_specs=[out_spec, out_spec, out_spec],
        out_shape=[out_sds, out_sds, out_sds],
        scratch_shapes=(
            [pltpu.VMEM((UNIT, MIX_COLS), F32)] * 2
            + [pltpu.VMEM((2 * UNIT, GROUP_W), F32)] * 3
            + [pltpu.VMEM((UNIT, GROUP_W), F32)] * 5
            + [pltpu.VMEM((SUBLANES, GROUP_W), F32)] * 2
        ),
        compiler_params=pltpu.CompilerParams(
            dimension_semantics=("parallel",), vmem_limit_bytes=VMEM_LIMIT),
        name="mixers",
    )(h3, w_in, w_in,
      p["dww"], p["dwb"], p["gng"], p["gnb"], p["scw"], p["lcw"], p["lcb"],
      p["wax"], p["bax"], p["lam"], p["gm"])
    return [y.reshape(bsz, seq, GROUP_W) for y in outs]


_NT = (((1,), (1,)), ((), ()))


def _attn_kernel(n_cast, q_ref, k_ref, v_ref, *rest):
    w_refs, o_ref, w_b_refs = rest[:n_cast], rest[n_cast], rest[n_cast + 1:2 * n_cast + 1]
    kb, vt, kmm, gate_s, cmask, pos_q, rbias, qm, s_buf, acc = rest[2 * n_cast + 1:]
    for w_ref, w_b_ref in zip(w_refs, w_b_refs):
        w_b_ref[...] = w_ref[...].astype(BF16)
    nb = kb.shape[0]
    blk = MOBA_BLOCK
    heads = range(ATT_HEADS)
    log2e = math.log2(math.e)
    slopes2 = [log2e * 2.0 ** (-8.0 * (h + 1) / ATT_HEADS) for h in heads]
    lane_head = lax.broadcasted_iota(jnp.int32, (1, GROUP_W), 1) // HEAD_DIM
    pos = lambda rho: rho

    pos_s = pos(lax.broadcasted_iota(jnp.int32, (1, blk), 1))
    rel = (pos_s - (blk - 1)).astype(F32)
    first_row = lax.broadcasted_iota(jnp.int32, (V_ROWS - HEAD_DIM, blk), 0) == 0
    for n in range(nb):
        kblk = k_ref[n * blk:(n + 1) * blk, :]
        kb[n] = kblk.astype(BF16)
        v_t = v_ref[n * blk:(n + 1) * blk, :].T
        kmean = jnp.mean(kblk, axis=0, keepdims=True)
        for h in heads:
            kmm[h * nb + n:h * nb + n + 1, :] = jnp.where(lane_head == h, kmean, 0.0)
            f = jnp.exp2(slopes2[h] * rel)
            vt[n, h * V_ROWS:h * V_ROWS + HEAD_DIM, :] = (v_t[h * HEAD_DIM:(h + 1) * HEAD_DIM, :] * f).astype(BF16)
            vt[n, h * V_ROWS + HEAD_DIM:(h + 1) * V_ROWS, :] = jnp.where(first_row, f, 0.0).astype(BF16)
    t_l = pos(lax.broadcasted_iota(jnp.int32, (blk, blk), 1))
    s_l = pos(lax.broadcasted_iota(jnp.int32, (blk, blk), 0))
    cmask[...] = jnp.where(t_l >= s_l, 0.0, NEG_INF)
    pos_q[...] = pos(lax.broadcasted_iota(jnp.int32, (nb, blk), 1)).astype(F32)
    blk_id = lax.broadcasted_iota(jnp.int32, (nb, blk), 0)

    def prepare(i):
        par = i % 2
        q = q_ref[i * blk:(i + 1) * blk, :] * (HEAD_DIM ** -0.5 * log2e)
        for h in heads:
            qm[par, h] = jnp.where(lane_head == h, q, 0.0).astype(BF16)
        gate_s[...] = lax.dot_general(kmm[...], q, _NT, precision=lax.Precision.HIGHEST,
                                      preferred_element_type=F32)
        dist = (i - blk_id).astype(F32) * blk + (pos_q[...] - (blk - 1))
        for h in heads:
            g = gate_s[h * nb:(h + 1) * nb, :]
            rank = jnp.zeros((nb, blk), jnp.int32)
            for m in range(i):
                gm = gate_s[h * nb + m:h * nb + m + 1, :]
                rank = rank + jnp.where((gm > g) | ((gm == g) & (m < blk_id)), 1, 0)
            visible = (blk_id == i) | ((blk_id < i) & (rank < MOBA_TOP_K))
            rbias[par, h] = jnp.where(visible, -slopes2[h] * dist, NEG_INF)

    def score(k, i, n, m_run):
        out = []
        for h in heads:
            s_t = lax.dot_general(kb[n], qm[i % 2, h], _NT, preferred_element_type=F32)
            if n == i:
                s_t = s_t + cmask[...]
            s_buf[k % 2, h] = s_t
            m_blk = jnp.max(s_t, axis=0, keepdims=True) + rbias[i % 2, h, n:n + 1, :]
            out.append(m_blk if n == i else jnp.maximum(m_run[h], m_blk))
        return out

    def fold(k, i, n, m_now, m_acc):
        for h in heads:
            p = jnp.exp2(s_buf[k % 2, h] - (m_now[h] - rbias[i % 2, h, n:n + 1, :]))
            pv = _dot(vt[n, h * V_ROWS:(h + 1) * V_ROWS, :], p.astype(BF16))
            acc[i % 2, h] = pv if n == i else jnp.exp2(m_acc[h] - m_now[h]) * acc[i % 2, h] + pv

    def finalize(i):
        out_t = jnp.concatenate(
            [acc[i % 2, h, 0:HEAD_DIM, :] / acc[i % 2, h, HEAD_DIM:HEAD_DIM + 1, :] for h in heads], axis=0)
        o_ref[i * blk:(i + 1) * blk, :] = out_t.T.astype(o_ref.dtype)

    units = [(i, n) for i in range(nb) for n in [i] + list(range(i))]
    prepare(0)
    m_run = score(0, 0, 0, None)
    prepare(1)
    m_fold = None
    for k in range(1, len(units) + 1):
        pi, pn = units[k - 1]
        m_prev = m_run
        if k < len(units):
            i, n = units[k]
            m_run = score(k, i, n, m_run)
        fold(k - 1, pi, pn, m_prev, m_fold)
        m_fold = m_prev
        if pn == pi - 1 or pi == 0:
            finalize(pi)
        if k < len(units) and n == i and i + 1 < nb:
            prepare(i + 1)


def _attention(u3, weights):
    bsz, seq, _ = u3.shape
    nb = seq // MOBA_BLOCK
    blk = MOBA_BLOCK

    def ucol(j):
        return pl.BlockSpec((None, seq, GROUP_W), lambda b: (b, 0, j))

    def slab_in(w, layer):
        return pl.BlockSpec((None, w.shape[1] // bsz, w.shape[2]), lambda b: (layer, b, 0))

    def slab_out(w):
        return pl.BlockSpec((w.shape[1] // bsz, w.shape[2]), lambda b: (b, 0))

    return pl.pallas_call(
        functools.partial(_attn_kernel, len(weights)),
        grid=(bsz,),
        in_specs=[ucol(0), ucol(1), ucol(2)] + [slab_in(w, layer) for w, layer in weights],
        out_specs=[pl.BlockSpec((None, seq, GROUP_W), lambda b: (b, 0, 0))] + [slab_out(w) for w, _ in weights],
        out_shape=[jax.ShapeDtypeStruct((bsz, seq, GROUP_W), BF16)]
        + [jax.ShapeDtypeStruct(w.shape[1:], BF16) for w, _ in weights],
        scratch_shapes=[
            pltpu.VMEM((nb, blk, GROUP_W), BF16),
            pltpu.VMEM((nb, ATT_HEADS * V_ROWS, blk), BF16),
            pltpu.VMEM((ATT_HEADS * nb, GROUP_W), F32),
            pltpu.VMEM((ATT_HEADS * nb, blk), F32),
            pltpu.VMEM((blk, blk), F32),
            pltpu.VMEM((nb, blk), F32),
            pltpu.VMEM((2, ATT_HEADS, nb, blk), F32),
            pltpu.VMEM((2, ATT_HEADS, blk, GROUP_W), BF16),
            pltpu.VMEM((2, ATT_HEADS, blk, blk), F32),
            pltpu.VMEM((2, ATT_HEADS, V_ROWS, blk), F32),
        ],
        compiler_params=pltpu.CompilerParams(
            dimension_semantics=("parallel",), vmem_limit_bytes=VMEM_LIMIT),
        name="moba_attention",
    )(u3, u3, u3, *[w for w, _ in weights])


def _out_mlp_kernel(with_next, ya_ref, yb_ref, yc_ref, yd_ref, x_ref, wo_ref, gpost_ref, gpre_ref, w1_ref, w2_ref,
                    gmlp_ref, *rest):
    if with_next:
        gnext_ref, wq_ref, wk_ref, wv_ref, xo_ref, u_ref, hseg_ref = rest
    else:
        (xo_ref,) = rest
    half = x_ref.shape[0] // 2
    halves = [slice(0, half), slice(half, 2 * half)]

    def out_projection(rows):
        y = None
        for j, y_ref in enumerate((ya_ref, yb_ref, yc_ref, yd_ref)):
            term = _dot(y_ref[rows, :], wo_ref[j * GROUP_W:(j + 1) * GROUP_W, :])
            y = term if y is None else y + term
        return y

    def mlp(h):
        m = None
        for c in range(0, D_FF, FF_CHUNK):
            t = jnp.maximum(_dot(h, w1_ref[:, c:c + FF_CHUNK]), 0.0)
            term = _dot((t * t).astype(BF16), w2_ref[c:c + FF_CHUNK, :])
            m = term if m is None else m + term
        return m

    ys = [out_projection(rows) for rows in halves]
    for rows, y in zip(halves, ys):
        xo_ref[rows, :] = x_ref[rows, :] + _rms(y, gpost_ref[...])
        m = mlp(_rms(xo_ref[rows, :], gpre_ref[...]).astype(BF16))
        xo_ref[rows, :] = xo_ref[rows, :] + _rms(m, gmlp_ref[...])
    if with_next:
        for rows in halves:
            _in_proj_tile(xo_ref[rows, :], gnext_ref, (wq_ref, wk_ref, wv_ref), u_ref, hseg_ref, rows.start)


def _out_mlp(ya, yb, yc, yd, x2, wo, gpost, gpre, w1, w2, gmlp, layer, w_in=None, gnext=None):
    tok = x2.shape[0]
    tm = TOKEN_TILE
    with_next = w_in is not None
    yspec = pl.BlockSpec((tm, GROUP_W), lambda t: (t, 0))
    xspec = pl.BlockSpec((tm, D_MODEL), lambda t: (t, 0))
    gspec = pl.BlockSpec((None, 1, D_MODEL), lambda t: (layer, 0, 0))

    def resident(shape):
        return pl.BlockSpec(shape, lambda t: (0, 0), pipeline_mode=pl.Buffered(1))

    in_specs = [yspec, yspec, yspec, yspec, xspec, resident((D_MODEL, D_MODEL)), gspec, gspec,
                resident((D_MODEL, D_FF)), resident((D_FF, D_MODEL)), gspec]
    operands = [ya, yb, yc, yd, x2, wo, gpost, gpre, w1, w2, gmlp]
    out_specs = [xspec]
    out_shape = [jax.ShapeDtypeStruct((tok, D_MODEL), F32)]
    if with_next:
        cols = ATT_COL1 - ATT_COL0
        in_specs += [pl.BlockSpec((None, 1, D_MODEL), lambda t: (layer + 1, 0, 0))]
        in_specs += [pl.BlockSpec((D_MODEL, GROUP_W), lambda t, j=j: (0, ATT_COL0 // GROUP_W + j),
                                  pipeline_mode=pl.Buffered(1)) for j in range(3)]
        operands += [gnext, w_in, w_in, w_in]
        out_specs += [pl.BlockSpec((tm, cols), lambda t: (t, 0)), xspec]
        out_shape += [jax.ShapeDtypeStruct((tok, cols), F32), jax.ShapeDtypeStruct((tok, D_MODEL), BF16)]
    outs = pl.pallas_call(
        functools.partial(_out_mlp_kernel, with_next),
        grid=(tok // tm,),
        in_specs=in_specs,
        out_specs=out_specs,
        out_shape=out_shape,
        compiler_params=pltpu.CompilerParams(
            dimension_semantics=("parallel",), vmem_limit_bytes=VMEM_LIMIT_TAIL),
        name="out_mlp",
    )(*operands)
    return outs if with_next else outs[0]


def _block_diag_dense(w):
    depth, nblk, n, _ = w.shape
    eye = jnp.eye(nblk, dtype=w.dtype)
    return jnp.einsum("dgij,gh->dgihj", w, eye).reshape(depth, nblk * n, nblk * n)


def kernel(x, pre_mix_g, w_in, conf_dw_w, conf_dw_b, conf_gn_g, conf_gn_b, sconv_w, lru_conv_w, lru_conv_b,
           lru_wa, lru_ba, lru_wx, lru_bx, lru_lam, w_out, post_mix_g, pre_mlp_g, mlp_w1, mlp_w2, post_mlp_g):
    bsz, seq, dm = x.shape
    depth = w_in.shape[0]
    assert dm == D_MODEL and seq % MOBA_BLOCK == 0 and (bsz * seq) % TOKEN_TILE == 0
    assert seq // MOBA_BLOCK > MOBA_TOP_K
    assert D_MODEL % (BF16_TILE_ROWS * bsz) == 0

    row = lambda a: a.reshape(depth, 1, a.shape[-1])
    taps = lambda w: jnp.broadcast_to(w[:, :, None, :], w.shape[:2] + (SUBLANES, w.shape[-1]))
    group_of = jnp.arange(GROUP_W) // (GROUP_W // CONF_GROUPS)
    params = {
        "dww": taps(conf_dw_w), "dwb": row(conf_dw_b), "gng": row(conf_gn_g), "gnb": row(conf_gn_b),
        "scw": taps(sconv_w), "lcw": taps(lru_conv_w), "lcb": row(lru_conv_b),
        "wax": jnp.concatenate([_block_diag_dense(lru_wa), _block_diag_dense(lru_wx)], axis=-1).astype(BF16),
        "bax": row(jnp.concatenate([lru_ba, lru_bx], axis=-1)),
        "lam": row(lru_lam),
        "gm": ((group_of[:, None] == group_of[None, :]).astype(F32) / (GROUP_W // CONF_GROUPS)).astype(BF16),
    }
    w_in_b = w_in[0].astype(BF16)
    g_pre_mix, g_post_mix = row(pre_mix_g), row(post_mix_g)
    g_pre_mlp, g_post_mlp = row(pre_mlp_g), row(post_mlp_g)

    x2 = x.reshape(bsz * seq, dm)
    u_att, h_seg = _in_proj(x2, g_pre_mix, w_in_b, 0)
    for layer in range(depth):
        ya, yb, yd = _mixers(h_seg.reshape(bsz, seq, dm), w_in_b, params, layer)
        to_cast = [(w_out, layer), (mlp_w1, layer), (mlp_w2, layer)] + ([(w_in, layer + 1)] if layer + 1 < depth else [])
        yc, w_out_b, w1_b, w2_b, *w_in_next = _attention(u_att.reshape(bsz, seq, ATT_COL1 - ATT_COL0), to_cast)
        flat = lambda a: a.reshape(bsz * seq, GROUP_W)
        args = (flat(ya), flat(yb), flat(yc), flat(yd), x2, w_out_b, g_post_mix, g_pre_mlp, w1_b, w2_b, g_post_mlp)
        if layer + 1 < depth:
            (w_in_b,) = w_in_next
            x2, u_att, h_seg = _out_mlp(*args, layer, w_in=w_in_b, gnext=g_pre_mix)
        else:
            x2 = _out_mlp(*args, layer)
    return x2.reshape(bsz, seq, dm)
```

```python
import functools
import math

import jax
import jax.numpy as jnp
from jax import lax
from jax.experimental import pallas as pl
from jax.experimental.pallas import tpu as pltpu

F32 = jnp.float32
BF16 = jnp.bfloat16

D_MODEL = 1024
GROUP_W = 256
CONF_KERNEL = 31
CONF_GROUPS = 4
GN_EPS = 1e-5
SCONV_KERNEL = 3
ATT_HEADS = 4
HEAD_DIM = GROUP_W // ATT_HEADS
MOBA_BLOCK = 256
MOBA_TOP_K = 3
LRU_BLOCKS = 4
LRU_CONV = 4
LRU_C = 8.0
D_FF = 4 * D_MODEL
RMS_EPS = 1e-6
IN_COLS = 10 * GROUP_W
ATT_COL0, ATT_COL1 = 5 * GROUP_W, 8 * GROUP_W
MIX_COLS = IN_COLS - (ATT_COL1 - ATT_COL0)

SUBLANES = 8
LANES = 128
BF16_TILE_ROWS = 16

TOKEN_TILE = 1024
TILE_PARTS = 4
UNIT = MOBA_BLOCK
MIX_ROWS = 2
SEG = UNIT // SUBLANES
CONV_GROUPS = 8
FF_CHUNK = 1024
V_ROWS = HEAD_DIM + BF16_TILE_ROWS
VMEM_LIMIT = 56 * 1024 * 1024
VMEM_LIMIT_TAIL = 62 * 1024 * 1024

NEG_INF = float("-inf")


def _rms(x, g):
    return x * lax.rsqrt(jnp.mean(x * x, axis=-1, keepdims=True) + RMS_EPS) * g


def _dot(a, b):
    return jnp.dot(a, b, preferred_element_type=F32)


def _grp(a):
    return slice(SUBLANES * a, SUBLANES * (a + 1))


def _segment_permutations():
    row = lax.broadcasted_iota(jnp.int32, (UNIT, UNIT), 0)
    col = lax.broadcasted_iota(jnp.int32, (UNIT, UNIT), 1)
    token_of = lambda rho: SEG * (rho & (SUBLANES - 1)) + (rho >> (SUBLANES.bit_length() - 1))
    to_seg = jnp.where(col == token_of(row), 1.0, 0.0).astype(BF16)
    from_seg = jnp.where(row == token_of(col), 1.0, 0.0).astype(BF16)
    return to_seg, from_seg


def _in_proj_tile(x, g_ref, w_refs, u_ref, hseg_ref, row0):
    rows = x.shape[0]
    h = _rms(x, g_ref[...]).astype(BF16)
    for j, w_ref in enumerate(w_refs):
        u_ref[row0:row0 + rows, j * GROUP_W:(j + 1) * GROUP_W] = _dot(h, w_ref[...])
    to_seg, _ = _segment_permutations()
    for r in range(0, rows, UNIT):
        hseg_ref[row0 + r:row0 + r + UNIT, :] = _dot(to_seg, h[r:r + UNIT, :]).astype(BF16)


def _in_proj_kernel(x_ref, g_ref, wq_ref, wk_ref, wv_ref, u_ref, hseg_ref):
    _in_proj_tile(x_ref[...], g_ref, (wq_ref, wk_ref, wv_ref), u_ref, hseg_ref, 0)


def _in_proj(x2, g, w, layer):
    tok = x2.shape[0]
    cols = ATT_COL1 - ATT_COL0

    def wcol(j):
        return pl.BlockSpec((D_MODEL, GROUP_W), lambda t: (0, ATT_COL0 // GROUP_W + j))

    xspec = pl.BlockSpec((TOKEN_TILE, D_MODEL), lambda t: (t, 0))
    return pl.pallas_call(
        _in_proj_kernel,
        grid=(tok // TOKEN_TILE,),
        in_specs=[xspec, pl.BlockSpec((None, 1, D_MODEL), lambda t: (layer, 0, 0)), wcol(0), wcol(1), wcol(2)],
        out_specs=[pl.BlockSpec((TOKEN_TILE, cols), lambda t: (t, 0)), xspec],
        out_shape=[jax.ShapeDtypeStruct((tok, cols), F32), jax.ShapeDtypeStruct((tok, D_MODEL), BF16)],
        compiler_params=pltpu.CompilerParams(
            dimension_semantics=("parallel",), vmem_limit_bytes=VMEM_LIMIT),
        name="in_proj",
    )(x2, g, w, w, w)


def _group_mean(v, gm):
    hi = v.astype(BF16)
    lo = (v - hi.astype(F32)).astype(BF16)
    return _dot(hi, gm) + _dot(lo, gm)


def _seg_conv(z, e_ref, w_ref, bias, width, out_ref, fresh, ls):
    sub = lax.broadcasted_iota(jnp.int32, (SUBLANES, LANES), 0)
    for a in range(SEG - (width - 1), SEG):
        cur = z[_grp(a), :]
        prev = jnp.where(fresh, 0.0, e_ref[_grp(SEG + a), ls])
        e_ref[_grp(a), ls] = pltpu.roll(jnp.where(sub == SUBLANES - 1, prev, cur), 1, axis=0)
    e_ref[UNIT:2 * UNIT, ls] = z
    for a0 in range(0, SEG, CONV_GROUPS):
        accs = [bias] * CONV_GROUPS
        for j in range(width):
            w_j = w_ref[width - 1 - j, :, ls]
            for a in range(a0, a0 + CONV_GROUPS):
                term = w_j * e_ref[_grp(SEG + a - j), ls]
                accs[a - a0] = term if accs[a - a0] is None else accs[a - a0] + term
        for a in range(a0, a0 + CONV_GROUPS):
            out_ref[_grp(a), ls] = accs[a - a0]


def _mixers_kernel(h_ref, win_lo_ref, win_hi_ref,
                   dww_ref, dwb_ref, gng_ref, gnb_ref, scw_ref, lcw_ref, lcb_ref,
                   wax_ref, bax_ref, lam_ref, gm_ref,
                   ya_ref, yb_ref, yd_ref,
                   u_even, u_odd, ea, eb, ed, cbuf, abuf, bbuf, hloc, acum, endb, hcar):
    _, from_seg = _segment_permutations()
    n_units = h_ref.shape[0] // UNIT
    units_per_seq = n_units // MIX_ROWS
    for e in (ea, eb, ed):
        e[UNIT:2 * UNIT, :] = jnp.zeros((UNIT, GROUP_W), F32)
    hcar[...] = jnp.zeros_like(hcar)


    def projection(u, u_ref):
        rows = pl.ds(pl.multiple_of(u * UNIT, UNIT), UNIT)
        step = 2 * GROUP_W

        def lo(c):
            def run():
                w = min(step, ATT_COL0 - c)
                u_ref[:, c:c + w] = _dot(h_ref[rows, :], win_lo_ref[:, c:c + w])
            return run

        def hi():
            u_ref[:, ATT_COL0:MIX_COLS] = _dot(h_ref[rows, :], win_hi_ref[...])

        return [lo(c) for c in range(0, ATT_COL0, step)] + [hi]

    def colh(u_ref, j, ls):
        return u_ref[:, j * GROUP_W + ls.start:j * GROUP_W + ls.stop]

    def mix(u, u_ref, ahead):
        rows = pl.ds(pl.multiple_of(u * UNIT, UNIT), UNIT)
        fresh = u % units_per_seq == 0
        ahead = list(ahead)
        ahead.pop(0)()
        halves = [slice(h * LANES, (h + 1) * LANES) for h in range(GROUP_W // LANES)]
        sub = lax.broadcasted_iota(jnp.int32, (SUBLANES, LANES), 0)
        ya_seg, yb_seg, yd_seg = [], [], []

        for ls in halves:
            bias_a = jnp.broadcast_to(dwb_ref[:, ls], (SUBLANES, LANES))
            _seg_conv(colh(u_ref, 0, ls) * jax.nn.sigmoid(colh(u_ref, 1, ls)), ea, dww_ref, bias_a, CONF_KERNEL,
                      cbuf, fresh, ls)
            c = cbuf[:, ls]
            gm = gm_ref[ls, ls]
            d = c - _group_mean(c, gm)
            var = _group_mean(d * d, gm)
            if len(ahead) > 2:
                ahead.pop(0)()
            yn = d * lax.rsqrt(var + GN_EPS) * gng_ref[:, ls] + gnb_ref[:, ls]
            ya_seg.append((yn * jax.nn.sigmoid(yn)).astype(BF16))

        for ls in halves:
            _seg_conv(colh(u_ref, 3, ls) * colh(u_ref, 4, ls), eb, scw_ref, None, SCONV_KERNEL, cbuf, fresh, ls)
            yb_seg.append((colh(u_ref, 2, ls) * cbuf[:, ls]).astype(BF16))

        for ls in halves:
            bias_d = jnp.broadcast_to(lcb_ref[:, ls], (SUBLANES, LANES))
            _seg_conv(colh(u_ref, 5, ls), ed, lcw_ref, bias_d, LRU_CONV, cbuf, fresh, ls)
            xc = cbuf[:, ls]
            xc_b = xc.astype(BF16)
            gs = slice(GROUP_W + ls.start, GROUP_W + ls.stop)
            r = jax.nn.sigmoid(_dot(xc_b, wax_ref[ls, ls]) + bax_ref[:, ls])
            i = jax.nn.sigmoid(_dot(xc_b, wax_ref[ls, gs]) + bax_ref[:, gs])
            while ahead:
                ahead.pop(0)()
            lam = lam_ref[:, ls]
            log_a = (-LRU_C * (jnp.maximum(-lam, 0.0) + jnp.log1p(jnp.exp(-jnp.abs(lam))))) * r
            abuf[:, ls] = jnp.exp(log_a)
            th = jnp.tanh(log_a)
            bbuf[:, ls] = jnp.sqrt(-2.0 * th / (1.0 - th)) * (i * xc)
            h_run = bbuf[_grp(0), ls]
            a_run = abuf[_grp(0), ls]
            hloc[_grp(0), ls] = h_run
            acum[_grp(0), ls] = a_run
            for s in range(1, SEG):
                a_s = abuf[_grp(s), ls]
                h_run = a_s * h_run + bbuf[_grp(s), ls]
                a_run = a_s * a_run
                hloc[_grp(s), ls] = h_run
                acum[_grp(s), ls] = a_run
            dist = 1
            while dist < SUBLANES:
                keep = sub >= dist
                h_run = jnp.where(keep, a_run * pltpu.roll(h_run, dist, axis=0) + h_run, h_run)
                a_run = jnp.where(keep, a_run * pltpu.roll(a_run, dist, axis=0), a_run)
                dist *= 2
            h_in = jnp.where(fresh, 0.0, hcar[:, ls])
            seg_end = a_run * h_in + h_run
            endb[:, ls] = seg_end
            seg_in = jnp.where(sub == 0, h_in, pltpu.roll(seg_end, 1, axis=0))
            hcar[:, ls] = jnp.broadcast_to(endb[SUBLANES - 1:SUBLANES, ls], (SUBLANES, LANES))
            for s in range(SEG):
                hloc[_grp(s), ls] = hloc[_grp(s), ls] + acum[_grp(s), ls] * seg_in
            g = colh(u_ref, 6, ls)
            gelu = 0.5 * g * (1.0 + jnp.tanh(math.sqrt(2.0 / math.pi) * (g + 0.044715 * (g * g * g))))
            yd_seg.append((hloc[:, ls] * gelu).astype(BF16))
        for y_ref, y_seg in ((ya_ref, ya_seg), (yb_ref, yb_seg), (yd_ref, yd_seg)):
            y_ref[rows, :] = _dot(from_seg, jnp.concatenate(y_seg, axis=1)).astype(y_ref.dtype)

    for run in projection(0, u_even):
        run()

    def two_units(jj, carry):
        mix(2 * jj, u_even, projection(2 * jj + 1, u_odd))
        mix(2 * jj + 1, u_odd, projection(jnp.minimum(2 * jj + 2, n_units - 1), u_even))
        return carry

    lax.fori_loop(0, n_units // 2, two_units, 0)


def _mixers(h3, w_in, p, layer):
    bsz, seq, dm = h3.shape
    assert (seq // UNIT) % 2 == 0 and bsz % MIX_ROWS == 0
    steps, rows = bsz // MIX_ROWS, MIX_ROWS * seq
    h3 = h3.reshape(steps, rows, dm)
    hi_cols = IN_COLS - ATT_COL1
    assert ATT_COL1 % hi_cols == 0

    def par(shape):
        return pl.BlockSpec((None,) + shape, lambda b: (layer,) + (0,) * len(shape))

    out_spec = pl.BlockSpec((None, rows, GROUP_W), lambda b: (b, 0, 0))
    out_sds = jax.ShapeDtypeStruct((steps, rows, GROUP_W), BF16)
    outs = pl.pallas_call(
        _mixers_kernel,
        grid=(steps,),
        in_specs=[pl.BlockSpec((None, rows, dm), lambda b: (b, 0, 0)),
                  pl.BlockSpec((dm, ATT_COL0), lambda b: (0, 0), pipeline_mode=pl.Buffered(1)),
                  pl.BlockSpec((dm, hi_cols), lambda b: (0, ATT_COL1 // hi_cols), pipeline_mode=pl.Buffered(1)),
                  par((CONF_KERNEL, SUBLANES, GROUP_W)), par((1, GROUP_W)), par((1, GROUP_W)), par((1, GROUP_W)),
                  par((SCONV_KERNEL, SUBLANES, GROUP_W)), par((LRU_CONV, SUBLANES, GROUP_W)), par((1, GROUP_W)),
                  par((GROUP_W, 2 * GROUP_W)), par((1, 2 * GROUP_W)), par((1, GROUP_W)),
                  pl.BlockSpec((GROUP_W, GROUP_W), lambda b: (0, 0))],
        out_specs=[out_spec, out_spec, out_spec],
        out_shape=[out_sds, out_sds, out_sds],
        scratch_shapes=(
            [pltpu.VMEM((UNIT, MIX_COLS), F32)] * 2
            + [pltpu.VMEM((2 * UNIT, GROUP_W), F32)] * 3
            + [pltpu.VMEM((UNIT, GROUP_W), F32)] * 5
            + [pltpu.VMEM((SUBLANES, GROUP_W), F32)] * 2
        ),
        compiler_params=pltpu.CompilerParams(
            dimension_semantics=("parallel",), vmem_limit_bytes=VMEM_LIMIT),
        name="mixers",
    )(h3, w_in, w_in,
      p["dww"], p["dwb"], p["gng"], p["gnb"], p["scw"], p["lcw"], p["lcb"],
      p["wax"], p["bax"], p["lam"], p["gm"])
    return [y.reshape(bsz, seq, GROUP_W) for y in outs]


_NT = (((1,), (1,)), ((), ()))


def _attn_kernel(n_cast, q_ref, k_ref, v_ref, *rest):
    w_refs, o_ref, w_b_refs = rest[:n_cast], rest[n_cast], rest[n_cast + 1:2 * n_cast + 1]
    kb, vt, kmm, gate_s, cmask, pos_q, rbias, qm, s_buf, acc = rest[2 * n_cast + 1:]
    for w_ref, w_b_ref in zip(w_refs, w_b_refs):
        w_b_ref[...] = w_ref[...].astype(BF16)
    nb = kb.shape[0]
    blk = MOBA_BLOCK
    heads = range(ATT_HEADS)
    log2e = math.log2(math.e)
    slopes2 = [log2e * 2.0 ** (-8.0 * (h + 1) / ATT_HEADS) for h in heads]
    lane_head = lax.broadcasted_iota(jnp.int32, (1, GROUP_W), 1) // HEAD_DIM
    pos = lambda rho: rho

    pos_s = pos(lax.broadcasted_iota(jnp.int32, (1, blk), 1))
    rel = (pos_s - (blk - 1)).astype(F32)
    first_row = lax.broadcasted_iota(jnp.int32, (V_ROWS - HEAD_DIM, blk), 0) == 0
    for n in range(nb):
        kblk = k_ref[n * blk:(n + 1) * blk, :]
        kb[n] = kblk.astype(BF16)
        v_t = v_ref[n * blk:(n + 1) * blk, :].T
        kmean = jnp.mean(kblk, axis=0, keepdims=True)
        for h in heads:
            kmm[h * nb + n:h * nb + n + 1, :] = jnp.where(lane_head == h, kmean, 0.0)
            f = jnp.exp2(slopes2[h] * rel)
            vt[n, h * V_ROWS:h * V_ROWS + HEAD_DIM, :] = (v_t[h * HEAD_DIM:(h + 1) * HEAD_DIM, :] * f).astype(BF16)
            vt[n, h * V_ROWS + HEAD_DIM:(h + 1) * V_ROWS, :] = jnp.where(first_row, f, 0.0).astype(BF16)
    t_l = pos(lax.broadcasted_iota(jnp.int32, (blk, blk), 1))
    s_l = pos(lax.broadcasted_iota(jnp.int32, (blk, blk), 0))
    cmask[...] = jnp.where(t_l >= s_l, 0.0, NEG_INF)
    pos_q[...] = pos(lax.broadcasted_iota(jnp.int32, (nb, blk), 1)).astype(F32)
    blk_id = lax.broadcasted_iota(jnp.int32, (nb, blk), 0)

    def prepare(i):
        par = i % 2
        q = q_ref[i * blk:(i + 1) * blk, :] * (HEAD_DIM ** -0.5 * log2e)
        for h in heads:
            qm[par, h] = jnp.where(lane_head == h, q, 0.0).astype(BF16)
        gate_s[...] = lax.dot_general(kmm[...], q, _NT, precision=lax.Precision.HIGHEST,
                                      preferred_element_type=F32)
        dist = (i - blk_id).astype(F32) * blk + (pos_q[...] - (blk - 1))
        for h in heads:
            g = gate_s[h * nb:(h + 1) * nb, :]
            rank = jnp.zeros((nb, blk), jnp.int32)
            for m in range(i):
                gm = gate_s[h * nb + m:h * nb + m + 1, :]
                rank = rank + jnp.where((gm > g) | ((gm == g) & (m < blk_id)), 1, 0)
            visible = (blk_id == i) | ((blk_id < i) & (rank < MOBA_TOP_K))
            rbias[par, h] = jnp.where(visible, -slopes2[h] * dist, NEG_INF)

    def score(k, i, n, m_run):
        out = []
        for h in heads:
            s_t = lax.dot_general(kb[n], qm[i % 2, h], _NT, preferred_element_type=F32)
            if n == i:
                s_t = s_t + cmask[...]
            s_buf[k % 2, h] = s_t
            m_blk = jnp.max(s_t, axis=0, keepdims=True) + rbias[i % 2, h, n:n + 1, :]
            out.append(m_blk if n == i else jnp.maximum(m_run[h], m_blk))
        return out

    def fold(k, i, n, m_now, m_acc):
        for h in heads:
            p = jnp.exp2(s_buf[k % 2, h] - (m_now[h] - rbias[i % 2, h, n:n + 1, :]))
            pv = _dot(vt[n, h * V_ROWS:(h + 1) * V_ROWS, :], p.astype(BF16))
            acc[i % 2, h] = pv if n == i else jnp.exp2(m_acc[h] - m_now[h]) * acc[i % 2, h] + pv

    def finalize(i):
        out_t = jnp.concatenate(
            [acc[i % 2, h, 0:HEAD_DIM, :] / acc[i % 2, h, HEAD_DIM:HEAD_DIM + 1, :] for h in heads], axis=0)
        o_ref[i * blk:(i + 1) * blk, :] = out_t.T.astype(o_ref.dtype)

    units = [(i, n) for i in range(nb) for n in [i] + list(range(i))]
    prepare(0)
    m_run = score(0, 0, 0, None)
    prepare(1)
    m_fold = None
    for k in range(1, len(units) + 1):
        pi, pn = units[k - 1]
        m_prev = m_run
        if k < len(units):
            i, n = units[k]
            m_run = score(k, i, n, m_run)
        fold(k - 1, pi, pn, m_prev, m_fold)
        m_fold = m_prev
        if pn == pi - 1 or pi == 0:
            finalize(pi)
        if k < len(units) and n == i and i + 1 < nb:
            prepare(i + 1)


def _attention(u3, weights):
    bsz, seq, _ = u3.shape
    nb = seq // MOBA_BLOCK
    blk = MOBA_BLOCK

    def ucol(j):
        return pl.BlockSpec((None, seq, GROUP_W), lambda b: (b, 0, j))

    def slab_in(w, layer):
        return pl.BlockSpec((None, w.shape[1] // bsz, w.shape[2]), lambda b: (layer, b, 0))

    def slab_out(w):
        return pl.BlockSpec((w.shape[1] // bsz, w.shape[2]), lambda b: (b, 0))

    return pl.pallas_call(
        functools.partial(_attn_kernel, len(weights)),
        grid=(bsz,),
        in_specs=[ucol(0), ucol(1), ucol(2)] + [slab_in(w, layer) for w, layer in weights],
        out_specs=[pl.BlockSpec((None, seq, GROUP_W), lambda b: (b, 0, 0))] + [slab_out(w) for w, _ in weights],
        out_shape=[jax.ShapeDtypeStruct((bsz, seq, GROUP_W), BF16)]
        + [jax.ShapeDtypeStruct(w.shape[1:], BF16) for w, _ in weights],
        scratch_shapes=[
            pltpu.VMEM((nb, blk, GROUP_W), BF16),
            pltpu.VMEM((nb, ATT_HEADS * V_ROWS, blk), BF16),
            pltpu.VMEM((ATT_HEADS * nb, GROUP_W), F32),
            pltpu.VMEM((ATT_HEADS * nb, blk), F32),
            pltpu.VMEM((blk, blk), F32),
            pltpu.VMEM((nb, blk), F32),
            pltpu.VMEM((2, ATT_HEADS, nb, blk), F32),
            pltpu.VMEM((2, ATT_HEADS, blk, GROUP_W), BF16),
            pltpu.VMEM((2, ATT_HEADS, blk, blk), F32),
            pltpu.VMEM((2, ATT_HEADS, V_ROWS, blk), F32),
        ],
        compiler_params=pltpu.CompilerParams(
            dimension_semantics=("parallel",), vmem_limit_bytes=VMEM_LIMIT),
        name="moba_attention",
    )(u3, u3, u3, *[w for w, _ in weights])


def _out_mlp_kernel(with_next, ya_ref, yb_ref, yc_ref, yd_ref, x_ref, wo_ref, gpost_ref, gpre_ref, w1_ref, w2_ref,
                    gmlp_ref, *rest):
    if with_next:
        gnext_ref, wq_ref, wk_ref, wv_ref, xo_ref, u_ref, hseg_ref = rest
    else:
        (xo_ref,) = rest
    half = x_ref.shape[0] // TILE_PARTS
    halves = [slice(p * half, (p + 1) * half) for p in range(TILE_PARTS)]

    def out_projection(rows):
        y = None
        for j, y_ref in enumerate((ya_ref, yb_ref, yc_ref, yd_ref)):
            term = _dot(y_ref[rows, :], wo_ref[j * GROUP_W:(j + 1) * GROUP_W, :])
            y = term if y is None else y + term
        return y

    def mlp(h):
        m = None
        for c in range(0, D_FF, FF_CHUNK):
            t = jnp.maximum(_dot(h, w1_ref[:, c:c + FF_CHUNK]), 0.0)
            term = _dot((t * t).astype(BF16), w2_ref[c:c + FF_CHUNK, :])
            m = term if m is None else m + term
        return m

    ys = [out_projection(rows) for rows in halves]
    for rows, y in zip(halves, ys):
        xo_ref[rows, :] = x_ref[rows, :] + _rms(y, gpost_ref[...])
        m = mlp(_rms(xo_ref[rows, :], gpre_ref[...]).astype(BF16))
        xo_ref[rows, :] = xo_ref[rows, :] + _rms(m, gmlp_ref[...])
    if with_next:
        for rows in halves:
            _in_proj_tile(xo_ref[rows, :], gnext_ref, (wq_ref, wk_ref, wv_ref), u_ref, hseg_ref, rows.start)


def _out_mlp(ya, yb, yc, yd, x2, wo, gpost, gpre, w1, w2, gmlp, layer, w_in=None, gnext=None):
    tok = x2.shape[0]
    tm = TOKEN_TILE
    with_next = w_in is not None
    yspec = pl.BlockSpec((tm, GROUP_W), lambda t: (t, 0))
    xspec = pl.BlockSpec((tm, D_MODEL), lambda t: (t, 0))
    gspec = pl.BlockSpec((None, 1, D_MODEL), lambda t: (layer, 0, 0))

    def resident(shape):
        return pl.BlockSpec(shape, lambda t: (0, 0), pipeline_mode=pl.Buffered(1))

    in_specs = [yspec, yspec, yspec, yspec, xspec, resident((D_MODEL, D_MODEL)), gspec, gspec,
                resident((D_MODEL, D_FF)), resident((D_FF, D_MODEL)), gspec]
    operands = [ya, yb, yc, yd, x2, wo, gpost, gpre, w1, w2, gmlp]
    out_specs = [xspec]
    out_shape = [jax.ShapeDtypeStruct((tok, D_MODEL), F32)]
    if with_next:
        cols = ATT_COL1 - ATT_COL0
        in_specs += [pl.BlockSpec((None, 1, D_MODEL), lambda t: (layer + 1, 0, 0))]
        in_specs += [pl.BlockSpec((D_MODEL, GROUP_W), lambda t, j=j: (0, ATT_COL0 // GROUP_W + j),
                                  pipeline_mode=pl.Buffered(1)) for j in range(3)]
        operands += [gnext, w_in, w_in, w_in]
        out_specs += [pl.BlockSpec((tm, cols), lambda t: (t, 0)), xspec]
        out_shape += [jax.ShapeDtypeStruct((tok, cols), F32), jax.ShapeDtypeStruct((tok, D_MODEL), BF16)]
    outs = pl.pallas_call(
        functools.partial(_out_mlp_kernel, with_next),
        grid=(tok // tm,),
        in_specs=in_specs,
        out_specs=out_specs,
        out_shape=out_shape,
        compiler_params=pltpu.CompilerParams(
            dimension_semantics=("parallel",), vmem_limit_bytes=VMEM_LIMIT_TAIL),
        name="out_mlp",
    )(*operands)
    return outs if with_next else outs[0]


def _block_diag_dense(w):
    depth, nblk, n, _ = w.shape
    eye = jnp.eye(nblk, dtype=w.dtype)
    return jnp.einsum("dgij,gh->dgihj", w, eye).reshape(depth, nblk * n, nblk * n)


def kernel(x, pre_mix_g, w_in, conf_dw_w, conf_dw_b, conf_gn_g, conf_gn_b, sconv_w, lru_conv_w, lru_conv_b,
           lru_wa, lru_ba, lru_wx, lru_bx, lru_lam, w_out, post_mix_g, pre_mlp_g, mlp_w1, mlp_w2, post_mlp_g):
    bsz, seq, dm = x.shape
    depth = w_in.shape[0]
    assert dm == D_MODEL and seq % MOBA_BLOCK == 0 and (bsz * seq) % TOKEN_TILE == 0
    assert seq // MOBA_BLOCK > MOBA_TOP_K
    assert D_MODEL % (BF16_TILE_ROWS * bsz) == 0

    row = lambda a: a.reshape(depth, 1, a.shape[-1])
    taps = lambda w: jnp.broadcast_to(w[:, :, None, :], w.shape[:2] + (SUBLANES, w.shape[-1]))
    group_of = jnp.arange(GROUP_W) // (GROUP_W // CONF_GROUPS)
    params = {
        "dww": taps(conf_dw_w), "dwb": row(conf_dw_b), "gng": row(conf_gn_g), "gnb": row(conf_gn_b),
        "scw": taps(sconv_w), "lcw": taps(lru_conv_w), "lcb": row(lru_conv_b),
        "wax": jnp.concatenate([_block_diag_dense(lru_wa), _block_diag_dense(lru_wx)], axis=-1).astype(BF16),
        "bax": row(jnp.concatenate([lru_ba, lru_bx], axis=-1)),
        "lam": row(lru_lam),
        "gm": ((group_of[:, None] == group_of[None, :]).astype(F32) / (GROUP_W // CONF_GROUPS)).astype(BF16),
    }
    w_in_b = w_in[0].astype(BF16)
    g_pre_mix, g_post_mix = row(pre_mix_g), row(post_mix_g)
    g_pre_mlp, g_post_mlp = row(pre_mlp_g), row(post_mlp_g)

    x2 = x.reshape(bsz * seq, dm)
    u_att, h_seg = _in_proj(x2, g_pre_mix, w_in_b, 0)
    for layer in range(depth):
        ya, yb, yd = _mixers(h_seg.reshape(bsz, seq, dm), w_in_b, params, layer)
        to_cast = [(w_out, layer), (mlp_w1, layer), (mlp_w2, layer)] + ([(w_in, layer + 1)] if layer + 1 < depth else [])
        yc, w_out_b, w1_b, w2_b, *w_in_next = _attention(u_att.reshape(bsz, seq, ATT_COL1 - ATT_COL0), to_cast)
        flat = lambda a: a.reshape(bsz * seq, GROUP_W)
        args = (flat(ya), flat(yb), flat(yc), flat(yd), x2, w_out_b, g_post_mix, g_pre_mlp, w1_b, w2_b, g_post_mlp)
        if layer + 1 < depth:
            (w_in_b,) = w_in_next
            x2, u_att, h_seg = _out_mlp(*args, layer, w_in=w_in_b, gnext=g_pre_mix)
        else:
            x2 = _out_mlp(*args, layer)
    return x2.reshape(bsz, seq, dm)
```
